```python
import math
import jax, jax.numpy as jnp
from jax import lax
import numpy as np

D_MODEL = 2048
BATCH = 4
SEQ = 8192
DEPTH = 4

N_BRANCH = 4
BRANCH_WIDTH = 512
EPS = 1e-6
N_MOD = 6
DA_HEADS = 4
DA_QK_DIM = 64
DA_V_DIM = 128
Q_BLOCK = 128
N_BUCKETS = 32
MAX_DISTANCE = 128
ML_HEADS = 4
ML_DIM = 128
ML_CONV = 4
GLA_HEADS = 4
GLA_DK = 64
GLA_DV = 128
GLA_RANK = 16
GLA_TAU = 16.0
CHUNK = 64
S5_CH = 16
S5_GROUPS = BRANCH_WIDTH // S5_CH
S5_STATE = 64
FFN_HIDDEN = -(-(8 * D_MODEL) // (3 * 256)) * 256

DA_QK_W = DA_HEADS * 2 * DA_QK_DIM
DA_V_W = DA_HEADS * DA_V_DIM
ML_W = ML_HEADS * ML_DIM
GLA_K_W = GLA_HEADS * GLA_DK
GLA_V_W = GLA_HEADS * GLA_DV
IN_SPLITS = (DA_QK_W, DA_QK_W, DA_V_W,
             ML_W, ML_W, ML_W, ML_W, ML_HEADS, ML_HEADS,
             GLA_K_W, GLA_K_W, GLA_V_W, GLA_V_W, GLA_RANK,
             BRANCH_WIDTH)
IN_WIDTH = sum(IN_SPLITS)

kernel_name = 'hybrid_gated_parallel_mixer_trunk'


def _rms(x, gain=None):
    xf = x.astype(jnp.float32)
    y = xf * lax.rsqrt(jnp.mean(xf * xf, axis=-1, keepdims=True) + EPS)
    if gain is not None:
        y = y * gain.astype(jnp.float32)
    return y.astype(x.dtype)


def _t5_bucket(rel):
    n = jnp.maximum(-rel, 0)
    exact = N_BUCKETS // 2
    nf = jnp.maximum(n, 1).astype(jnp.float32)
    large = exact + (jnp.log(nf / exact) / math.log(MAX_DISTANCE / exact)
                     * (N_BUCKETS - exact)).astype(jnp.int32)
    return jnp.where(n < exact, n, jnp.minimum(large, N_BUCKETS - 1))


def _causal_conv(x, w):
    k, s = w.shape[0], x.shape[1]
    xp = jnp.pad(x, ((0, 0), (k - 1, 0), (0, 0)))
    y = xp[:, 0:s] * w[0]
    for j in range(1, k):
        y = y + xp[:, j:j + s] * w[j]
    return y


def _to_chunks(t, nc):
    t = t.reshape((t.shape[0], nc, CHUNK) + t.shape[2:])
    return jnp.moveaxis(jnp.swapaxes(t, 2, 3), 1, 0)


def _from_chunks(t):
    t = jnp.swapaxes(jnp.moveaxis(t, 0, 1), 2, 3)
    return t.reshape((t.shape[0], t.shape[1] * t.shape[2], -1))


def _diff_attention(q, k, v, lam, lam_init, rel_bias):
    B, S = q.shape[:2]
    q = q.reshape(B, S, DA_HEADS, 2, DA_QK_DIM).transpose(3, 0, 2, 1, 4)
    k = k.reshape(B, S, DA_HEADS, 2, DA_QK_DIM).transpose(3, 0, 2, 1, 4)
    v = v.reshape(B, S, DA_HEADS, DA_V_DIM).transpose(0, 2, 1, 3)
    kpos = jnp.arange(S)
    scale = DA_QK_DIM ** -0.5

    def block(i):
        start = i * Q_BLOCK
        qb = lax.dynamic_slice_in_dim(q, start, Q_BLOCK, axis=3)
        rel = kpos[None, :] - (start + jnp.arange(Q_BLOCK))[:, None]
        bias = jnp.transpose(rel_bias[_t5_bucket(rel)], (2, 0, 1))
        logits = jnp.einsum('mbhqd,mbhkd->mbhqk', qb, k) * scale + bias
        logits = jnp.where(rel <= 0, logits, -jnp.inf)
        p = jax.nn.softmax(logits, axis=-1)
        return jnp.einsum('bhqk,bhkd->bhqd', p[0] - lam * p[1], v)

    o = lax.map(block, jnp.arange(S // Q_BLOCK))
    o = o.transpose(1, 0, 3, 2, 4).reshape(B, S, DA_HEADS, DA_V_DIM)
    return (_rms(o) * (1.0 - lam_init)).reshape(B, S, DA_V_W)


def _mlstm(q, k, v, ig, lf):
    B, S = q.shape[:2]
    nc = S // CHUNK
    causal = jnp.tril(jnp.ones((CHUNK, CHUNK), bool))

    def step(carry, xs):
        C, n, m = carry
        qc, kc, vc, ic, lfc = xs
        b = jnp.cumsum(lfc, axis=-1)
        Dm = jnp.where(causal, b[..., :, None] - b[..., None, :] + ic[..., None, :], -jnp.inf)
        inter = b + m[..., None]
        m_t = jnp.maximum(inter, jnp.max(Dm, axis=-1))
        s = jnp.einsum('bhtd,bhjd->bhtj', qc, kc) * jnp.exp(Dm - m_t[..., None])
        a = jnp.exp(inter - m_t)
        num = a[..., None] * jnp.einsum('bhtd,bhde->bhte', qc, C) + jnp.einsum('bhtj,bhje->bhte', s, vc)
        den = a * jnp.einsum('bhtd,bhd->bht', qc, n) + jnp.sum(s, axis=-1)
        h = num / jnp.maximum(jnp.abs(den), jnp.exp(-m_t))[..., None]
        m_new = m_t[..., -1]
        a_state = jnp.exp(b[..., -1] + m - m_new)
        wj = jnp.exp(b[..., -1:] - b + ic - m_new[..., None])
        C = a_state[..., None, None] * C + jnp.einsum('bhj,bhjd,bhje->bhde', wj, kc, vc)
        n = a_state[..., None] * n + jnp.einsum('bhj,bhjd->bhd', wj, kc)
        return (C, n, m_new), h

    init = (jnp.zeros((B, ML_HEADS, ML_DIM, ML_DIM), jnp.float32),
            jnp.zeros((B, ML_HEADS, ML_DIM), jnp.float32),
            jnp.zeros((B, ML_HEADS), jnp.float32))
    xs = (_to_chunks(q, nc), _to_chunks(k, nc), _to_chunks(v, nc), _to_chunks(ig, nc), _to_chunks(lf, nc))
    _, h = lax.scan(step, init, xs)
    return _from_chunks(h)


def _gla(q, k, v, la):
    B, S = q.shape[:2]
    nc = S // CHUNK
    causal = jnp.tril(jnp.ones((CHUNK, CHUNK), bool))[:, :, None]

    def step(St, xs):
        qc, kc, vc, lac = xs
        bc = jnp.cumsum(lac, axis=2)
        inter = jnp.einsum('bhtd,bhde->bhte', qc * jnp.exp(bc), St)
        decay = jnp.exp(jnp.where(causal, bc[:, :, :, None, :] - bc[:, :, None, :, :], -jnp.inf))
        att = jnp.einsum('bhtd,bhjd,bhtjd->bhtj', qc, kc, decay)
        o = inter + jnp.einsum('bhtj,bhje->bhte', att, vc)
        last = bc[:, :, -1:, :]
        St = jnp.exp(last[:, :, 0])[..., None] * St + jnp.einsum('bhjd,bhje->bhde', kc * jnp.exp(last - bc), vc)
        return St, o

    init = jnp.zeros((B, GLA_HEADS, GLA_DK, GLA_DV), jnp.float32)
    xs = (_to_chunks(q, nc), _to_chunks(k, nc), _to_chunks(v, nc), _to_chunks(la, nc))
    _, o = lax.scan(step, init, xs)
    return _from_chunks(o).reshape(B, S, GLA_HEADS, GLA_DV)


def _s5(u, a_re, a_im, log_dt, b_re, b_im, c_re, c_im, d_skip):
    f32 = jnp.float32
    a_re, a_im = a_re.astype(f32), a_im.astype(f32)
    dt = jnp.exp(log_dt.astype(f32))[:, None]
    mag = jnp.exp(dt * a_re)
    ab_re, ab_im = mag * jnp.cos(dt * a_im), mag * jnp.sin(dt * a_im)
    nr, ni = ab_re - 1.0, ab_im
    den = a_re * a_re + a_im * a_im
    f_re = (nr * a_re + ni * a_im) / den
    f_im = (ni * a_re - nr * a_im) / den
    b_re, b_im = b_re.astype(f32), b_im.astype(f32)
    bb_re = f_re[..., None] * b_re - f_im[..., None] * b_im
    bb_im = f_re[..., None] * b_im + f_im[..., None] * b_re
    c_re, c_im, d_skip = c_re.astype(f32), c_im.astype(f32), d_skip.astype(f32)

    def combine(e1, e2):
        a1r, a1i, b1r, b1i = e1
        a2r, a2i, b2r, b2i = e2
        return (a2r * a1r - a2i * a1i, a2r * a1i + a2i * a1r,
                a2r * b1r - a2i * b1i + b2r, a2r * b1i + a2i * b1r + b2i)

    def one_seq(us):
        bu_re = jnp.einsum('sgc,gpc->sgp', us, bb_re)
        bu_im = jnp.einsum('sgc,gpc->sgp', us, bb_im)
        ar = jnp.broadcast_to(ab_re, bu_re.shape)
        ai = jnp.broadcast_to(ab_im, bu_re.shape)
        _, _, xr, xi = lax.associative_scan(combine, (ar, ai, bu_re, bu_im), axis=0)
        return (jnp.einsum('sgp,gcp->sgc', xr, c_re) - jnp.einsum('sgp,gcp->sgc', xi, c_im)
                + d_skip * us)

    return lax.map(one_seq, u)


def setup_inputs(seed: int = 0) -> dict:
    key = jax.random.key(seed)
    ks = jax.random.split(key, 32)
    f32 = jnp.float32

    def nrm(k, shape, scale):
        return jax.random.normal(k, shape, f32) * scale

    L, D, W, G, P, C = DEPTH, D_MODEL, BRANCH_WIDTH, S5_GROUPS, S5_STATE, S5_CH
    return {
        'x': nrm(ks[0], (BATCH, SEQ, D), 1.0),
        'c': nrm(ks[1], (BATCH, D), 1.0),
        'ada_w': nrm(ks[2], (L, D, N_MOD * D), 0.5 * D ** -0.5),
        'ada_b': nrm(ks[3], (L, N_MOD * D), 0.01),
        'norm_g': 1.0 + nrm(ks[4], (L, 4, D), 0.01),
        'w_in': nrm(ks[5], (L, D, IN_WIDTH), D ** -0.5),
        'rel_bias': nrm(ks[6], (N_BUCKETS, DA_HEADS), 0.5),
        'diff_lambda': nrm(ks[7], (L, 4, DA_QK_DIM), 0.1),
        'ml_conv': nrm(ks[8], (L, ML_CONV, 2 * ML_W), ML_CONV ** -0.5),
        'ml_gate_b': jnp.stack([nrm(ks[9], (L, ML_HEADS), 0.1),
                                jnp.linspace(3.0, 6.0, ML_HEADS, dtype=f32)[None, :]
                                + nrm(ks[10], (L, ML_HEADS), 0.1)], axis=1),
        'gla_wa2': nrm(ks[11], (L, GLA_RANK, GLA_K_W), GLA_RANK ** -0.5),
        'gla_ba': nrm(ks[12], (L, GLA_K_W), 0.1),
        's5_a_re': -0.5 + nrm(ks[13], (L, G, P), 0.01),
        's5_a_im': math.pi * jnp.arange(P, dtype=f32) + nrm(ks[14], (L, G, P), 0.01),
        's5_log_dt': jax.random.uniform(ks[15], (L, G), f32, math.log(1e-3), math.log(1e-1)),
        's5_b_re': nrm(ks[16], (L, G, P, C), (2 * C) ** -0.5),
        's5_b_im': nrm(ks[17], (L, G, P, C), (2 * C) ** -0.5),
        's5_c_re': nrm(ks[18], (L, G, C, P), (2 * P) ** -0.5),
        's5_c_im': nrm(ks[19], (L, G, C, P), (2 * P) ** -0.5),
        's5_d': nrm(ks[20], (L, G, C), 1.0),
        's5_glu_w': nrm(ks[21], (L, W, W), W ** -0.5),
        's5_glu_b': nrm(ks[22], (L, W), 0.01),
        'w_branch': nrm(ks[23], (L, N_BRANCH, W, D), W ** -0.5),
        'w_gate': nrm(ks[24], (L, N_BRANCH, D, D), D ** -0.5),
        'b_gate': nrm(ks[25], (L, N_BRANCH, D), 0.01),
        'w_out': nrm(ks[26], (L, D, D), D ** -0.5),
        'ffn_w_in': nrm(ks[27], (L, D, 2 * FFN_HIDDEN), D ** -0.5),
        'ffn_w_out': nrm(ks[28], (L, FFN_HIDDEN, D), FFN_HIDDEN ** -0.5),
    }


def reference(x, c, ada_w, ada_b, norm_g, w_in, rel_bias, diff_lambda, ml_conv, ml_gate_b,
              gla_wa2, gla_ba, s5_a_re, s5_a_im, s5_log_dt, s5_b_re, s5_b_im, s5_c_re, s5_c_im,
              s5_d, s5_glu_w, s5_glu_b, w_branch, w_gate, b_gate, w_out, ffn_w_in, ffn_w_out):
    B, S, _ = x.shape
    f32 = jnp.float32
    split_points = [int(p) for p in np.cumsum(IN_SPLITS)[:-1]]
    cs = jax.nn.silu(c)
    rel_bias32 = rel_bias.astype(f32)
    for l in range(DEPTH):
        mod = (cs @ ada_w[l] + ada_b[l]).reshape(B, N_MOD, D_MODEL)[:, :, None, :]
        shift_m, scale_m, gate_m, shift_f, scale_f, gate_f = (mod[:, i] for i in range(N_MOD))

        h = _rms(x, norm_g[l, 0]) * (1.0 + scale_m) + shift_m
        proj = (h @ w_in[l]).astype(f32)
        (da_q, da_k, da_v, ml_q, ml_k, ml_v, ml_o, ml_i, ml_f,
         gl_q, gl_k, gl_v, gl_r, gl_a, s5_u) = jnp.split(proj, split_points, axis=-1)

        lam_init = 0.8 - 0.6 * math.exp(-0.3 * l)
        lp = diff_lambda[l].astype(f32)
        lam = jnp.exp(jnp.sum(lp[0] * lp[1])) - jnp.exp(jnp.sum(lp[2] * lp[3])) + lam_init
        o_a = _diff_attention(da_q, da_k, da_v, lam, lam_init, rel_bias32)

        qk = jax.nn.silu(_causal_conv(jnp.concatenate([ml_q, ml_k], axis=-1), ml_conv[l].astype(f32)))
        mq, mk = jnp.split(qk, 2, axis=-1)
        gb = ml_gate_b[l].astype(f32)
        hm = _mlstm(mq.reshape(B, S, ML_HEADS, ML_DIM),
                    mk.reshape(B, S, ML_HEADS, ML_DIM) * ML_DIM ** -0.5,
                    ml_v.reshape(B, S, ML_HEADS, ML_DIM),
                    ml_i + gb[0], jax.nn.log_sigmoid(ml_f + gb[1]))
        o_b = jax.nn.sigmoid(ml_o) * hm

        la = jax.nn.log_sigmoid(gl_a @ gla_wa2[l].astype(f32) + gla_ba[l].astype(f32)) / GLA_TAU
        og = _gla(gl_q.reshape(B, S, GLA_HEADS, GLA_DK) * GLA_DK ** -0.5,
                  gl_k.reshape(B, S, GLA_HEADS, GLA_DK),
                  gl_v.reshape(B, S, GLA_HEADS, GLA_DV),
                  la.reshape(B, S, GLA_HEADS, GLA_DK))
        o_c = _rms(og).reshape(B, S, GLA_V_W) * jax.nn.silu(gl_r)

        y = _s5(s5_u.reshape(B, S, S5_GROUPS, S5_CH), s5_a_re[l], s5_a_im[l], s5_log_dt[l],
                s5_b_re[l], s5_b_im[l], s5_c_re[l], s5_c_im[l], s5_d[l]).reshape(B, S, BRANCH_WIDTH)
        z = jax.nn.gelu(y)
        o_d = z * jax.nn.sigmoid(z @ s5_glu_w[l].astype(f32) + s5_glu_b[l].astype(f32))

        branches = (o_a, o_b, o_c, o_d)
        merged = [jax.nn.sigmoid(h @ w_gate[l, i] + b_gate[l, i]) * (branches[i].astype(h.dtype) @ w_branch[l, i])
                  for i in range(N_BRANCH)]
        mix = (merged[0] + merged[1] + merged[2] + merged[3]) @ w_out[l]
        x = x + (gate_m * _rms(mix, norm_g[l, 1])).astype(x.dtype)

        h = _rms(x, norm_g[l, 2]) * (1.0 + scale_f) + shift_f
        a, g = jnp.split(h @ ffn_w_in[l], 2, axis=-1)
        x = x + (gate_f * _rms((jax.nn.silu(a) * g) @ ffn_w_out[l], norm_g[l, 3])).astype(x.dtype)
    return x
```

```python
import functools
import math

import numpy as np
import jax
import jax.numpy as jnp
from jax import lax
from jax.experimental import pallas as pl
from jax.experimental.pallas import tpu as pltpu

F32 = jnp.float32
BF16 = jnp.bfloat16
HIGHEST = lax.Precision.HIGHEST

D_MODEL = 2048
DEPTH = 4
EPS = 1e-6
N_MOD = 6
N_BRANCH = 4
BRANCH_WIDTH = 512
DA_HEADS = 4
DA_QK_DIM = 64
DA_V_DIM = 128
N_BUCKETS = 32
MAX_DISTANCE = 128
ML_HEADS = 4
ML_DIM = 128
ML_CONV = 4
GLA_HEADS = 4
GLA_DK = 64
GLA_DV = 128
GLA_RANK = 16
GLA_TAU = 16.0
S5_CH = 16
S5_GROUPS = BRANCH_WIDTH // S5_CH
S5_STATE = 64
FFN_HIDDEN = -(-(8 * D_MODEL) // (3 * 256)) * 256

LANES = 128
SUBLANES = 8
VMEM_LIMIT = 56 * 1024 * 1024

MAIN_W = 5632
SMALL_W = LANES
OFF_DA_Q, OFF_DA_K, OFF_DA_V = 0, 512, 1024
OFF_ML_Q, OFF_ML_K, OFF_ML_V, OFF_ML_O = 1536, 2048, 2560, 3072
OFF_GL_Q, OFF_GL_K, OFF_GL_V, OFF_GL_R = 3584, 3840, 4096, 4608
OFF_S5_U = 5120
SM_ML_I, SM_ML_F, SM_GL_A = 0, 4, 8

NEG = -1e30

TM_PROJ, TN_PROJ = 1024, 512
TQ_ATT = 256
L_MLSTM = 256
L_GLA, C_GLA = 256, 16
TM_S5 = 256
TM_MERGE, TN_MERGE = 512, 256
TM_FFN, TH_FFN = 512, 512


def _cparams(sem):
    return pltpu.CompilerParams(dimension_semantics=sem, vmem_limit_bytes=VMEM_LIMIT)


def _rms(x):
    return x * lax.rsqrt(jnp.mean(x * x, axis=-1, keepdims=True) + EPS)


def _sigmoid(x):
    return 1.0 / (1.0 + jnp.exp(-x))


def _silu(x):
    return x * _sigmoid(x)


def _log_sigmoid(x):
    return jnp.minimum(x, 0.0) - jnp.log1p(jnp.exp(-jnp.abs(x)))


def _dot(a, b):
    return jnp.dot(a, b, preferred_element_type=F32)


def _dot_nt(a, b):
    return lax.dot_general(a, b, (((1,), (1,)), ((), ())), preferred_element_type=F32)


def _dot_tn(a, b):
    return lax.dot_general(a, b, (((0,), (0,)), ((), ())), preferred_element_type=F32)


def _dot_exact(a, b):
    return jnp.dot(a, b, preferred_element_type=F32, precision=HIGHEST)


def _adaln_kernel(c_ref, w_ref, b_ref, o_ref):
    c = c_ref[...]
    o_ref[...] = _dot_exact(_silu(c), w_ref[...]) + b_ref[...]


def _adaln(c_pad, ada_w, ada_b):
    depth, d, n = ada_w.shape
    rows = c_pad.shape[0]
    tn = 1024
    return pl.pallas_call(
        _adaln_kernel,
        grid=(depth, n // tn),
        in_specs=[
            pl.BlockSpec((rows, d), lambda l, j: (0, 0)),
            pl.BlockSpec((None, d, tn), lambda l, j: (l, 0, j)),
            pl.BlockSpec((None, 1, tn), lambda l, j: (l, 0, j)),
        ],
        out_specs=pl.BlockSpec((None, rows, tn), lambda l, j: (l, 0, j)),
        out_shape=jax.ShapeDtypeStruct((depth, rows, n), F32),
        compiler_params=_cparams(("parallel", "parallel")),
        name="adaln",
    )(c_pad, ada_w, ada_b.reshape(depth, 1, n))


def _modulated_norm(x, gain, shift, scale):
    return (_rms(x) * gain) * (1.0 + scale) + shift


def _proj_kernel(x_ref, mod_ref, g_ref, w_ref, ws_ref, o_ref, os_ref, h_scr):
    j = pl.program_id(1)

    @pl.when(j == 0)
    def _():
        h = _modulated_norm(x_ref[...], g_ref[...], mod_ref[0:1, :], mod_ref[1:2, :])
        hb = h.astype(BF16)
        h_scr[...] = hb
        os_ref[...] = _dot(hb, ws_ref[...])

    o_ref[...] = _dot(h_scr[...], w_ref[...]).astype(o_ref.dtype)


def _proj(x2, mod_l, gain, w_main, w_small, seq, out_dtype):
    t, d = x2.shape
    tm, tn = min(TM_PROJ, seq), TN_PROJ
    tiles_per_seq = seq // tm
    return pl.pallas_call(
        _proj_kernel,
        grid=(t // tm, MAIN_W // tn),
        in_specs=[
            pl.BlockSpec((tm, d), lambda i, j: (i, 0)),
            pl.BlockSpec((None, SUBLANES, d), lambda i, j: (i // tiles_per_seq, 0, 0)),
            pl.BlockSpec((1, d), lambda i, j: (0, 0)),
            pl.BlockSpec((d, tn), lambda i, j: (0, j)),
            pl.BlockSpec((d, SMALL_W), lambda i, j: (0, 0)),
        ],
        out_specs=[
            pl.BlockSpec((tm, tn), lambda i, j: (i, j)),
            pl.BlockSpec((tm, SMALL_W), lambda i, j: (i, 0)),
        ],
        out_shape=[
            jax.ShapeDtypeStruct((t, MAIN_W), out_dtype),
            jax.ShapeDtypeStruct((t, SMALL_W), F32),
        ],
        scratch_shapes=[pltpu.VMEM((tm, d), BF16)],
        compiler_params=_cparams(("parallel", "arbitrary")),
        name="proj",
    )(x2, mod_l, gain, w_main, w_small)


def _attn_kernel(lam_ref, q_ref, k_ref, v_ref, bias_ref, o_ref, *, tq, out_scale):
    i = pl.program_id(2)
    lam = lam_ref[0]
    q = q_ref[...].astype(F32) * (DA_QK_DIM ** -0.5)
    lane = lax.broadcasted_iota(jnp.int32, q.shape, 1)
    qa = jnp.where(lane < DA_QK_DIM, q, 0.0).astype(BF16)
    qb = jnp.where(lane >= DA_QK_DIM, q, 0.0).astype(BF16)

    def one_map(qm, kt, vt, bias, m, l, acc):
        s = _dot_nt(qm, kt)
        if bias is not None:
            s = s + bias
        m_new = jnp.maximum(m, jnp.max(s, axis=-1, keepdims=True))
        p = jnp.exp(s - m_new)
        alpha = jnp.exp(m - m_new)
        l = alpha * l + jnp.sum(p, axis=-1, keepdims=True)
        acc = alpha * acc + _dot(p.astype(BF16), vt)
        return m_new, l, acc

    def tile(j, bias, carry):
        r0 = pl.multiple_of(j * tq, tq)
        kt = k_ref[pl.ds(r0, tq), :].astype(BF16)
        vt = v_ref[pl.ds(r0, tq), :].astype(BF16)
        m1, l1, a1, m2, l2, a2 = carry
        m1, l1, a1 = one_map(qa, kt, vt, bias, m1, l1, a1)
        m2, l2, a2 = one_map(qb, kt, vt, bias, m2, l2, a2)
        return m1, l1, a1, m2, l2, a2

    col = jnp.full((tq, 1), NEG, F32)
    zc = jnp.zeros((tq, 1), F32)
    za = jnp.zeros((tq, DA_V_DIM), F32)
    carry = (col, zc, za, col, zc, za)
    carry = tile(i, bias_ref[1], carry)
    prev_idx = jnp.where(i >= 1, 0, 2)
    carry = tile(jnp.maximum(i - 1, 0), bias_ref[prev_idx], carry)
    carry = lax.fori_loop(0, jnp.maximum(i - 1, 0), lambda j, c: tile(j, None, c), carry)
    m1, l1, a1, m2, l2, a2 = carry
    o = a1 / l1 - lam * (a2 / l2)
    o_ref[...] = (_rms(o) * out_scale).astype(o_ref.dtype)


def _attention(lam, pm, bias_tiles, batch, seq, lam_init, out_dtype):
    tq = min(TQ_ATT, seq)
    nq = seq // tq
    kern = functools.partial(_attn_kernel, tq=tq, out_scale=1.0 - lam_init)
    qb, kb, vb = OFF_DA_Q // LANES, OFF_DA_K // LANES, OFF_DA_V // LANES
    return pl.pallas_call(
        kern,
        grid=(batch, DA_HEADS, nq),
        in_specs=[
            pl.BlockSpec(memory_space=pltpu.SMEM),
            pl.BlockSpec((tq, LANES), lambda b, h, i: (b * nq + i, qb + h)),
            pl.BlockSpec((seq, LANES), lambda b, h, i: (b, kb + h)),
            pl.BlockSpec((seq, LANES), lambda b, h, i: (b, vb + h)),
            pl.BlockSpec((None, 3, tq, tq), lambda b, h, i: (h, 0, 0, 0)),
        ],
        out_specs=pl.BlockSpec((tq, LANES), lambda b, h, i: (b * nq + i, h)),
        out_shape=jax.ShapeDtypeStruct((batch * seq, DA_HEADS * DA_V_DIM), out_dtype),
        compiler_params=_cparams(("parallel", "parallel", "arbitrary")),
        name="diff_attn",
    )(lam, pm, pm, pm, bias_tiles)


def _t5_bucket_table(n_max):
    n = np.arange(n_max)
    exact = N_BUCKETS // 2
    nf = np.maximum(n, 1).astype(np.float64)
    large = exact + (np.log(nf / exact) / math.log(MAX_DISTANCE / exact)
                     * (N_BUCKETS - exact)).astype(np.int64)
    return np.where(n < exact, n, np.minimum(large, N_BUCKETS - 1)).astype(np.int32)


def _bias_tiles(rel_bias, tq):
    assert tq >= MAX_DISTANCE
    bucket = _t5_bucket_table(2 * tq)
    rb = rel_bias.astype(F32)
    tbl = rb[bucket] - rb[N_BUCKETS - 1][None, :]
    r = np.arange(tq)[:, None]
    c = np.arange(tq)[None, :]
    prev = jnp.transpose(tbl[tq + r - c], (2, 0, 1))
    diag = jnp.transpose(tbl[np.maximum(r - c, 0)], (2, 0, 1))
    diag = jnp.where(jnp.asarray(r >= c)[None], diag, NEG)
    masked = jnp.full_like(prev, NEG)
    return jnp.stack([prev, diag, masked], axis=1)


def _mlstm_kernel(q_ref, k_ref, v_ref, og_ref, sm_ref, qh_ref, kh_ref, cw_ref, gb_ref,
                  tril_ref, out_ref, xq_scr, xk_scr, c_scr, n_scr, m_scr, *, L):
    ci = pl.program_id(1)

    @pl.when(ci == 0)
    def _():
        c_scr[...] = jnp.zeros_like(c_scr)
        n_scr[...] = jnp.zeros_like(n_scr)
        m_scr[...] = jnp.zeros_like(m_scr)

    halo = SUBLANES
    keep = (ci > 0).astype(F32)
    xq_scr[0:halo, :] = qh_ref[...].astype(F32) * keep
    xk_scr[0:halo, :] = kh_ref[...].astype(F32) * keep
    xq_scr[halo:halo + L, :] = q_ref[...].astype(F32)
    xk_scr[halo:halo + L, :] = k_ref[...].astype(F32)

    def conv_silu(scr, w):
        y = scr[halo:halo + L, :] * w[ML_CONV - 1:ML_CONV, :]
        for j in range(ML_CONV - 1):
            off = halo - (ML_CONV - 1) + j
            y = y + scr[off:off + L, :] * w[j:j + 1, :]
        return _silu(y)

    w_all = cw_ref[...]
    width = ML_HEADS * ML_DIM
    q_all = conv_silu(xq_scr, w_all[:, 0:width])
    k_all = conv_silu(xk_scr, w_all[:, width:2 * width]) * (ML_DIM ** -0.5)

    g = sm_ref[...] + gb_ref[...]
    lane = lax.broadcasted_iota(jnp.int32, g.shape, 1)
    is_f = (lane >= SM_ML_F) & (lane < SM_ML_F + ML_HEADS)
    g = jnp.where(is_f, _log_sigmoid(g), g)
    bcum = _dot_exact(tril_ref[...], g)
    g_t = g.T
    b_t = bcum.T
    row = lax.broadcasted_iota(jnp.int32, (L, L), 0)
    colj = lax.broadcasted_iota(jnp.int32, (L, L), 1)
    causal = colj <= row

    for h in range(ML_HEADS):
        sl = slice(h * ML_DIM, (h + 1) * ML_DIM)
        qh = q_all[:, sl]
        kh = k_all[:, sl]
        vh = v_ref[:, sl].astype(F32)
        qb, kb, vb = qh.astype(BF16), kh.astype(BF16), vh.astype(BF16)
        ig_col = g[:, SM_ML_I + h:SM_ML_I + h + 1]
        b_col = bcum[:, SM_ML_F + h:SM_ML_F + h + 1]
        ig_row = g_t[SM_ML_I + h:SM_ML_I + h + 1, :]
        b_row = b_t[SM_ML_F + h:SM_ML_F + h + 1, :]
        m_old = m_scr[h][:, 0:1]
        c_old = c_scr[h]
        n_old = n_scr[h]

        dm = jnp.where(causal, b_col - b_row + ig_row, NEG)
        inter = b_col + m_old
        m_t = jnp.maximum(inter, jnp.max(dm, axis=-1, keepdims=True))
        s = _dot_nt(qb, kb) * jnp.exp(dm - m_t)
        a = jnp.exp(inter - m_t)
        num = a * _dot(qb, c_old.astype(BF16)) + _dot(s.astype(BF16), vb)
        den = (a * jnp.sum(qh * n_old, axis=-1, keepdims=True)
               + jnp.sum(s, axis=-1, keepdims=True))
        hv = num / jnp.maximum(jnp.abs(den), jnp.exp(-m_t))
        out_ref[:, sl] = (_sigmoid(og_ref[:, sl].astype(F32)) * hv).astype(out_ref.dtype)

        m_new = m_t[L - 1:L, :]
        b_last = b_col[L - 1:L, :]
        a_state = jnp.exp(b_last + m_old - m_new)
        w_col = jnp.exp(b_last - b_col + ig_col - m_new)
        kw = kh * w_col
        c_scr[h] = a_state * c_old + _dot_tn(kw.astype(BF16), vb)
        n_scr[h] = a_state * n_old + jnp.sum(kw, axis=0, keepdims=True)
        m_scr[h] = jnp.broadcast_to(m_new, (1, LANES))


def _mlstm(pm, ps, conv_w, gate_row, batch, seq, out_dtype):
    L = min(L_MLSTM, seq)
    nc = seq // L
    width = ML_HEADS * ML_DIM
    qb, kb, vb, ob = (OFF_ML_Q // width, OFF_ML_K // width, OFF_ML_V // width, OFF_ML_O // width)
    lb = L // SUBLANES
    tril = jnp.asarray(np.tril(np.ones((L, L), np.float32)))

    def halo_map(colblk):
        return lambda b, c: (jnp.maximum(b * (seq // SUBLANES) + c * lb - 1, 0), colblk)

    return pl.pallas_call(
        functools.partial(_mlstm_kernel, L=L),
        grid=(batch, nc),
        in_specs=[
            pl.BlockSpec((L, width), lambda b, c: (b * nc + c, qb)),
            pl.BlockSpec((L, width), lambda b, c: (b * nc + c, kb)),
            pl.BlockSpec((L, width), lambda b, c: (b * nc + c, vb)),
            pl.BlockSpec((L, width), lambda b, c: (b * nc + c, ob)),
            pl.BlockSpec((L, SMALL_W), lambda b, c: (b * nc + c, 0)),
            pl.BlockSpec((SUBLANES, width), halo_map(qb)),
            pl.BlockSpec((SUBLANES, width), halo_map(kb)),
            pl.BlockSpec((ML_CONV, 2 * width), lambda b, c: (0, 0)),
            pl.BlockSpec((1, SMALL_W), lambda b, c: (0, 0)),
            pl.BlockSpec((L, L), lambda b, c: (0, 0)),
        ],
        out_specs=pl.BlockSpec((L, width), lambda b, c: (b * nc + c, 0)),
        out_shape=jax.ShapeDtypeStruct((batch * seq, width), out_dtype),
        scratch_shapes=[
            pltpu.VMEM((L + SUBLANES, width), F32),
            pltpu.VMEM((L + SUBLANES, width), F32),
            pltpu.VMEM((ML_HEADS, ML_DIM, ML_DIM), F32),
            pltpu.VMEM((ML_HEADS, 1, ML_DIM), F32),
            pltpu.VMEM((ML_HEADS, 1, LANES), F32),
        ],
        compiler_params=_cparams(("parallel", "arbitrary")),
        name="mlstm",
    )(pm, pm, pm, pm, ps, pm, pm, conv_w, gate_row, tril)


def _gla_kernel(q_ref, k_ref, v_ref, r_ref, sm_ref, wa_ref, ba_ref, tril_ref, mexp_ref,
                hmask_ref, out_ref, bc_scr, a_scr, st_scr, *, L, c):
    ci = pl.program_id(1)

    @pl.when(ci == 0)
    def _():
        st_scr[...] = jnp.zeros_like(st_scr)

    la = _log_sigmoid(_dot_exact(sm_ref[...], wa_ref[...]) + ba_ref[...]) * (1.0 / GLA_TAU)
    bc_scr[...] = _dot_exact(tril_ref[...], la)
    rowc = lax.broadcasted_iota(jnp.int32, (c, GLA_HEADS * GLA_DK), 0)

    def sub(i, carry):
        r0 = pl.multiple_of(i * c, c)
        qs = q_ref[pl.ds(r0, c), :].astype(F32) * (GLA_DK ** -0.5)
        ks = k_ref[pl.ds(r0, c), :].astype(F32)
        vs = v_ref[pl.ds(r0, c), :].astype(BF16).astype(F32)
        bcs = bc_scr[pl.ds(r0, c), :]
        e_end = bcs[c - 1:c, :]
        st = st_scr[...]
        o = _dot_nt((qs * jnp.exp(bcs)).astype(BF16), st.astype(BF16))

        for t in range(c):
            dec = jnp.exp(jnp.minimum(bcs[t:t + 1, :] - bcs, 0.0))
            a_t = jnp.where(rowc <= t, qs[t:t + 1, :] * ks * dec, 0.0)
            a_scr[t * c:(t + 1) * c, :] = a_t.astype(BF16)
        p = _dot(a_scr[...], mexp_ref[...])
        o = o + jnp.sum(p.reshape(c, c, GLA_HEADS * GLA_DV) * vs[None, :, :], axis=1)

        outs = [_rms(o[:, h * GLA_DV:(h + 1) * GLA_DV]) for h in range(GLA_HEADS)]
        on = jnp.concatenate(outs, axis=1)
        out_ref[pl.ds(r0, c), :] = (on * _silu(r_ref[pl.ds(r0, c), :].astype(F32))).astype(out_ref.dtype)

        khat = (ks * jnp.exp(e_end - bcs)).astype(BF16)
        upd = _dot_tn(vs.astype(BF16), khat)
        st_scr[...] = st * jnp.exp(e_end) + upd * hmask_ref[...]
        return carry

    lax.fori_loop(0, L // c, sub, 0)


def _gla(pm, ps, wa_pad, ba_row, batch, seq, out_dtype):
    L = min(L_GLA, seq)
    c = C_GLA
    nc = seq // L
    kw, vw = GLA_HEADS * GLA_DK, GLA_HEADS * GLA_DV
    qb, kb, vb, rb = OFF_GL_Q // kw, OFF_GL_K // kw, OFF_GL_V // vw, OFF_GL_R // vw
    idx = np.arange(L)
    tril = ((idx[:, None] >= idx[None, :]) & (idx[:, None] // c == idx[None, :] // c))
    tril = jnp.asarray(tril.astype(np.float32))
    mexp = np.zeros((kw, vw), np.float32)
    for h in range(GLA_HEADS):
        mexp[h * GLA_DK:(h + 1) * GLA_DK, h * GLA_DV:(h + 1) * GLA_DV] = 1.0
    hmask = jnp.asarray(mexp.T)
    mexp = jnp.asarray(mexp, dtype=BF16)
    return pl.pallas_call(
        functools.partial(_gla_kernel, L=L, c=c),
        grid=(batch, nc),
        in_specs=[
            pl.BlockSpec((L, kw), lambda b, i: (b * nc + i, qb)),
            pl.BlockSpec((L, kw), lambda b, i: (b * nc + i, kb)),
            pl.BlockSpec((L, vw), lambda b, i: (b * nc + i, vb)),
            pl.BlockSpec((L, vw), lambda b, i: (b * nc + i, rb)),
            pl.BlockSpec((L, SMALL_W), lambda b, i: (b * nc + i, 0)),
            pl.BlockSpec((SMALL_W, kw), lambda b, i: (0, 0)),
            pl.BlockSpec((1, kw), lambda b, i: (0, 0)),
            pl.BlockSpec((L, L), lambda b, i: (0, 0)),
            pl.BlockSpec((kw, vw), lambda b, i: (0, 0)),
            pl.BlockSpec((vw, kw), lambda b, i: (0, 0)),
        ],
        out_specs=pl.BlockSpec((L, vw), lambda b, i: (b * nc + i, 0)),
        out_shape=jax.ShapeDtypeStruct((batch * seq, vw), out_dtype),
        scratch_shapes=[
            pltpu.VMEM((L, kw), F32),
            pltpu.VMEM((c * c, kw), BF16),
            pltpu.VMEM((vw, kw), F32),
        ],
        compiler_params=_cparams(("parallel", "arbitrary")),
        name="gla",
    )(pm, pm, pm, pm, ps, wa_pad, ba_row, tril, mexp, hmask)


S5_NSTATE = S5_GROUPS * S5_STATE
S5_BLK = 4
S5_BLK_STATE = S5_NSTATE // S5_BLK


def _gelu_tanh(x):
    return 0.5 * x * (1.0 + jnp.tanh(math.sqrt(2.0 / math.pi) * (x + 0.044715 * (x * x * x))))


def _s5_kernel(u_ref, bre_ref, bim_ref, cre_ref, cim_ref, as_ref, pw_ref, d_ref, gw_ref,
               gb_ref, out_ref, xr_scr, xi_scr, cr_scr, ci_scr, *, tm):
    ti = pl.program_id(1)

    @pl.when(ti == 0)
    def _():
        cr_scr[...] = jnp.zeros_like(cr_scr)
        ci_scr[...] = jnp.zeros_like(ci_scr)

    u = u_ref[...].astype(F32)
    ub = u.astype(BF16)
    nb = S5_BLK_STATE
    for q in range(S5_BLK):
        uq = ub[:, q * LANES:(q + 1) * LANES]
        xr_scr[:, q * nb:(q + 1) * nb] = _dot(uq, bre_ref[q])
        xi_scr[:, q * nb:(q + 1) * nb] = _dot(uq, bim_ref[q])

    rowi = lax.broadcasted_iota(jnp.int32, (SUBLANES, nb), 0)
    for cc in range(S5_BLK):
        cols = slice(cc * nb, (cc + 1) * nb)
        a_r = as_ref[0, :, cols]
        a_i = as_ref[1, :, cols]
        p_r = pw_ref[0, :, cols]
        p_i = pw_ref[1, :, cols]

        def body(g, carry, cols=cols, a_r=a_r, a_i=a_i, p_r=p_r, p_i=p_i):
            cr, ci = carry
            r0 = pl.multiple_of(g * SUBLANES, SUBLANES)
            zr = xr_scr[pl.ds(r0, SUBLANES), cols]
            zi = xi_scr[pl.ds(r0, SUBLANES), cols]
            for si, s in enumerate((1, 2, 4)):
                sr = jnp.where(rowi >= s, pltpu.roll(zr, s, 0), 0.0)
                sim = jnp.where(rowi >= s, pltpu.roll(zi, s, 0), 0.0)
                ar = a_r[si:si + 1, :]
                ai = a_i[si:si + 1, :]
                zr, zi = zr + ar * sr - ai * sim, zi + ar * sim + ai * sr
            xr = zr + p_r * cr - p_i * ci
            xi = zi + p_r * ci + p_i * cr
            xr_scr[pl.ds(r0, SUBLANES), cols] = xr
            xi_scr[pl.ds(r0, SUBLANES), cols] = xi
            return xr[SUBLANES - 1:SUBLANES, :], xi[SUBLANES - 1:SUBLANES, :]

        cr, ci = lax.fori_loop(0, tm // SUBLANES, body, (cr_scr[:, cols], ci_scr[:, cols]))
        cr_scr[:, cols] = cr
        ci_scr[:, cols] = ci

    ys = []
    for q in range(S5_BLK):
        xr = xr_scr[:, q * nb:(q + 1) * nb].astype(BF16)
        xi = xi_scr[:, q * nb:(q + 1) * nb].astype(BF16)
        ys.append(_dot(xr, cre_ref[q]) + _dot(xi, cim_ref[q]))
    y = jnp.concatenate(ys, axis=1) + d_ref[...] * u
    z = _gelu_tanh(y)
    gate = _sigmoid(_dot(z.astype(BF16), gw_ref[...]) + gb_ref[...])
    out_ref[...] = (z * gate).astype(out_ref.dtype)


def _s5_params(a_re, a_im, log_dt, b_re, b_im, c_re, c_im):
    a_re, a_im = a_re.astype(F32), a_im.astype(F32)
    dt = jnp.exp(log_dt.astype(F32))[:, None]
    mag = jnp.exp(dt * a_re)
    ab_re, ab_im = mag * jnp.cos(dt * a_im), mag * jnp.sin(dt * a_im)
    nr, ni = ab_re - 1.0, ab_im
    den = a_re * a_re + a_im * a_im
    f_re = (nr * a_re + ni * a_im) / den
    f_im = (ni * a_re - nr * a_im) / den
    b_re, b_im = b_re.astype(F32), b_im.astype(F32)
    bb_re = f_re[..., None] * b_re - f_im[..., None] * b_im
    bb_im = f_re[..., None] * b_im + f_im[..., None] * b_re

    def apow(k):
        mk = jnp.exp(k * dt * a_re)
        return (mk * jnp.cos(k * dt * a_im)).reshape(-1), (mk * jnp.sin(k * dt * a_im)).reshape(-1)

    zero = jnp.zeros((S5_NSTATE,), F32)
    steps = [apow(float(s)) for s in (1, 2, 4)]
    as_arr = jnp.stack([jnp.stack([s[0] for s in steps] + [zero] * 5),
                        jnp.stack([s[1] for s in steps] + [zero] * 5)])
    pws = [apow(float(k + 1)) for k in range(SUBLANES)]
    pw_arr = jnp.stack([jnp.stack([p[0] for p in pws]), jnp.stack([p[1] for p in pws])])

    gpb = S5_GROUPS // S5_BLK
    eye = jnp.eye(gpb, dtype=F32)

    def pack_b(bb):
        bb = bb.reshape(S5_BLK, gpb, S5_STATE, S5_CH)
        return jnp.einsum('qgpc,gh->qgchp', bb, eye).reshape(S5_BLK, gpb * S5_CH, gpb * S5_STATE)

    def pack_c(cc):
        cc = cc.reshape(S5_BLK, gpb, S5_CH, S5_STATE)
        return jnp.einsum('qgcp,gh->qgphc', cc, eye).reshape(S5_BLK, gpb * S5_STATE, gpb * S5_CH)

    return (pack_b(bb_re).astype(BF16), pack_b(bb_im).astype(BF16),
            pack_c(c_re.astype(F32)).astype(BF16), pack_c(-c_im.astype(F32)).astype(BF16),
            as_arr, pw_arr)


def _s5(pm, packed, d_row, glu_w, glu_b, batch, seq, out_dtype):
    tm = min(TM_S5, seq)
    nt = seq // tm
    bre, bim, cre, cim, as_arr, pw_arr = packed
    w = BRANCH_WIDTH
    ub = OFF_S5_U // w
    full = lambda *shape: pl.BlockSpec(shape, lambda b, i: (0,) * len(shape))
    return pl.pallas_call(
        functools.partial(_s5_kernel, tm=tm),
        grid=(batch, nt),
        in_specs=[
            pl.BlockSpec((tm, w), lambda b, i: (b * nt + i, ub)),
            full(S5_BLK, LANES, S5_BLK_STATE), full(S5_BLK, LANES, S5_BLK_STATE),
            full(S5_BLK, S5_BLK_STATE, LANES), full(S5_BLK, S5_BLK_STATE, LANES),
            full(2, SUBLANES, S5_NSTATE), full(2, SUBLANES, S5_NSTATE),
            full(1, w), full(w, w), full(1, w),
        ],
        out_specs=pl.BlockSpec((tm, w), lambda b, i: (b * nt + i, 0)),
        out_shape=jax.ShapeDtypeStruct((batch * seq, w), out_dtype),
        scratch_shapes=[
            pltpu.VMEM((tm, S5_NSTATE), F32),
            pltpu.VMEM((tm, S5_NSTATE), F32),
            pltpu.VMEM((1, S5_NSTATE), F32),
            pltpu.VMEM((1, S5_NSTATE), F32),
        ],
        compiler_params=_cparams(("parallel", "arbitrary")),
        name="s5",
    )(pm, bre, bim, cre, cim, as_arr, pw_arr, d_row, glu_w, glu_b)


def _merge_kernel(x_ref, mod_ref, g_ref, oa_ref, ob_ref, oc_ref, od_ref, wg_ref, bg_ref,
                  wb_ref, wo_ref, out_ref, h_scr, acc_scr):
    n = pl.program_id(1)

    @pl.when(n == 0)
    def _():
        h = _modulated_norm(x_ref[...], g_ref[0:1, :], mod_ref[0:1, :], mod_ref[1:2, :])
        h_scr[...] = h.astype(BF16)
        acc_scr[...] = jnp.zeros_like(acc_scr)

    hb = h_scr[...]
    merged = None
    for i, o_ref in enumerate((oa_ref, ob_ref, oc_ref, od_ref)):
        gate = _sigmoid(_dot(hb, wg_ref[i]) + bg_ref[i])
        term = gate * _dot(o_ref[...].astype(BF16), wb_ref[i])
        merged = term if merged is None else merged + term
    acc_scr[...] += _dot(merged.astype(BF16), wo_ref[...])

    @pl.when(n == pl.num_programs(1) - 1)
    def _():
        y = _rms(acc_scr[...]) * g_ref[1:2, :]
        out_ref[...] = x_ref[...] + mod_ref[2:3, :] * y


def _merge(x2, mod_l, gains, oa, ob, oc, od, w_gate, b_gate, w_branch, w_out, seq):
    t, d = x2.shape
    tm, tn = min(TM_MERGE, seq), TN_MERGE
    tiles_per_seq = seq // tm
    w = BRANCH_WIDTH
    br_spec = pl.BlockSpec((tm, w), lambda i, n: (i, 0))
    return pl.pallas_call(
        _merge_kernel,
        grid=(t // tm, d // tn),
        in_specs=[
            pl.BlockSpec((tm, d), lambda i, n: (i, 0)),
            pl.BlockSpec((None, SUBLANES, d), lambda i, n: (i // tiles_per_seq, 0, 0)),
            pl.BlockSpec((2, d), lambda i, n: (0, 0)),
            br_spec, br_spec, br_spec, br_spec,
            pl.BlockSpec((N_BRANCH, d, tn), lambda i, n: (0, 0, n)),
            pl.BlockSpec((N_BRANCH, 1, tn), lambda i, n: (0, 0, n)),
            pl.BlockSpec((N_BRANCH, w, tn), lambda i, n: (0, 0, n)),
            pl.BlockSpec((tn, d), lambda i, n: (n, 0)),
        ],
        out_specs=pl.BlockSpec((tm, d), lambda i, n: (i, 0)),
        out_shape=jax.ShapeDtypeStruct((t, d), F32),
        scratch_shapes=[pltpu.VMEM((tm, d), BF16), pltpu.VMEM((tm, d), F32)],
        compiler_params=_cparams(("parallel", "arbitrary")),
        name="merge",
    )(x2, mod_l, gains, oa, ob, oc, od, w_gate, b_gate, w_branch, w_out)


def _ffn_kernel(x_ref, mod_ref, g_ref, wa_ref, wg_ref, wo_ref, out_ref, h_scr, acc_scr):
    j = pl.program_id(1)

    @pl.when(j == 0)
    def _():
        h = _modulated_norm(x_ref[...], g_ref[0:1, :], mod_ref[3:4, :], mod_ref[4:5, :])
        h_scr[...] = h.astype(BF16)
        acc_scr[...] = jnp.zeros_like(acc_scr)

    hb = h_scr[...]
    a = _dot(hb, wa_ref[...])
    g = _dot(hb, wg_ref[...])
    acc_scr[...] += _dot((_silu(a) * g).astype(BF16), wo_ref[...])

    @pl.when(j == pl.num_programs(1) - 1)
    def _():
        y = _rms(acc_scr[...]) * g_ref[1:2, :]
        out_ref[...] = x_ref[...] + mod_ref[5:6, :] * y


def _ffn(x2, mod_l, gains, w_in, w_out, seq):
    t, d = x2.shape
    tm, th = min(TM_FFN, seq), TH_FFN
    tiles_per_seq = seq // tm
    nh = FFN_HIDDEN // th
    return pl.pallas_call(
        _ffn_kernel,
        grid=(t // tm, nh),
        in_specs=[
            pl.BlockSpec((tm, d), lambda i, j: (i, 0)),
            pl.BlockSpec((None, SUBLANES, d), lambda i, j: (i // tiles_per_seq, 0, 0)),
            pl.BlockSpec((2, d), lambda i, j: (0, 0)),
            pl.BlockSpec((d, th), lambda i, j: (0, j)),
            pl.BlockSpec((d, th), lambda i, j: (0, j + nh)),
            pl.BlockSpec((th, d), lambda i, j: (j, 0)),
        ],
        out_specs=pl.BlockSpec((tm, d), lambda i, j: (i, 0)),
        out_shape=jax.ShapeDtypeStruct((t, d), F32),
        scratch_shapes=[pltpu.VMEM((tm, d), BF16), pltpu.VMEM((tm, d), F32)],
        compiler_params=_cparams(("parallel", "arbitrary")),
        name="ffn",
    )(x2, mod_l, gains, w_in, w_in, w_out)


def _split_w_in(w):
    d = w.shape[0]
    main = jnp.concatenate([w[:, :3584], w[:, 3592:5128], w[:, 5144:5656]], axis=1)
    small = jnp.concatenate([w[:, 3584:3592], w[:, 5128:5144],
                             jnp.zeros((d, SMALL_W - 2 * ML_HEADS - GLA_RANK), w.dtype)], axis=1)
    return main.astype(BF16), small.astype(BF16)


def _pad_row(vals, offset):
    row = jnp.zeros((1, SMALL_W), F32)
    return lax.dynamic_update_slice(row, vals.reshape(1, -1).astype(F32), (0, offset))


ACT_DTYPE = F32


def kernel(x, c, ada_w, ada_b, norm_g, w_in, rel_bias, diff_lambda, ml_conv, ml_gate_b,
           gla_wa2, gla_ba, s5_a_re, s5_a_im, s5_log_dt, s5_b_re, s5_b_im, s5_c_re, s5_c_im,
           s5_d, s5_glu_w, s5_glu_b, w_branch, w_gate, b_gate, w_out, ffn_w_in, ffn_w_out):
    batch, seq, d = x.shape
    depth = ada_w.shape[0]
    t = batch * seq

    c_pad = jnp.concatenate([c, jnp.zeros((SUBLANES - batch, d), c.dtype)], axis=0)
    mod = _adaln(c_pad, ada_w, ada_b)[:, :batch]
    mod = mod.reshape(depth, batch, N_MOD, d)
    mod = jnp.concatenate([mod, jnp.zeros((depth, batch, SUBLANES - N_MOD, d), F32)], axis=2)

    bias_tiles = _bias_tiles(rel_bias, min(TQ_ATT, seq))

    x2 = x.reshape(t, d)
    for l in range(depth):
        w_main, w_small = _split_w_in(w_in[l])
        pm, ps = _proj(x2, mod[l], norm_g[l, 0:1], w_main, w_small, seq, ACT_DTYPE)

        lam_init = 0.8 - 0.6 * math.exp(-0.3 * l)
        lp = diff_lambda[l].astype(F32)
        lam = (jnp.exp(jnp.sum(lp[0] * lp[1])) - jnp.exp(jnp.sum(lp[2] * lp[3])) + lam_init)
        o_a = _attention(lam.reshape(1), pm, bias_tiles, batch, seq, lam_init, BF16)

        gate_row = (_pad_row(ml_gate_b[l, 0], SM_ML_I) + _pad_row(ml_gate_b[l, 1], SM_ML_F))
        o_b = _mlstm(pm, ps, ml_conv[l].astype(F32), gate_row, batch, seq, BF16)

        wa_pad = jnp.zeros((SMALL_W, GLA_HEADS * GLA_DK), F32)
        wa_pad = lax.dynamic_update_slice(wa_pad, gla_wa2[l].astype(F32), (SM_GL_A, 0))
        o_c = _gla(pm, ps, wa_pad, gla_ba[l].reshape(1, -1).astype(F32), batch, seq, BF16)

        packed = _s5_params(s5_a_re[l], s5_a_im[l], s5_log_dt[l], s5_b_re[l], s5_b_im[l],
                            s5_c_re[l], s5_c_im[l])
        o_d = _s5(pm, packed, s5_d[l].reshape(1, -1).astype(F32), s5_glu_w[l].astype(BF16),
                  s5_glu_b[l].reshape(1, -1).astype(F32), batch, seq, BF16)

        x2 = _merge(x2, mod[l], norm_g[l, 0:2], o_a, o_b, o_c, o_d,
                    w_gate[l].astype(BF16), b_gate[l].reshape(N_BRANCH, 1, d).astype(F32),
                    w_branch[l].astype(BF16), w_out[l].astype(BF16), seq)
        x2 = _ffn(x2, mod[l], norm_g[l, 2:4], ffn_w_in[l].astype(BF16),
                  ffn_w_out[l].astype(BF16), seq)
    return x2.reshape(batch, seq, d)
```

```python
import functools
import math

import numpy as np
import jax
import jax.numpy as jnp
from jax import lax
from jax.experimental import pallas as pl
from jax.experimental.pallas import tpu as pltpu

F32 = jnp.float32
BF16 = jnp.bfloat16
HIGHEST = lax.Precision.HIGHEST

D_MODEL = 2048
DEPTH = 4
EPS = 1e-6
N_MOD = 6
N_BRANCH = 4
BRANCH_WIDTH = 512
DA_HEADS = 4
DA_QK_DIM = 64
DA_V_DIM = 128
N_BUCKETS = 32
MAX_DISTANCE = 128
ML_HEADS = 4
ML_DIM = 128
ML_CONV = 4
GLA_HEADS = 4
GLA_DK = 64
GLA_DV = 128
GLA_RANK = 16
GLA_TAU = 16.0
S5_CH = 16
S5_GROUPS = BRANCH_WIDTH // S5_CH
S5_STATE = 64
FFN_HIDDEN = -(-(8 * D_MODEL) // (3 * 256)) * 256

LANES = 128
SUBLANES = 8
BF16_ROWS = 16
VMEM_LIMIT = 56 * 1024 * 1024

MAIN_W = 5632
SMALL_W = LANES
OFF_DA_Q, OFF_DA_K, OFF_DA_V = 0, 512, 1024
OFF_ML_Q, OFF_ML_K, OFF_ML_V, OFF_ML_O = 1536, 2048, 2560, 3072
OFF_GL_Q, OFF_GL_K, OFF_GL_V, OFF_GL_R = 3584, 3840, 4096, 4608
OFF_S5_U = 5120
SM_ML_I, SM_ML_F, SM_GL_A = 0, 4, 8

NEG = -1e30
LOG2E = math.log2(math.e)

TM_PROJ, TN_PROJ = 1024, 512
TQ_ATT = 512
L_MLSTM = 256
L_GLA, C_GLA = 256, 16
TM_S5 = 256
TM_MERGE, TN_MERGE = 512, 256
TM_FFN, TH_FFN = 512, 512


def _cparams(sem):
    return pltpu.CompilerParams(dimension_semantics=sem, vmem_limit_bytes=VMEM_LIMIT)


def _rms(x):
    return x * lax.rsqrt(jnp.mean(x * x, axis=-1, keepdims=True) + EPS)


def _sigmoid(x):
    return 1.0 / (1.0 + jnp.exp(-x))


def _silu(x):
    return x * _sigmoid(x)


def _log_sigmoid(x):
    return jnp.minimum(x, 0.0) - jnp.log1p(jnp.exp(-jnp.abs(x)))


def _dot(a, b):
    return jnp.dot(a, b, preferred_element_type=F32)


def _dot_nt(a, b):
    return lax.dot_general(a, b, (((1,), (1,)), ((), ())), preferred_element_type=F32)


def _dot_tn(a, b):
    return lax.dot_general(a, b, (((0,), (0,)), ((), ())), preferred_element_type=F32)


def _dot_exact(a, b):
    return jnp.dot(a, b, preferred_element_type=F32, precision=HIGHEST)


def _adaln_kernel(c_ref, w_ref, b_ref, o_ref):
    c = c_ref[...]
    o_ref[...] = _dot_exact(_silu(c), w_ref[...]) + b_ref[...]


def _adaln(c_pad, ada_w, ada_b):
    depth, d, n = ada_w.shape
    rows = c_pad.shape[0]
    tn = 1024
    return pl.pallas_call(
        _adaln_kernel,
        grid=(depth, n // tn),
        in_specs=[
            pl.BlockSpec((rows, d), lambda l, j: (0, 0)),
            pl.BlockSpec((None, d, tn), lambda l, j: (l, 0, j)),
            pl.BlockSpec((None, 1, tn), lambda l, j: (l, 0, j)),
        ],
        out_specs=pl.BlockSpec((None, rows, tn), lambda l, j: (l, 0, j)),
        out_shape=jax.ShapeDtypeStruct((depth, rows, n), F32),
        compiler_params=_cparams(("parallel", "parallel")),
        name="adaln",
    )(c_pad, ada_w, ada_b.reshape(depth, 1, n))


def _modulated_norm(x, gain, shift, scale):
    return (_rms(x) * gain) * (1.0 + scale) + shift


def _proj_kernel(x_ref, mod_ref, g_ref, w_ref, ws_ref, o_ref, os_ref, h_scr):
    j = pl.program_id(1)

    @pl.when(j == 0)
    def _():
        h = _modulated_norm(x_ref[...], g_ref[...], mod_ref[0:1, :], mod_ref[1:2, :])
        hb = h.astype(BF16)
        h_scr[...] = hb
        os_ref[...] = _dot(hb, ws_ref[...])

    o_ref[...] = _dot(h_scr[...], w_ref[...]).astype(o_ref.dtype)


def _proj(x2, mod_l, gain, w_main, w_small, seq, out_dtype):
    t, d = x2.shape
    tm, tn = min(TM_PROJ, seq), TN_PROJ
    tiles_per_seq = seq // tm
    return pl.pallas_call(
        _proj_kernel,
        grid=(t // tm, MAIN_W // tn),
        in_specs=[
            pl.BlockSpec((tm, d), lambda i, j: (i, 0)),
            pl.BlockSpec((None, SUBLANES, d), lambda i, j: (i // tiles_per_seq, 0, 0)),
            pl.BlockSpec((1, d), lambda i, j: (0, 0)),
            pl.BlockSpec((d, tn), lambda i, j: (0, j)),
            pl.BlockSpec((d, SMALL_W), lambda i, j: (0, 0)),
        ],
        out_specs=[
            pl.BlockSpec((tm, tn), lambda i, j: (i, j)),
            pl.BlockSpec((tm, SMALL_W), lambda i, j: (i, 0)),
        ],
        out_shape=[
            jax.ShapeDtypeStruct((t, MAIN_W), out_dtype),
            jax.ShapeDtypeStruct((t, SMALL_W), F32),
        ],
        scratch_shapes=[pltpu.VMEM((tm, d), BF16)],
        compiler_params=_cparams(("parallel", "arbitrary")),
        name="proj",
    )(x2, mod_l, gain, w_main, w_small)


def _attn_kernel(lam_ref, q_ref, k_ref, v_ref, bias_ref, o_ref, m_scr, l_scr, acc_scr,
                 *, tq, out_scale):
    i = pl.program_id(2)
    lam = lam_ref[0]
    q = q_ref[...].astype(F32) * (DA_QK_DIM ** -0.5 * LOG2E)
    lane = lax.broadcasted_iota(jnp.int32, q.shape, 1)
    qa = jnp.where(lane < DA_QK_DIM, q, 0.0).astype(BF16)
    qb = jnp.where(lane >= DA_QK_DIM, q, 0.0).astype(BF16)
    q2 = jnp.concatenate([qa, qb], axis=0)

    m_scr[...] = jnp.full_like(m_scr, NEG)
    l_scr[...] = jnp.zeros_like(l_scr)
    acc_scr[...] = jnp.zeros_like(acc_scr)

    def tile(j, bias):
        r0 = pl.multiple_of(j * tq, tq)
        kt = k_ref[pl.ds(r0, tq), :].astype(BF16)
        vt = v_ref[pl.ds(r0, tq), :].astype(BF16)
        s = _dot_nt(kt, q2)
        if bias is not None:
            s = s + jnp.concatenate([bias, bias], axis=1)
        m_old = m_scr[...]
        m_new = jnp.maximum(m_old, jnp.max(s, axis=0, keepdims=True))
        p = jnp.exp2(s - m_new)
        alpha = jnp.exp2(m_old - m_new)
        l_scr[...] = alpha * l_scr[...] + jnp.sum(p, axis=0, keepdims=True)
        acc_scr[...] = alpha * acc_scr[...] + _dot_tn(vt, p.astype(BF16))
        m_scr[...] = m_new

    tile(i, bias_ref[1])
    prev_idx = jnp.where(i >= 1, 0, 2)
    tile(jnp.maximum(i - 1, 0), bias_ref[prev_idx])

    def far(j, carry):
        tile(j, None)
        return carry

    lax.fori_loop(0, jnp.maximum(i - 1, 0), far, 0)
    on = acc_scr[...] / l_scr[...]
    ot = on[:, 0:tq] - lam * on[:, tq:2 * tq]
    ot = ot * (lax.rsqrt(jnp.mean(ot * ot, axis=0, keepdims=True) + EPS) * out_scale)
    o_ref[...] = ot.T.astype(o_ref.dtype)


def _attention(lam, pm, bias_tiles, batch, seq, lam_init, out_dtype):
    tq = min(TQ_ATT, seq)
    nq = seq // tq
    kern = functools.partial(_attn_kernel, tq=tq, out_scale=1.0 - lam_init)
    scratch = [pltpu.VMEM((1, 2 * tq), F32), pltpu.VMEM((1, 2 * tq), F32),
               pltpu.VMEM((DA_V_DIM, 2 * tq), F32)]
    qb, kb, vb = OFF_DA_Q // LANES, OFF_DA_K // LANES, OFF_DA_V // LANES
    return pl.pallas_call(
        kern,
        grid=(batch, DA_HEADS, nq),
        in_specs=[
            pl.BlockSpec(memory_space=pltpu.SMEM),
            pl.BlockSpec((tq, LANES), lambda b, h, i: (b * nq + i, qb + h)),
            pl.BlockSpec((seq, LANES), lambda b, h, i: (b, kb + h)),
            pl.BlockSpec((seq, LANES), lambda b, h, i: (b, vb + h)),
            pl.BlockSpec((None, 3, tq, tq), lambda b, h, i: (h, 0, 0, 0)),
        ],
        out_specs=pl.BlockSpec((tq, LANES), lambda b, h, i: (b * nq + i, h)),
        out_shape=jax.ShapeDtypeStruct((batch * seq, DA_HEADS * DA_V_DIM), out_dtype),
        scratch_shapes=scratch,
        compiler_params=_cparams(("parallel", "parallel", "arbitrary")),
        name="diff_attn",
    )(lam, pm, pm, pm, bias_tiles)


def _t5_bucket_table(n_max):
    n = np.arange(n_max)
    exact = N_BUCKETS // 2
    nf = np.maximum(n, 1).astype(np.float64)
    large = exact + (np.log(nf / exact) / math.log(MAX_DISTANCE / exact)
                     * (N_BUCKETS - exact)).astype(np.int64)
    return np.where(n < exact, n, np.minimum(large, N_BUCKETS - 1)).astype(np.int32)


def _bias_tiles(rel_bias, tq):
    assert tq >= MAX_DISTANCE
    bucket = _t5_bucket_table(2 * tq)
    rb = rel_bias.astype(F32)
    tbl = (rb[bucket] - rb[N_BUCKETS - 1][None, :]) * LOG2E
    c = np.arange(tq)[:, None]
    r = np.arange(tq)[None, :]
    prev = jnp.transpose(tbl[tq + r - c], (2, 0, 1))
    diag = jnp.transpose(tbl[np.maximum(r - c, 0)], (2, 0, 1))
    diag = jnp.where(jnp.asarray(r >= c)[None], diag, NEG)
    masked = jnp.full_like(prev, NEG)
    return jnp.stack([prev, diag, masked], axis=1)


def _mlstm_kernel(q_ref, k_ref, v_ref, og_ref, sm_ref, qh_ref, kh_ref, cw_ref, gb_ref,
                  tril_ref, out_ref, xq_scr, xk_scr, c_scr, n_scr, m_scr, *, L):
    ci = pl.program_id(1)

    @pl.when(ci == 0)
    def _():
        c_scr[...] = jnp.zeros_like(c_scr)
        n_scr[...] = jnp.zeros_like(n_scr)
        m_scr[...] = jnp.zeros_like(m_scr)

    halo = BF16_ROWS
    keep = (ci > 0).astype(F32)
    xq_scr[0:halo, :] = qh_ref[...].astype(F32) * keep
    xk_scr[0:halo, :] = kh_ref[...].astype(F32) * keep
    xq_scr[halo:halo + L, :] = q_ref[...].astype(F32)
    xk_scr[halo:halo + L, :] = k_ref[...].astype(F32)

    def conv_silu(scr, w):
        y = scr[halo:halo + L, :] * w[ML_CONV - 1:ML_CONV, :]
        for j in range(ML_CONV - 1):
            off = halo - (ML_CONV - 1) + j
            y = y + scr[off:off + L, :] * w[j:j + 1, :]
        return _silu(y)

    w_all = cw_ref[...]
    width = ML_HEADS * ML_DIM
    q_all = conv_silu(xq_scr, w_all[:, 0:width])
    k_all = conv_silu(xk_scr, w_all[:, width:2 * width]) * (ML_DIM ** -0.5)

    g = sm_ref[...] + gb_ref[...]
    lane = lax.broadcasted_iota(jnp.int32, g.shape, 1)
    is_f = (lane >= SM_ML_F) & (lane < SM_ML_F + ML_HEADS)
    g = jnp.where(is_f, _log_sigmoid(g), g)
    bcum = _dot_exact(tril_ref[...], g)
    g_t = g.T
    b_t = bcum.T
    row = lax.broadcasted_iota(jnp.int32, (L, L), 0)
    colj = lax.broadcasted_iota(jnp.int32, (L, L), 1)
    causal = colj <= row

    for h in range(ML_HEADS):
        sl = slice(h * ML_DIM, (h + 1) * ML_DIM)
        qh = q_all[:, sl]
        kh = k_all[:, sl]
        vh = v_ref[:, sl].astype(F32)
        qb, kb, vb = qh.astype(BF16), kh.astype(BF16), vh.astype(BF16)
        ig_col = g[:, SM_ML_I + h:SM_ML_I + h + 1]
        b_col = bcum[:, SM_ML_F + h:SM_ML_F + h + 1]
        ig_row = g_t[SM_ML_I + h:SM_ML_I + h + 1, :]
        b_row = b_t[SM_ML_F + h:SM_ML_F + h + 1, :]
        m_old = m_scr[h][:, 0:1]
        c_old = c_scr[h]
        n_old = n_scr[h]

        dm = jnp.where(causal, b_col - b_row + ig_row, NEG)
        inter = b_col + m_old
        m_t = jnp.maximum(inter, jnp.max(dm, axis=-1, keepdims=True))
        s = _dot_nt(qb, kb) * jnp.exp(dm - m_t)
        a = jnp.exp(inter - m_t)
        num = a * _dot(qb, c_old.astype(BF16)) + _dot(s.astype(BF16), vb)
        den = (a * jnp.sum(qh * n_old, axis=-1, keepdims=True)
               + jnp.sum(s, axis=-1, keepdims=True))
        hv = num / jnp.maximum(jnp.abs(den), jnp.exp(-m_t))
        out_ref[:, sl] = (_sigmoid(og_ref[:, sl].astype(F32)) * hv).astype(out_ref.dtype)

        m_new = m_t[L - 1:L, :]
        b_last = b_col[L - 1:L, :]
        a_state = jnp.exp(b_last + m_old - m_new)
        w_col = jnp.exp(b_last - b_col + ig_col - m_new)
        kw = kh * w_col
        c_scr[h] = a_state * c_old + _dot_tn(kw.astype(BF16), vb)
        n_scr[h] = a_state * n_old + jnp.sum(kw, axis=0, keepdims=True)
        m_scr[h] = jnp.broadcast_to(m_new, (1, LANES))


def _mlstm(pm, ps, conv_w, gate_row, batch, seq, out_dtype):
    L = min(L_MLSTM, seq)
    nc = seq // L
    width = ML_HEADS * ML_DIM
    qb, kb, vb, ob = (OFF_ML_Q // width, OFF_ML_K // width, OFF_ML_V // width, OFF_ML_O // width)
    lb = L // BF16_ROWS
    tril = jnp.asarray(np.tril(np.ones((L, L), np.float32)))

    def halo_map(colblk):
        return lambda b, c: (jnp.maximum(b * (seq // BF16_ROWS) + c * lb - 1, 0), colblk)

    return pl.pallas_call(
        functools.partial(_mlstm_kernel, L=L),
        grid=(batch, nc),
        in_specs=[
            pl.BlockSpec((L, width), lambda b, c: (b * nc + c, qb)),
            pl.BlockSpec((L, width), lambda b, c: (b * nc + c, kb)),
            pl.BlockSpec((L, width), lambda b, c: (b * nc + c, vb)),
            pl.BlockSpec((L, width), lambda b, c: (b * nc + c, ob)),
            pl.BlockSpec((L, SMALL_W), lambda b, c: (b * nc + c, 0)),
            pl.BlockSpec((BF16_ROWS, width), halo_map(qb)),
            pl.BlockSpec((BF16_ROWS, width), halo_map(kb)),
            pl.BlockSpec((ML_CONV, 2 * width), lambda b, c: (0, 0)),
            pl.BlockSpec((1, SMALL_W), lambda b, c: (0, 0)),
            pl.BlockSpec((L, L), lambda b, c: (0, 0)),
        ],
        out_specs=pl.BlockSpec((L, width), lambda b, c: (b * nc + c, 0)),
        out_shape=jax.ShapeDtypeStruct((batch * seq, width), out_dtype),
        scratch_shapes=[
            pltpu.VMEM((L + BF16_ROWS, width), F32),
            pltpu.VMEM((L + BF16_ROWS, width), F32),
            pltpu.VMEM((ML_HEADS, ML_DIM, ML_DIM), F32),
            pltpu.VMEM((ML_HEADS, 1, ML_DIM), F32),
            pltpu.VMEM((ML_HEADS, 1, LANES), F32),
        ],
        compiler_params=_cparams(("parallel", "arbitrary")),
        name="mlstm",
    )(pm, pm, pm, pm, ps, pm, pm, conv_w, gate_row, tril)


def _gla_kernel(q_ref, k_ref, v_ref, r_ref, sm_ref, wa_ref, ba_ref, tril_ref, mexp_ref,
                hmask_ref, out_ref, bc_scr, a_scr, st_scr, *, L, c):
    ci = pl.program_id(1)

    @pl.when(ci == 0)
    def _():
        st_scr[...] = jnp.zeros_like(st_scr)

    la = _log_sigmoid(_dot_exact(sm_ref[...], wa_ref[...]) + ba_ref[...]) * (1.0 / GLA_TAU)
    bc_scr[...] = _dot_exact(tril_ref[...], la)
    rowc = lax.broadcasted_iota(jnp.int32, (c, GLA_HEADS * GLA_DK), 0)

    def sub(i, carry):
        r0 = pl.multiple_of(i * c, c)
        qs = q_ref[pl.ds(r0, c), :].astype(F32) * (GLA_DK ** -0.5)
        ks = k_ref[pl.ds(r0, c), :].astype(F32)
        vs = v_ref[pl.ds(r0, c), :].astype(BF16).astype(F32)
        bcs = bc_scr[pl.ds(r0, c), :]
        e_end = bcs[c - 1:c, :]
        st = st_scr[...]
        o = _dot_nt((qs * jnp.exp(bcs)).astype(BF16), st.astype(BF16))

        for t in range(c):
            dec = jnp.exp(jnp.minimum(bcs[t:t + 1, :] - bcs, 0.0))
            a_t = jnp.where(rowc <= t, qs[t:t + 1, :] * ks * dec, 0.0)
            a_scr[t * c:(t + 1) * c, :] = a_t.astype(BF16)
        p = _dot(a_scr[...], mexp_ref[...])
        o = o + jnp.sum(p.reshape(c, c, GLA_HEADS * GLA_DV) * vs[None, :, :], axis=1)

        outs = [_rms(o[:, h * GLA_DV:(h + 1) * GLA_DV]) for h in range(GLA_HEADS)]
        on = jnp.concatenate(outs, axis=1)
        out_ref[pl.ds(r0, c), :] = (on * _silu(r_ref[pl.ds(r0, c), :].astype(F32))).astype(out_ref.dtype)

        khat = (ks * jnp.exp(e_end - bcs)).astype(BF16)
        upd = _dot_tn(vs.astype(BF16), khat)
        st_scr[...] = st * jnp.exp(e_end) + upd * hmask_ref[...]
        return carry

    lax.fori_loop(0, L // c, sub, 0)


def _gla(pm, ps, wa_pad, ba_row, batch, seq, out_dtype):
    L = min(L_GLA, seq)
    c = C_GLA
    nc = seq // L
    kw, vw = GLA_HEADS * GLA_DK, GLA_HEADS * GLA_DV
    qb, kb, vb, rb = OFF_GL_Q // kw, OFF_GL_K // kw, OFF_GL_V // vw, OFF_GL_R // vw
    idx = np.arange(L)
    tril = ((idx[:, None] >= idx[None, :]) & (idx[:, None] // c == idx[None, :] // c))
    tril = jnp.asarray(tril.astype(np.float32))
    mexp = np.zeros((kw, vw), np.float32)
    for h in range(GLA_HEADS):
        mexp[h * GLA_DK:(h + 1) * GLA_DK, h * GLA_DV:(h + 1) * GLA_DV] = 1.0
    hmask = jnp.asarray(mexp.T)
    mexp = jnp.asarray(mexp, dtype=BF16)
    return pl.pallas_call(
        functools.partial(_gla_kernel, L=L, c=c),
        grid=(batch, nc),
        in_specs=[
            pl.BlockSpec((L, kw), lambda b, i: (b * nc + i, qb)),
            pl.BlockSpec((L, kw), lambda b, i: (b * nc + i, kb)),
            pl.BlockSpec((L, vw), lambda b, i: (b * nc + i, vb)),
            pl.BlockSpec((L, vw), lambda b, i: (b * nc + i, rb)),
            pl.BlockSpec((L, SMALL_W), lambda b, i: (b * nc + i, 0)),
            pl.BlockSpec((SMALL_W, kw), lambda b, i: (0, 0)),
            pl.BlockSpec((1, kw), lambda b, i: (0, 0)),
            pl.BlockSpec((L, L), lambda b, i: (0, 0)),
            pl.BlockSpec((kw, vw), lambda b, i: (0, 0)),
            pl.BlockSpec((vw, kw), lambda b, i: (0, 0)),
        ],
        out_specs=pl.BlockSpec((L, vw), lambda b, i: (b * nc + i, 0)),
        out_shape=jax.ShapeDtypeStruct((batch * seq, vw), out_dtype),
        scratch_shapes=[
            pltpu.VMEM((L, kw), F32),
            pltpu.VMEM((c * c, kw), BF16),
            pltpu.VMEM((vw, kw), F32),
        ],
        compiler_params=_cparams(("parallel", "arbitrary")),
        name="gla",
    )(pm, pm, pm, pm, ps, wa_pad, ba_row, tril, mexp, hmask)


S5_NSTATE = S5_GROUPS * S5_STATE
S5_BLK = 4
S5_BLK_STATE = S5_NSTATE // S5_BLK


def _gelu_tanh(x):
    return 0.5 * x * (1.0 + jnp.tanh(math.sqrt(2.0 / math.pi) * (x + 0.044715 * (x * x * x))))


def _s5_kernel(u_ref, bre_ref, bim_ref, cre_ref, cim_ref, as_ref, pw_ref, d_ref, gw_ref,
               gb_ref, out_ref, xr_scr, xi_scr, cr_scr, ci_scr, *, tm):
    ti = pl.program_id(1)

    @pl.when(ti == 0)
    def _():
        cr_scr[...] = jnp.zeros_like(cr_scr)
        ci_scr[...] = jnp.zeros_like(ci_scr)

    u = u_ref[...].astype(F32)
    ub = u.astype(BF16)
    nb = S5_BLK_STATE
    for q in range(S5_BLK):
        uq = ub[:, q * LANES:(q + 1) * LANES]
        xr_scr[:, q * nb:(q + 1) * nb] = _dot(uq, bre_ref[q])
        xi_scr[:, q * nb:(q + 1) * nb] = _dot(uq, bim_ref[q])

    rowi = lax.broadcasted_iota(jnp.int32, (SUBLANES, nb), 0)
    for cc in range(S5_BLK):
        cols = slice(cc * nb, (cc + 1) * nb)
        a_r = as_ref[0, :, cols]
        a_i = as_ref[1, :, cols]
        p_r = pw_ref[0, :, cols]
        p_i = pw_ref[1, :, cols]

        def body(g, carry, cols=cols, a_r=a_r, a_i=a_i, p_r=p_r, p_i=p_i):
            cr, ci = carry
            r0 = pl.multiple_of(g * SUBLANES, SUBLANES)
            zr = xr_scr[pl.ds(r0, SUBLANES), cols]
            zi = xi_scr[pl.ds(r0, SUBLANES), cols]
            for si, s in enumerate((1, 2, 4)):
                sr = jnp.where(rowi >= s, pltpu.roll(zr, s, 0), 0.0)
                sim = jnp.where(rowi >= s, pltpu.roll(zi, s, 0), 0.0)
                ar = a_r[si:si + 1, :]
                ai = a_i[si:si + 1, :]
                zr, zi = zr + ar * sr - ai * sim, zi + ar * sim + ai * sr
            xr = zr + p_r * cr - p_i * ci
            xi = zi + p_r * ci + p_i * cr
            xr_scr[pl.ds(r0, SUBLANES), cols] = xr
            xi_scr[pl.ds(r0, SUBLANES), cols] = xi
            return xr[SUBLANES - 1:SUBLANES, :], xi[SUBLANES - 1:SUBLANES, :]

        cr, ci = lax.fori_loop(0, tm // SUBLANES, body, (cr_scr[:, cols], ci_scr[:, cols]))
        cr_scr[:, cols] = cr
        ci_scr[:, cols] = ci

    ys = []
    for q in range(S5_BLK):
        xr = xr_scr[:, q * nb:(q + 1) * nb].astype(BF16)
        xi = xi_scr[:, q * nb:(q + 1) * nb].astype(BF16)
        ys.append(_dot(xr, cre_ref[q]) + _dot(xi, cim_ref[q]))
    y = jnp.concatenate(ys, axis=1) + d_ref[...] * u
    z = _gelu_tanh(y)
    gate = _sigmoid(_dot(z.astype(BF16), gw_ref[...]) + gb_ref[...])
    out_ref[...] = (z * gate).astype(out_ref.dtype)


def _s5_params(a_re, a_im, log_dt, b_re, b_im, c_re, c_im):
    a_re, a_im = a_re.astype(F32), a_im.astype(F32)
    dt = jnp.exp(log_dt.astype(F32))[:, None]
    mag = jnp.exp(dt * a_re)
    ab_re, ab_im = mag * jnp.cos(dt * a_im), mag * jnp.sin(dt * a_im)
    nr, ni = ab_re - 1.0, ab_im
    den = a_re * a_re + a_im * a_im
    f_re = (nr * a_re + ni * a_im) / den
    f_im = (ni * a_re - nr * a_im) / den
    b_re, b_im = b_re.astype(F32), b_im.astype(F32)
    bb_re = f_re[..., None] * b_re - f_im[..., None] * b_im
    bb_im = f_re[..., None] * b_im + f_im[..., None] * b_re

    def apow(k):
        mk = jnp.exp(k * dt * a_re)
        return (mk * jnp.cos(k * dt * a_im)).reshape(-1), (mk * jnp.sin(k * dt * a_im)).reshape(-1)

    zero = jnp.zeros((S5_NSTATE,), F32)
    steps = [apow(float(s)) for s in (1, 2, 4)]
    as_arr = jnp.stack([jnp.stack([s[0] for s in steps] + [zero] * 5),
                        jnp.stack([s[1] for s in steps] + [zero] * 5)])
    pws = [apow(float(k + 1)) for k in range(SUBLANES)]
    pw_arr = jnp.stack([jnp.stack([p[0] for p in pws]), jnp.stack([p[1] for p in pws])])

    gpb = S5_GROUPS // S5_BLK
    eye = jnp.eye(gpb, dtype=F32)

    def pack_b(bb):
        bb = bb.reshape(S5_BLK, gpb, S5_STATE, S5_CH)
        return jnp.einsum('qgpc,gh->qgchp', bb, eye).reshape(S5_BLK, gpb * S5_CH, gpb * S5_STATE)

    def pack_c(cc):
        cc = cc.reshape(S5_BLK, gpb, S5_CH, S5_STATE)
        return jnp.einsum('qgcp,gh->qgphc', cc, eye).reshape(S5_BLK, gpb * S5_STATE, gpb * S5_CH)

    return (pack_b(bb_re).astype(BF16), pack_b(bb_im).astype(BF16),
            pack_c(c_re.astype(F32)).astype(BF16), pack_c(-c_im.astype(F32)).astype(BF16),
            as_arr, pw_arr)


def _s5(pm, packed, d_row, glu_w, glu_b, batch, seq, out_dtype):
    tm = min(TM_S5, seq)
    nt = seq // tm
    bre, bim, cre, cim, as_arr, pw_arr = packed
    w = BRANCH_WIDTH
    ub = OFF_S5_U // w
    full = lambda *shape: pl.BlockSpec(shape, lambda b, i: (0,) * len(shape))
    return pl.pallas_call(
        functools.partial(_s5_kernel, tm=tm),
        grid=(batch, nt),
        in_specs=[
            pl.BlockSpec((tm, w), lambda b, i: (b * nt + i, ub)),
            full(S5_BLK, LANES, S5_BLK_STATE), full(S5_BLK, LANES, S5_BLK_STATE),
            full(S5_BLK, S5_BLK_STATE, LANES), full(S5_BLK, S5_BLK_STATE, LANES),
            full(2, SUBLANES, S5_NSTATE), full(2, SUBLANES, S5_NSTATE),
            full(1, w), full(w, w), full(1, w),
        ],
        out_specs=pl.BlockSpec((tm, w), lambda b, i: (b * nt + i, 0)),
        out_shape=jax.ShapeDtypeStruct((batch * seq, w), out_dtype),
        scratch_shapes=[
            pltpu.VMEM((tm, S5_NSTATE), F32),
            pltpu.VMEM((tm, S5_NSTATE), F32),
            pltpu.VMEM((1, S5_NSTATE), F32),
            pltpu.VMEM((1, S5_NSTATE), F32),
        ],
        compiler_params=_cparams(("parallel", "arbitrary")),
        name="s5",
    )(pm, bre, bim, cre, cim, as_arr, pw_arr, d_row, glu_w, glu_b)


def _merge_kernel(x_ref, mod_ref, g_ref, oa_ref, ob_ref, oc_ref, od_ref, wg_ref, bg_ref,
                  wb_ref, wo_ref, out_ref, h_scr, acc_scr):
    n = pl.program_id(1)

    @pl.when(n == 0)
    def _():
        h = _modulated_norm(x_ref[...], g_ref[0:1, :], mod_ref[0:1, :], mod_ref[1:2, :])
        h_scr[...] = h.astype(BF16)
        acc_scr[...] = jnp.zeros_like(acc_scr)

    hb = h_scr[...]
    merged = None
    for i, o_ref in enumerate((oa_ref, ob_ref, oc_ref, od_ref)):
        gate = _sigmoid(_dot(hb, wg_ref[i]) + bg_ref[i])
        term = gate * _dot(o_ref[...].astype(BF16), wb_ref[i])
        merged = term if merged is None else merged + term
    acc_scr[...] += _dot(merged.astype(BF16), wo_ref[...])

    @pl.when(n == pl.num_programs(1) - 1)
    def _():
        y = _rms(acc_scr[...]) * g_ref[1:2, :]
        out_ref[...] = x_ref[...] + mod_ref[2:3, :] * y


def _merge(x2, mod_l, gains, oa, ob, oc, od, w_gate, b_gate, w_branch, w_out, seq):
    t, d = x2.shape
    tm, tn = min(TM_MERGE, seq), TN_MERGE
    tiles_per_seq = seq // tm
    w = BRANCH_WIDTH
    br_spec = pl.BlockSpec((tm, w), lambda i, n: (i, 0))
    return pl.pallas_call(
        _merge_kernel,
        grid=(t // tm, d // tn),
        in_specs=[
            pl.BlockSpec((tm, d), lambda i, n: (i, 0)),
            pl.BlockSpec((None, SUBLANES, d), lambda i, n: (i // tiles_per_seq, 0, 0)),
            pl.BlockSpec((2, d), lambda i, n: (0, 0)),
            br_spec, br_spec, br_spec, br_spec,
            pl.BlockSpec((N_BRANCH, d, tn), lambda i, n: (0, 0, n)),
            pl.BlockSpec((N_BRANCH, 1, tn), lambda i, n: (0, 0, n)),
            pl.BlockSpec((N_BRANCH, w, tn), lambda i, n: (0, 0, n)),
            pl.BlockSpec((tn, d), lambda i, n: (n, 0)),
        ],
        out_specs=pl.BlockSpec((tm, d), lambda i, n: (i, 0)),
        out_shape=jax.ShapeDtypeStruct((t, d), F32),
        scratch_shapes=[pltpu.VMEM((tm, d), BF16), pltpu.VMEM((tm, d), F32)],
        compiler_params=_cparams(("parallel", "arbitrary")),
        name="merge",
    )(x2, mod_l, gains, oa, ob, oc, od, w_gate, b_gate, w_branch, w_out)


def _ffn_kernel(x_ref, mod_ref, g_ref, wa_ref, wg_ref, wo_ref, out_ref, h_scr, acc_scr):
    j = pl.program_id(1)

    @pl.when(j == 0)
    def _():
        h = _modulated_norm(x_ref[...], g_ref[0:1, :], mod_ref[3:4, :], mod_ref[4:5, :])
        h_scr[...] = h.astype(BF16)
        acc_scr[...] = jnp.zeros_like(acc_scr)

    hb = h_scr[...]
    a = _dot(hb, wa_ref[...])
    g = _dot(hb, wg_ref[...])
    acc_scr[...] += _dot((_silu(a) * g).astype(BF16), wo_ref[...])

    @pl.when(j == pl.num_programs(1) - 1)
    def _():
        y = _rms(acc_scr[...]) * g_ref[1:2, :]
        out_ref[...] = x_ref[...] + mod_ref[5:6, :] * y


def _ffn(x2, mod_l, gains, w_in, w_out, seq):
    t, d = x2.shape
    tm, th = min(TM_FFN, seq), TH_FFN
    tiles_per_seq = seq // tm
    nh = FFN_HIDDEN // th
    return pl.pallas_call(
        _ffn_kernel,
        grid=(t // tm, nh),
        in_specs=[
            pl.BlockSpec((tm, d), lambda i, j: (i, 0)),
            pl.BlockSpec((None, SUBLANES, d), lambda i, j: (i // tiles_per_seq, 0, 0)),
            pl.BlockSpec((2, d), lambda i, j: (0, 0)),
            pl.BlockSpec((d, th), lambda i, j: (0, j)),
            pl.BlockSpec((d, th), lambda i, j: (0, j + nh)),
            pl.BlockSpec((th, d), lambda i, j: (j, 0)),
        ],
        out_specs=pl.BlockSpec((tm, d), lambda i, j: (i, 0)),
        out_shape=jax.ShapeDtypeStruct((t, d), F32),
        scratch_shapes=[pltpu.VMEM((tm, d), BF16), pltpu.VMEM((tm, d), F32)],
        compiler_params=_cparams(("parallel", "arbitrary")),
        name="ffn",
    )(x2, mod_l, gains, w_in, w_in, w_out)


def _split_w_in(w):
    d = w.shape[0]
    main = jnp.concatenate([w[:, :3584], w[:, 3592:5128], w[:, 5144:5656]], axis=1)
    small = jnp.concatenate([w[:, 3584:3592], w[:, 5128:5144],
                             jnp.zeros((d, SMALL_W - 2 * ML_HEADS - GLA_RANK), w.dtype)], axis=1)
    return main.astype(BF16), small.astype(BF16)


def _pad_row(vals, offset):
    row = jnp.zeros((1, SMALL_W), F32)
    return lax.dynamic_update_slice(row, vals.reshape(1, -1).astype(F32), (0, offset))


ACT_DTYPE = BF16


def kernel(x, c, ada_w, ada_b, norm_g, w_in, rel_bias, diff_lambda, ml_conv, ml_gate_b,
           gla_wa2, gla_ba, s5_a_re, s5_a_im, s5_log_dt, s5_b_re, s5_b_im, s5_c_re, s5_c_im,
           s5_d, s5_glu_w, s5_glu_b, w_branch, w_gate, b_gate, w_out, ffn_w_in, ffn_w_out):
    batch, seq, d = x.shape
    depth = ada_w.shape[0]
    t = batch * seq

    c_pad = jnp.concatenate([c, jnp.zeros((SUBLANES - batch, d), c.dtype)], axis=0)
    mod = _adaln(c_pad, ada_w, ada_b)[:, :batch]
    mod = mod.reshape(depth, batch, N_MOD, d)
    mod = jnp.concatenate([mod, jnp.zeros((depth, batch, SUBLANES - N_MOD, d), F32)], axis=2)

    bias_tiles = _bias_tiles(rel_bias, min(TQ_ATT, seq))

    x2 = x.reshape(t, d)
    for l in range(depth):
        w_main, w_small = _split_w_in(w_in[l])
        pm, ps = _proj(x2, mod[l], norm_g[l, 0:1], w_main, w_small, seq, ACT_DTYPE)

        lam_init = 0.8 - 0.6 * math.exp(-0.3 * l)
        lp = diff_lambda[l].astype(F32)
        lam = (jnp.exp(jnp.sum(lp[0] * lp[1])) - jnp.exp(jnp.sum(lp[2] * lp[3])) + lam_init)
        o_a = _attention(lam.reshape(1), pm, bias_tiles, batch, seq, lam_init, BF16)

        gate_row = (_pad_row(ml_gate_b[l, 0], SM_ML_I) + _pad_row(ml_gate_b[l, 1], SM_ML_F))
        o_b = _mlstm(pm, ps, ml_conv[l].astype(F32), gate_row, batch, seq, BF16)

        wa_pad = jnp.zeros((SMALL_W, GLA_HEADS * GLA_DK), F32)
        wa_pad = lax.dynamic_update_slice(wa_pad, gla_wa2[l].astype(F32), (SM_GL_A, 0))
        o_c = _gla(pm, ps, wa_pad, gla_ba[l].reshape(1, -1).astype(F32), batch, seq, BF16)

        packed = _s5_params(s5_a_re[l], s5_a_im[l], s5_log_dt[l], s5_b_re[l], s5_b_im[l],
                            s5_c_re[l], s5_c_im[l])
        o_d = _s5(pm, packed, s5_d[l].reshape(1, -1).astype(F32), s5_glu_w[l].astype(BF16),
                  s5_glu_b[l].reshape(1, -1).astype(F32), batch, seq, BF16)

        x2 = _merge(x2, mod[l], norm_g[l, 0:2], o_a, o_b, o_c, o_d,
                    w_gate[l].astype(BF16), b_gate[l].reshape(N_BRANCH, 1, d).astype(F32),
                    w_branch[l].astype(BF16), w_out[l].astype(BF16), seq)
        x2 = _ffn(x2, mod[l], norm_g[l, 2:4], ffn_w_in[l].astype(BF16),
                  ffn_w_out[l].astype(BF16), seq)
    return x2.reshape(batch, seq, d)
```

```python
import functools
import math

import numpy as np
import jax
import jax.numpy as jnp
from jax import lax
from jax.experimental import pallas as pl
from jax.experimental.pallas import tpu as pltpu

F32 = jnp.float32
BF16 = jnp.bfloat16
HIGHEST = lax.Precision.HIGHEST

D_MODEL = 2048
DEPTH = 4
EPS = 1e-6
N_MOD = 6
N_BRANCH = 4
BRANCH_WIDTH = 512
DA_HEADS = 4
DA_QK_DIM = 64
DA_V_DIM = 128
N_BUCKETS = 32
MAX_DISTANCE = 128
ML_HEADS = 4
ML_DIM = 128
ML_CONV = 4
GLA_HEADS = 4
GLA_DK = 64
GLA_DV = 128
GLA_RANK = 16
GLA_TAU = 16.0
S5_CH = 16
S5_GROUPS = BRANCH_WIDTH // S5_CH
S5_STATE = 64
FFN_HIDDEN = -(-(8 * D_MODEL) // (3 * 256)) * 256

LANES = 128
SUBLANES = 8
BF16_ROWS = 16
VMEM_LIMIT = 56 * 1024 * 1024

MAIN_W = 5632
SMALL_W = LANES
OFF_DA_Q, OFF_DA_K, OFF_DA_V = 0, 512, 1024
OFF_ML_Q, OFF_ML_K, OFF_ML_V, OFF_ML_O = 1536, 2048, 2560, 3072
OFF_GL_Q, OFF_GL_K, OFF_GL_V, OFF_GL_R = 3584, 3840, 4096, 4608
OFF_S5_U = 5120
SM_ML_I, SM_ML_F, SM_GL_A = 0, 4, 8

NEG = -1e30
LOG2E = math.log2(math.e)

TM_PROJ, TN_PROJ = 1024, 512
TQ_ATT = 512
L_MLSTM = 256
L_GLA, C_GLA = 256, 16
TM_S5 = 256
TM_MERGE, TN_MERGE = 512, 256
TM_FFN, TH_FFN = 512, 512


def _cparams(sem):
    return pltpu.CompilerParams(dimension_semantics=sem, vmem_limit_bytes=VMEM_LIMIT)


def _rms(x):
    return x * lax.rsqrt(jnp.mean(x * x, axis=-1, keepdims=True) + EPS)


def _sigmoid(x):
    return 1.0 / (1.0 + jnp.exp(-x))


def _silu(x):
    return x * _sigmoid(x)


def _log_sigmoid(x):
    return jnp.minimum(x, 0.0) - jnp.log1p(jnp.exp(-jnp.abs(x)))


def _dot(a, b):
    return jnp.dot(a, b, preferred_element_type=F32)


def _dot_nt(a, b):
    return lax.dot_general(a, b, (((1,), (1,)), ((), ())), preferred_element_type=F32)


def _dot_tn(a, b):
    return lax.dot_general(a, b, (((0,), (0,)), ((), ())), preferred_element_type=F32)


def _dot_exact(a, b):
    return jnp.dot(a, b, preferred_element_type=F32, precision=HIGHEST)


def _adaln_kernel(c_ref, w_ref, b_ref, o_ref):
    c = c_ref[...]
    o_ref[...] = _dot_exact(_silu(c), w_ref[...]) + b_ref[...]


def _adaln(c_pad, ada_w, ada_b):
    depth, d, n = ada_w.shape
    rows = c_pad.shape[0]
    tn = 1024
    return pl.pallas_call(
        _adaln_kernel,
        grid=(depth, n // tn),
        in_specs=[
            pl.BlockSpec((rows, d), lambda l, j: (0, 0)),
            pl.BlockSpec((None, d, tn), lambda l, j: (l, 0, j)),
            pl.BlockSpec((None, 1, tn), lambda l, j: (l, 0, j)),
        ],
        out_specs=pl.BlockSpec((None, rows, tn), lambda l, j: (l, 0, j)),
        out_shape=jax.ShapeDtypeStruct((depth, rows, n), F32),
        compiler_params=_cparams(("parallel", "parallel")),
        name="adaln",
    )(c_pad, ada_w, ada_b.reshape(depth, 1, n))


def _modulated_norm(x, gain, shift, scale):
    return (_rms(x) * gain) * (1.0 + scale) + shift


def _proj_kernel(x_ref, mod_ref, g_ref, w_ref, ws_ref, o_ref, os_ref, h_scr):
    j = pl.program_id(1)

    @pl.when(j == 0)
    def _():
        h = _modulated_norm(x_ref[...], g_ref[...], mod_ref[0:1, :], mod_ref[1:2, :])
        hb = h.astype(BF16)
        h_scr[...] = hb
        os_ref[...] = _dot(hb, ws_ref[...])

    o_ref[...] = _dot(h_scr[...], w_ref[...]).astype(o_ref.dtype)


def _proj(x2, mod_l, gain, w_main, w_small, seq, out_dtype):
    t, d = x2.shape
    tm, tn = min(TM_PROJ, seq), TN_PROJ
    tiles_per_seq = seq // tm
    return pl.pallas_call(
        _proj_kernel,
        grid=(t // tm, MAIN_W // tn),
        in_specs=[
            pl.BlockSpec((tm, d), lambda i, j: (i, 0)),
            pl.BlockSpec((None, SUBLANES, d), lambda i, j: (i // tiles_per_seq, 0, 0)),
            pl.BlockSpec((1, d), lambda i, j: (0, 0)),
            pl.BlockSpec((d, tn), lambda i, j: (0, j)),
            pl.BlockSpec((d, SMALL_W), lambda i, j: (0, 0)),
        ],
        out_specs=[
            pl.BlockSpec((tm, tn), lambda i, j: (i, j)),
            pl.BlockSpec((tm, SMALL_W), lambda i, j: (i, 0)),
        ],
        out_shape=[
            jax.ShapeDtypeStruct((t, MAIN_W), out_dtype),
            jax.ShapeDtypeStruct((t, SMALL_W), F32),
        ],
        scratch_shapes=[pltpu.VMEM((tm, d), BF16)],
        compiler_params=_cparams(("parallel", "arbitrary")),
        name="proj",
    )(x2, mod_l, gain, w_main, w_small)


def _attn_kernel(lam_ref, q_ref, k_ref, v_ref, bias_ref, o_ref, m_scr, l_scr, acc_scr,
                 s0_scr, s1_scr, *, tq, out_scale):
    i = pl.program_id(2)
    lam = lam_ref[0]
    q = q_ref[...].astype(F32) * (DA_QK_DIM ** -0.5 * LOG2E)
    lane = lax.broadcasted_iota(jnp.int32, q.shape, 1)
    qa = jnp.where(lane < DA_QK_DIM, q, 0.0).astype(BF16)
    qb = jnp.where(lane >= DA_QK_DIM, q, 0.0).astype(BF16)
    q2 = jnp.concatenate([qa, qb], axis=0)

    m_scr[...] = jnp.full_like(m_scr, NEG)
    l_scr[...] = jnp.zeros_like(l_scr)
    acc_scr[...] = jnp.zeros_like(acc_scr)

    def scores(j):
        r0 = pl.multiple_of(j * tq, tq)
        return _dot_nt(k_ref[pl.ds(r0, tq), :].astype(BF16), q2)

    def accumulate(j, s):
        r0 = pl.multiple_of(j * tq, tq)
        vt = v_ref[pl.ds(r0, tq), :].astype(BF16)
        m_old = m_scr[...]
        m_new = jnp.maximum(m_old, jnp.max(s, axis=0, keepdims=True))
        p = jnp.exp2(s - m_new)
        alpha = jnp.exp2(m_old - m_new)
        l_scr[...] = alpha * l_scr[...] + jnp.sum(p, axis=0, keepdims=True)
        acc_scr[...] = alpha * acc_scr[...] + _dot_tn(vt, p.astype(BF16))
        m_scr[...] = m_new

    def biased(j, bias):
        accumulate(j, scores(j) + jnp.concatenate([bias, bias], axis=1))

    biased(i, bias_ref[1])
    prev_idx = jnp.where(i >= 1, 0, 2)
    biased(jnp.maximum(i - 1, 0), bias_ref[prev_idx])

    n_far = jnp.maximum(i - 1, 0)
    pairs = n_far // 2

    @pl.when(n_far > 0)
    def _():
        s0_scr[...] = scores(0)

    def far_pair(g, carry):
        s1_scr[...] = scores(2 * g + 1)
        accumulate(2 * g, s0_scr[...])
        s0_scr[...] = scores(jnp.minimum(2 * g + 2, n_far - 1))
        accumulate(2 * g + 1, s1_scr[...])
        return carry

    lax.fori_loop(0, pairs, far_pair, 0)

    @pl.when(n_far > 2 * pairs)
    def _():
        accumulate(n_far - 1, s0_scr[...])

    on = acc_scr[...] / l_scr[...]
    ot = on[:, 0:tq] - lam * on[:, tq:2 * tq]
    ot = ot * (lax.rsqrt(jnp.mean(ot * ot, axis=0, keepdims=True) + EPS) * out_scale)
    o_ref[...] = ot.T.astype(o_ref.dtype)


def _attention(lam, pm, bias_tiles, batch, seq, lam_init, out_dtype):
    tq = min(TQ_ATT, seq)
    nq = seq // tq
    kern = functools.partial(_attn_kernel, tq=tq, out_scale=1.0 - lam_init)
    scratch = [pltpu.VMEM((1, 2 * tq), F32), pltpu.VMEM((1, 2 * tq), F32),
               pltpu.VMEM((DA_V_DIM, 2 * tq), F32),
               pltpu.VMEM((tq, 2 * tq), F32), pltpu.VMEM((tq, 2 * tq), F32)]
    qb, kb, vb = OFF_DA_Q // LANES, OFF_DA_K // LANES, OFF_DA_V // LANES
    return pl.pallas_call(
        kern,
        grid=(batch, DA_HEADS, nq),
        in_specs=[
            pl.BlockSpec(memory_space=pltpu.SMEM),
            pl.BlockSpec((tq, LANES), lambda b, h, i: (b * nq + i, qb + h)),
            pl.BlockSpec((seq, LANES), lambda b, h, i: (b, kb + h)),
            pl.BlockSpec((seq, LANES), lambda b, h, i: (b, vb + h)),
            pl.BlockSpec((None, 3, tq, tq), lambda b, h, i: (h, 0, 0, 0)),
        ],
        out_specs=pl.BlockSpec((tq, LANES), lambda b, h, i: (b * nq + i, h)),
        out_shape=jax.ShapeDtypeStruct((batch * seq, DA_HEADS * DA_V_DIM), out_dtype),
        scratch_shapes=scratch,
        compiler_params=_cparams(("parallel", "parallel", "arbitrary")),
        name="diff_attn",
    )(lam, pm, pm, pm, bias_tiles)


def _t5_bucket_table(n_max):
    n = np.arange(n_max)
    exact = N_BUCKETS // 2
    nf = np.maximum(n, 1).astype(np.float64)
    large = exact + (np.log(nf / exact) / math.log(MAX_DISTANCE / exact)
                     * (N_BUCKETS - exact)).astype(np.int64)
    return np.where(n < exact, n, np.minimum(large, N_BUCKETS - 1)).astype(np.int32)


def _bias_tiles(rel_bias, tq):
    assert tq >= MAX_DISTANCE
    n = tq
    heads = rel_bias.shape[1]
    rb = rel_bias.astype(F32)
    near = (rb[_t5_bucket_table(MAX_DISTANCE)] - rb[N_BUCKETS - 1][None, :]) * LOG2E
    f = jnp.concatenate([near, jnp.zeros((2 * n - MAX_DISTANCE, heads), F32)], axis=0)

    def toeplitz(v):
        vp = jnp.concatenate([v, jnp.zeros((1, heads), F32)], axis=0)
        flat = jnp.tile(vp, (n, 1))[: n * (2 * n - 1)]
        return jnp.transpose(flat.reshape(n, 2 * n - 1, heads)[:, n - 1:, :], (2, 0, 1))

    prev = toeplitz(f[1:2 * n])
    diag = toeplitz(jnp.concatenate([jnp.full((n - 1, heads), NEG, F32), f[0:n]], axis=0))
    masked = jnp.full_like(prev, NEG)
    return jnp.stack([prev, diag, masked], axis=1)


def _mlstm_kernel(q_ref, k_ref, v_ref, og_ref, sm_ref, qh_ref, kh_ref, cw_ref, gb_ref,
                  tril_ref, out_ref, xq_scr, xk_scr, c_scr, n_scr, m_scr, *, L):
    ci = pl.program_id(1)

    @pl.when(ci == 0)
    def _():
        c_scr[...] = jnp.zeros_like(c_scr)
        n_scr[...] = jnp.zeros_like(n_scr)
        m_scr[...] = jnp.zeros_like(m_scr)

    halo = BF16_ROWS
    keep = (ci > 0).astype(F32)
    xq_scr[0:halo, :] = qh_ref[...].astype(F32) * keep
    xk_scr[0:halo, :] = kh_ref[...].astype(F32) * keep
    xq_scr[halo:halo + L, :] = q_ref[...].astype(F32)
    xk_scr[halo:halo + L, :] = k_ref[...].astype(F32)

    def conv_silu(scr, w):
        y = scr[halo:halo + L, :] * w[ML_CONV - 1:ML_CONV, :]
        for j in range(ML_CONV - 1):
            off = halo - (ML_CONV - 1) + j
            y = y + scr[off:off + L, :] * w[j:j + 1, :]
        return _silu(y)

    w_all = cw_ref[...]
    width = ML_HEADS * ML_DIM
    q_all = conv_silu(xq_scr, w_all[:, 0:width])
    k_all = conv_silu(xk_scr, w_all[:, width:2 * width]) * (ML_DIM ** -0.5)

    g = sm_ref[...] + gb_ref[...]
    lane = lax.broadcasted_iota(jnp.int32, g.shape, 1)
    is_f = (lane >= SM_ML_F) & (lane < SM_ML_F + ML_HEADS)
    g = jnp.where(is_f, _log_sigmoid(g), g)
    bcum = _dot_exact(tril_ref[...], g)
    g_t = g.T
    b_t = bcum.T
    row = lax.broadcasted_iota(jnp.int32, (L, L), 0)
    colj = lax.broadcasted_iota(jnp.int32, (L, L), 1)
    causal = colj <= row

    for h in range(ML_HEADS):
        sl = slice(h * ML_DIM, (h + 1) * ML_DIM)
        qh = q_all[:, sl]
        kh = k_all[:, sl]
        vh = v_ref[:, sl].astype(F32)
        qb, kb, vb = qh.astype(BF16), kh.astype(BF16), vh.astype(BF16)
        ig_col = g[:, SM_ML_I + h:SM_ML_I + h + 1]
        b_col = bcum[:, SM_ML_F + h:SM_ML_F + h + 1]
        ig_row = g_t[SM_ML_I + h:SM_ML_I + h + 1, :]
        b_row = b_t[SM_ML_F + h:SM_ML_F + h + 1, :]
        m_old = m_scr[h][:, 0:1]
        c_old = c_scr[h]
        n_old = n_scr[h]

        dm = jnp.where(causal, b_col - b_row + ig_row, NEG)
        inter = b_col + m_old
        m_t = jnp.maximum(inter, jnp.max(dm, axis=-1, keepdims=True))
        s = _dot_nt(qb, kb) * jnp.exp(dm - m_t)
        a = jnp.exp(inter - m_t)
        num = a * _dot(qb, c_old.astype(BF16)) + _dot(s.astype(BF16), vb)
        den = (a * jnp.sum(qh * n_old, axis=-1, keepdims=True)
               + jnp.sum(s, axis=-1, keepdims=True))
        hv = num / jnp.maximum(jnp.abs(den), jnp.exp(-m_t))
        out_ref[:, sl] = (_sigmoid(og_ref[:, sl].astype(F32)) * hv).astype(out_ref.dtype)

        m_new = m_t[L - 1:L, :]
        b_last = b_col[L - 1:L, :]
        a_state = jnp.exp(b_last + m_old - m_new)
        w_col = jnp.exp(b_last - b_col + ig_col - m_new)
        kw = kh * w_col
        c_scr[h] = a_state * c_old + _dot_tn(kw.astype(BF16), vb)
        n_scr[h] = a_state * n_old + jnp.sum(kw, axis=0, keepdims=True)
        m_scr[h] = jnp.broadcast_to(m_new, (1, LANES))


def _mlstm(pm, ps, conv_w, gate_row, batch, seq, out_dtype):
    L = min(L_MLSTM, seq)
    nc = seq // L
    width = ML_HEADS * ML_DIM
    qb, kb, vb, ob = (OFF_ML_Q // width, OFF_ML_K // width, OFF_ML_V // width, OFF_ML_O // width)
    lb = L // BF16_ROWS
    tril = jnp.asarray(np.tril(np.ones((L, L), np.float32)))

    def halo_map(colblk):
        return lambda b, c: (jnp.maximum(b * (seq // BF16_ROWS) + c * lb - 1, 0), colblk)

    return pl.pallas_call(
        functools.partial(_mlstm_kernel, L=L),
        grid=(batch, nc),
        in_specs=[
            pl.BlockSpec((L, width), lambda b, c: (b * nc + c, qb)),
            pl.BlockSpec((L, width), lambda b, c: (b * nc + c, kb)),
            pl.BlockSpec((L, width), lambda b, c: (b * nc + c, vb)),
            pl.BlockSpec((L, width), lambda b, c: (b * nc + c, ob)),
            pl.BlockSpec((L, SMALL_W), lambda b, c: (b * nc + c, 0)),
            pl.BlockSpec((BF16_ROWS, width), halo_map(qb)),
            pl.BlockSpec((BF16_ROWS, width), halo_map(kb)),
            pl.BlockSpec((ML_CONV, 2 * width), lambda b, c: (0, 0)),
            pl.BlockSpec((1, SMALL_W), lambda b, c: (0, 0)),
            pl.BlockSpec((L, L), lambda b, c: (0, 0)),
        ],
        out_specs=pl.BlockSpec((L, width), lambda b, c: (b * nc + c, 0)),
        out_shape=jax.ShapeDtypeStruct((batch * seq, width), out_dtype),
        scratch_shapes=[
            pltpu.VMEM((L + BF16_ROWS, width), F32),
            pltpu.VMEM((L + BF16_ROWS, width), F32),
            pltpu.VMEM((ML_HEADS, ML_DIM, ML_DIM), F32),
            pltpu.VMEM((ML_HEADS, 1, ML_DIM), F32),
            pltpu.VMEM((ML_HEADS, 1, LANES), F32),
        ],
        compiler_params=_cparams(("parallel", "arbitrary")),
        name="mlstm",
    )(pm, pm, pm, pm, ps, pm, pm, conv_w, gate_row, tril)


def _gla_kernel(q_ref, k_ref, v_ref, r_ref, sm_ref, wa_ref, ba_ref, tril_ref, mexp_ref,
                hmask_ref, out_ref, bc_scr, a_scr, st_scr, *, L, c):
    ci = pl.program_id(1)

    @pl.when(ci == 0)
    def _():
        st_scr[...] = jnp.zeros_like(st_scr)

    la = _log_sigmoid(_dot_exact(sm_ref[...], wa_ref[...]) + ba_ref[...]) * (1.0 / GLA_TAU)
    bc_scr[...] = _dot_exact(tril_ref[...], la)
    rowc = lax.broadcasted_iota(jnp.int32, (c, GLA_HEADS * GLA_DK), 0)

    def sub(i, carry):
        r0 = pl.multiple_of(i * c, c)
        qs = q_ref[pl.ds(r0, c), :].astype(F32) * (GLA_DK ** -0.5)
        ks = k_ref[pl.ds(r0, c), :].astype(F32)
        vs = v_ref[pl.ds(r0, c), :].astype(BF16).astype(F32)
        bcs = bc_scr[pl.ds(r0, c), :]
        e_end = bcs[c - 1:c, :]
        st = st_scr[...]
        o = _dot_nt((qs * jnp.exp(bcs)).astype(BF16), st.astype(BF16))

        for t in range(c):
            dec = jnp.exp(jnp.minimum(bcs[t:t + 1, :] - bcs, 0.0))
            a_t = jnp.where(rowc <= t, qs[t:t + 1, :] * ks * dec, 0.0)
            a_scr[t * c:(t + 1) * c, :] = a_t.astype(BF16)
        p = _dot(a_scr[...], mexp_ref[...])
        o = o + jnp.sum(p.reshape(c, c, GLA_HEADS * GLA_DV) * vs[None, :, :], axis=1)

        outs = [_rms(o[:, h * GLA_DV:(h + 1) * GLA_DV]) for h in range(GLA_HEADS)]
        on = jnp.concatenate(outs, axis=1)
        out_ref[pl.ds(r0, c), :] = (on * _silu(r_ref[pl.ds(r0, c), :].astype(F32))).astype(out_ref.dtype)

        khat = (ks * jnp.exp(e_end - bcs)).astype(BF16)
        upd = _dot_tn(vs.astype(BF16), khat)
        st_scr[...] = st * jnp.exp(e_end) + upd * hmask_ref[...]
        return carry

    lax.fori_loop(0, L // c, sub, 0)


def _gla(pm, ps, wa_pad, ba_row, batch, seq, out_dtype):
    L = min(L_GLA, seq)
    c = C_GLA
    nc = seq // L
    kw, vw = GLA_HEADS * GLA_DK, GLA_HEADS * GLA_DV
    qb, kb, vb, rb = OFF_GL_Q // kw, OFF_GL_K // kw, OFF_GL_V // vw, OFF_GL_R // vw
    idx = np.arange(L)
    tril = ((idx[:, None] >= idx[None, :]) & (idx[:, None] // c == idx[None, :] // c))
    tril = jnp.asarray(tril.astype(np.float32))
    mexp = np.zeros((kw, vw), np.float32)
    for h in range(GLA_HEADS):
        mexp[h * GLA_DK:(h + 1) * GLA_DK, h * GLA_DV:(h + 1) * GLA_DV] = 1.0
    hmask = jnp.asarray(mexp.T)
    mexp = jnp.asarray(mexp, dtype=BF16)
    return pl.pallas_call(
        functools.partial(_gla_kernel, L=L, c=c),
        grid=(batch, nc),
        in_specs=[
            pl.BlockSpec((L, kw), lambda b, i: (b * nc + i, qb)),
            pl.BlockSpec((L, kw), lambda b, i: (b * nc + i, kb)),
            pl.BlockSpec((L, vw), lambda b, i: (b * nc + i, vb)),
            pl.BlockSpec((L, vw), lambda b, i: (b * nc + i, rb)),
            pl.BlockSpec((L, SMALL_W), lambda b, i: (b * nc + i, 0)),
            pl.BlockSpec((SMALL_W, kw), lambda b, i: (0, 0)),
            pl.BlockSpec((1, kw), lambda b, i: (0, 0)),
            pl.BlockSpec((L, L), lambda b, i: (0, 0)),
            pl.BlockSpec((kw, vw), lambda b, i: (0, 0)),
            pl.BlockSpec((vw, kw), lambda b, i: (0, 0)),
        ],
        out_specs=pl.BlockSpec((L, vw), lambda b, i: (b * nc + i, 0)),
        out_shape=jax.ShapeDtypeStruct((batch * seq, vw), out_dtype),
        scratch_shapes=[
            pltpu.VMEM((L, kw), F32),
            pltpu.VMEM((c * c, kw), BF16),
            pltpu.VMEM((vw, kw), F32),
        ],
        compiler_params=_cparams(("parallel", "arbitrary")),
        name="gla",
    )(pm, pm, pm, pm, ps, wa_pad, ba_row, tril, mexp, hmask)


S5_NSTATE = S5_GROUPS * S5_STATE
S5_BLK = 4
S5_BLK_STATE = S5_NSTATE // S5_BLK
S5_SHIFTS = (1, 2, 4)


def _gelu_tanh(x):
    return 0.5 * x * (1.0 + jnp.tanh(math.sqrt(2.0 / math.pi) * (x + 0.044715 * (x * x * x))))


def _s5_kernel(u_ref, bre_ref, bim_ref, cre_ref, cim_ref, as_ref, pw_ref, d_ref, gw_ref,
               gb_ref, out_ref, xr_scr, xi_scr, cr_scr, ci_scr, *, tm):
    ti = pl.program_id(1)

    @pl.when(ti == 0)
    def _():
        cr_scr[...] = jnp.zeros_like(cr_scr)
        ci_scr[...] = jnp.zeros_like(ci_scr)

    u = u_ref[...].astype(F32)
    ub = u.astype(BF16)
    nb = S5_BLK_STATE
    for q in range(S5_BLK):
        uq = ub[:, q * LANES:(q + 1) * LANES]
        xr_scr[:, q * nb:(q + 1) * nb] = _dot(uq, bre_ref[q])
        xi_scr[:, q * nb:(q + 1) * nb] = _dot(uq, bim_ref[q])

    for cc in range(S5_BLK):
        cols = slice(cc * nb, (cc + 1) * nb)

        def body(g, carry, cols=cols):
            cr, ci = carry
            r0 = pl.multiple_of(g * SUBLANES, SUBLANES)
            zr = xr_scr[pl.ds(r0, SUBLANES), cols]
            zi = xi_scr[pl.ds(r0, SUBLANES), cols]
            for si, s in enumerate(S5_SHIFTS):
                sr = pltpu.roll(zr, s, 0)
                sim = pltpu.roll(zi, s, 0)
                ar = as_ref[0, si, :, cols]
                ai = as_ref[1, si, :, cols]
                zr, zi = zr + ar * sr - ai * sim, zi + ar * sim + ai * sr
            p_r = pw_ref[0, :, cols]
            p_i = pw_ref[1, :, cols]
            xr = zr + p_r * cr - p_i * ci
            xi = zi + p_r * ci + p_i * cr
            xr_scr[pl.ds(r0, SUBLANES), cols] = xr
            xi_scr[pl.ds(r0, SUBLANES), cols] = xi
            return xr[SUBLANES - 1:SUBLANES, :], xi[SUBLANES - 1:SUBLANES, :]

        cr, ci = lax.fori_loop(0, tm // SUBLANES, body, (cr_scr[:, cols], ci_scr[:, cols]))
        cr_scr[:, cols] = cr
        ci_scr[:, cols] = ci

    ys = []
    for q in range(S5_BLK):
        xr = xr_scr[:, q * nb:(q + 1) * nb].astype(BF16)
        xi = xi_scr[:, q * nb:(q + 1) * nb].astype(BF16)
        ys.append(_dot(xr, cre_ref[q]) + _dot(xi, cim_ref[q]))
    y = jnp.concatenate(ys, axis=1) + d_ref[...] * u
    z = _gelu_tanh(y)
    gate = _sigmoid(_dot(z.astype(BF16), gw_ref[...]) + gb_ref[...])
    out_ref[...] = (z * gate).astype(out_ref.dtype)


def _s5_params(a_re, a_im, log_dt, b_re, b_im, c_re, c_im):
    a_re, a_im = a_re.astype(F32), a_im.astype(F32)
    dt = jnp.exp(log_dt.astype(F32))[:, None]
    mag = jnp.exp(dt * a_re)
    ab_re, ab_im = mag * jnp.cos(dt * a_im), mag * jnp.sin(dt * a_im)
    nr, ni = ab_re - 1.0, ab_im
    den = a_re * a_re + a_im * a_im
    f_re = (nr * a_re + ni * a_im) / den
    f_im = (ni * a_re - nr * a_im) / den
    b_re, b_im = b_re.astype(F32), b_im.astype(F32)
    bb_re = f_re[..., None] * b_re - f_im[..., None] * b_im
    bb_im = f_re[..., None] * b_im + f_im[..., None] * b_re

    def apow(k):
        mk = jnp.exp(k * dt * a_re)
        return (mk * jnp.cos(k * dt * a_im)).reshape(-1), (mk * jnp.sin(k * dt * a_im)).reshape(-1)

    rows = np.arange(SUBLANES)[:, None]

    def shift_table(s, part):
        return jnp.where(jnp.asarray(rows >= s), apow(float(s))[part][None, :], 0.0)

    as_arr = jnp.stack([jnp.stack([shift_table(s, part) for s in S5_SHIFTS])
                        for part in (0, 1)])
    pws = [apow(float(k + 1)) for k in range(SUBLANES)]
    pw_arr = jnp.stack([jnp.stack([p[0] for p in pws]), jnp.stack([p[1] for p in pws])])

    gpb = S5_GROUPS // S5_BLK
    eye = jnp.eye(gpb, dtype=F32)

    def pack_b(bb):
        bb = bb.reshape(S5_BLK, gpb, S5_STATE, S5_CH)
        return jnp.einsum('qgpc,gh->qgchp', bb, eye).reshape(S5_BLK, gpb * S5_CH, gpb * S5_STATE)

    def pack_c(cc):
        cc = cc.reshape(S5_BLK, gpb, S5_CH, S5_STATE)
        return jnp.einsum('qgcp,gh->qgphc', cc, eye).reshape(S5_BLK, gpb * S5_STATE, gpb * S5_CH)

    return (pack_b(bb_re).astype(BF16), pack_b(bb_im).astype(BF16),
            pack_c(c_re.astype(F32)).astype(BF16), pack_c(-c_im.astype(F32)).astype(BF16),
            as_arr, pw_arr)


def _s5(pm, packed, d_row, glu_w, glu_b, batch, seq, out_dtype):
    tm = min(TM_S5, seq)
    nt = seq // tm
    bre, bim, cre, cim, as_arr, pw_arr = packed
    w = BRANCH_WIDTH
    ub = OFF_S5_U // w
    full = lambda *shape: pl.BlockSpec(shape, lambda b, i: (0,) * len(shape))
    return pl.pallas_call(
        functools.partial(_s5_kernel, tm=tm),
        grid=(batch, nt),
        in_specs=[
            pl.BlockSpec((tm, w), lambda b, i: (b * nt + i, ub)),
            full(S5_BLK, LANES, S5_BLK_STATE), full(S5_BLK, LANES, S5_BLK_STATE),
            full(S5_BLK, S5_BLK_STATE, LANES), full(S5_BLK, S5_BLK_STATE, LANES),
            full(2, len(S5_SHIFTS), SUBLANES, S5_NSTATE), full(2, SUBLANES, S5_NSTATE),
            full(1, w), full(w, w), full(1, w),
        ],
        out_specs=pl.BlockSpec((tm, w), lambda b, i: (b * nt + i, 0)),
        out_shape=jax.ShapeDtypeStruct((batch * seq, w), out_dtype),
        scratch_shapes=[
            pltpu.VMEM((tm, S5_NSTATE), F32),
            pltpu.VMEM((tm, S5_NSTATE), F32),
            pltpu.VMEM((1, S5_NSTATE), F32),
            pltpu.VMEM((1, S5_NSTATE), F32),
        ],
        compiler_params=_cparams(("parallel", "arbitrary")),
        name="s5",
    )(pm, bre, bim, cre, cim, as_arr, pw_arr, d_row, glu_w, glu_b)


def _merge_kernel(x_ref, mod_ref, g_ref, oa_ref, ob_ref, oc_ref, od_ref, wg_ref, bg_ref,
                  wb_ref, wo_ref, out_ref, h_scr, acc_scr):
    n = pl.program_id(1)

    @pl.when(n == 0)
    def _():
        h = _modulated_norm(x_ref[...], g_ref[0:1, :], mod_ref[0:1, :], mod_ref[1:2, :])
        h_scr[...] = h.astype(BF16)
        acc_scr[...] = jnp.zeros_like(acc_scr)

    hb = h_scr[...]
    merged = None
    for i, o_ref in enumerate((oa_ref, ob_ref, oc_ref, od_ref)):
        gate = _sigmoid(_dot(hb, wg_ref[i]) + bg_ref[i])
        term = gate * _dot(o_ref[...].astype(BF16), wb_ref[i])
        merged = term if merged is None else merged + term
    acc_scr[...] += _dot(merged.astype(BF16), wo_ref[...])

    @pl.when(n == pl.num_programs(1) - 1)
    def _():
        y = _rms(acc_scr[...]) * g_ref[1:2, :]
        out_ref[...] = x_ref[...] + mod_ref[2:3, :] * y


def _merge(x2, mod_l, gains, oa, ob, oc, od, w_gate, b_gate, w_branch, w_out, seq):
    t, d = x2.shape
    tm, tn = min(TM_MERGE, seq), TN_MERGE
    tiles_per_seq = seq // tm
    w = BRANCH_WIDTH
    br_spec = pl.BlockSpec((tm, w), lambda i, n: (i, 0))
    return pl.pallas_call(
        _merge_kernel,
        grid=(t // tm, d // tn),
        in_specs=[
            pl.BlockSpec((tm, d), lambda i, n: (i, 0)),
            pl.BlockSpec((None, SUBLANES, d), lambda i, n: (i // tiles_per_seq, 0, 0)),
            pl.BlockSpec((2, d), lambda i, n: (0, 0)),
            br_spec, br_spec, br_spec, br_spec,
            pl.BlockSpec((N_BRANCH, d, tn), lambda i, n: (0, 0, n)),
            pl.BlockSpec((N_BRANCH, 1, tn), lambda i, n: (0, 0, n)),
            pl.BlockSpec((N_BRANCH, w, tn), lambda i, n: (0, 0, n)),
            pl.BlockSpec((tn, d), lambda i, n: (n, 0)),
        ],
        out_specs=pl.BlockSpec((tm, d), lambda i, n: (i, 0)),
        out_shape=jax.ShapeDtypeStruct((t, d), F32),
        scratch_shapes=[pltpu.VMEM((tm, d), BF16), pltpu.VMEM((tm, d), F32)],
        compiler_params=_cparams(("parallel", "arbitrary")),
        name="merge",
    )(x2, mod_l, gains, oa, ob, oc, od, w_gate, b_gate, w_branch, w_out)


def _ffn_kernel(x_ref, mod_ref, g_ref, wa_ref, wg_ref, wo_ref, out_ref, h_scr, acc_scr):
    j = pl.program_id(1)

    @pl.when(j == 0)
    def _():
        h = _modulated_norm(x_ref[...], g_ref[0:1, :], mod_ref[3:4, :], mod_ref[4:5, :])
        h_scr[...] = h.astype(BF16)
        acc_scr[...] = jnp.zeros_like(acc_scr)

    hb = h_scr[...]
    a = _dot(hb, wa_ref[...])
    g = _dot(hb, wg_ref[...])
    acc_scr[...] += _dot((_silu(a) * g).astype(BF16), wo_ref[...])

    @pl.when(j == pl.num_programs(1) - 1)
    def _():
        y = _rms(acc_scr[...]) * g_ref[1:2, :]
        out_ref[...] = x_ref[...] + mod_ref[5:6, :] * y


def _ffn(x2, mod_l, gains, w_in, w_out, seq):
    t, d = x2.shape
    tm, th = min(TM_FFN, seq), TH_FFN
    tiles_per_seq = seq // tm
    nh = FFN_HIDDEN // th
    return pl.pallas_call(
        _ffn_kernel,
        grid=(t // tm, nh),
        in_specs=[
            pl.BlockSpec((tm, d), lambda i, j: (i, 0)),
            pl.BlockSpec((None, SUBLANES, d), lambda i, j: (i // tiles_per_seq, 0, 0)),
            pl.BlockSpec((2, d), lambda i, j: (0, 0)),
            pl.BlockSpec((d, th), lambda i, j: (0, j)),
            pl.BlockSpec((d, th), lambda i, j: (0, j + nh)),
            pl.BlockSpec((th, d), lambda i, j: (j, 0)),
        ],
        out_specs=pl.BlockSpec((tm, d), lambda i, j: (i, 0)),
        out_shape=jax.ShapeDtypeStruct((t, d), F32),
        scratch_shapes=[pltpu.VMEM((tm, d), BF16), pltpu.VMEM((tm, d), F32)],
        compiler_params=_cparams(("parallel", "arbitrary")),
        name="ffn",
    )(x2, mod_l, gains, w_in, w_in, w_out)


def _split_w_in(w):
    d = w.shape[0]
    main = jnp.concatenate([w[:, :3584], w[:, 3592:5128], w[:, 5144:5656]], axis=1)
    small = jnp.concatenate([w[:, 3584:3592], w[:, 5128:5144],
                             jnp.zeros((d, SMALL_W - 2 * ML_HEADS - GLA_RANK), w.dtype)], axis=1)
    return main.astype(BF16), small.astype(BF16)


def _pad_row(vals, offset):
    row = jnp.zeros((1, SMALL_W), F32)
    return lax.dynamic_update_slice(row, vals.reshape(1, -1).astype(F32), (0, offset))


ACT_DTYPE = BF16


def kernel(x, c, ada_w, ada_b, norm_g, w_in, rel_bias, diff_lambda, ml_conv, ml_gate_b,
           gla_wa2, gla_ba, s5_a_re, s5_a_im, s5_log_dt, s5_b_re, s5_b_im, s5_c_re, s5_c_im,
           s5_d, s5_glu_w, s5_glu_b, w_branch, w_gate, b_gate, w_out, ffn_w_in, ffn_w_out):
    batch, seq, d = x.shape
    depth = ada_w.shape[0]
    t = batch * seq

    c_pad = jnp.concatenate([c, jnp.zeros((SUBLANES - batch, d), c.dtype)], axis=0)
    mod = _adaln(c_pad, ada_w, ada_b)[:, :batch]
    mod = mod.reshape(depth, batch, N_MOD, d)
    mod = jnp.concatenate([mod, jnp.zeros((depth, batch, SUBLANES - N_MOD, d), F32)], axis=2)

    bias_tiles = _bias_tiles(rel_bias, min(TQ_ATT, seq))

    x2 = x.reshape(t, d)
    for l in range(depth):
        w_main, w_small = _split_w_in(w_in[l])
        pm, ps = _proj(x2, mod[l], norm_g[l, 0:1], w_main, w_small, seq, ACT_DTYPE)

        lam_init = 0.8 - 0.6 * math.exp(-0.3 * l)
        lp = diff_lambda[l].astype(F32)
        lam = (jnp.exp(jnp.sum(lp[0] * lp[1])) - jnp.exp(jnp.sum(lp[2] * lp[3])) + lam_init)
        o_a = _attention(lam.reshape(1), pm, bias_tiles, batch, seq, lam_init, BF16)

        gate_row = (_pad_row(ml_gate_b[l, 0], SM_ML_I) + _pad_row(ml_gate_b[l, 1], SM_ML_F))
        o_b = _mlstm(pm, ps, ml_conv[l].astype(F32), gate_row, batch, seq, BF16)

        wa_pad = jnp.zeros((SMALL_W, GLA_HEADS * GLA_DK), F32)
        wa_pad = lax.dynamic_update_slice(wa_pad, gla_wa2[l].astype(F32), (SM_GL_A, 0))
        o_c = _gla(pm, ps, wa_pad, gla_ba[l].reshape(1, -1).astype(F32), batch, seq, BF16)

        packed = _s5_params(s5_a_re[l], s5_a_im[l], s5_log_dt[l], s5_b_re[l], s5_b_im[l],
                            s5_c_re[l], s5_c_im[l])
        o_d = _s5(pm, packed, s5_d[l].reshape(1, -1).astype(F32), s5_glu_w[l].astype(BF16),
                  s5_glu_b[l].reshape(1, -1).astype(F32), batch, seq, BF16)

        x2 = _merge(x2, mod[l], norm_g[l, 0:2], o_a, o_b, o_c, o_d,
                    w_gate[l].astype(BF16), b_gate[l].reshape(N_BRANCH, 1, d).astype(F32),
                    w_branch[l].astype(BF16), w_out[l].astype(BF16), seq)
        x2 = _ffn(x2, mod[l], norm_g[l, 2:4], ffn_w_in[l].astype(BF16),
                  ffn_w_out[l].astype(BF16), seq)
    return x2.reshape(batch, seq, d)
```

```python
import functools
import math

import numpy as np
import jax
import jax.numpy as jnp
from jax import lax
from jax.experimental import pallas as pl
from jax.experimental.pallas import tpu as pltpu

F32 = jnp.float32
BF16 = jnp.bfloat16
HIGHEST = lax.Precision.HIGHEST

D_MODEL = 2048
DEPTH = 4
EPS = 1e-6
N_MOD = 6
N_BRANCH = 4
BRANCH_WIDTH = 512
DA_HEADS = 4
DA_QK_DIM = 64
DA_V_DIM = 128
N_BUCKETS = 32
MAX_DISTANCE = 128
ML_HEADS = 4
ML_DIM = 128
ML_CONV = 4
GLA_HEADS = 4
GLA_DK = 64
GLA_DV = 128
GLA_RANK = 16
GLA_TAU = 16.0
S5_CH = 16
S5_GROUPS = BRANCH_WIDTH // S5_CH
S5_STATE = 64
FFN_HIDDEN = -(-(8 * D_MODEL) // (3 * 256)) * 256

LANES = 128
SUBLANES = 8
BF16_ROWS = 16
VMEM_LIMIT = 56 * 1024 * 1024

MAIN_W = 5632
SMALL_W = LANES
OFF_DA_Q, OFF_DA_K, OFF_DA_V = 0, 512, 1024
OFF_ML_Q, OFF_ML_K, OFF_ML_V, OFF_ML_O = 1536, 2048, 2560, 3072
OFF_GL_Q, OFF_GL_K, OFF_GL_V, OFF_GL_R = 3584, 3840, 4096, 4608
OFF_S5_U = 5120
SM_ML_I, SM_ML_F, SM_GL_A = 0, 4, 8

NEG = -1e30
LOG2E = math.log2(math.e)

TM_PROJ, TN_PROJ = 1024, 512
TQ_ATT = 512
L_MLSTM = 256
L_GLA, C_GLA = 256, 16
TM_S5 = 256
TM_MERGE, TN_MERGE = 512, 256
TM_FFN, TH_FFN = 512, 512


def _cparams(sem):
    return pltpu.CompilerParams(dimension_semantics=sem, vmem_limit_bytes=VMEM_LIMIT)


def _rms(x):
    return x * lax.rsqrt(jnp.mean(x * x, axis=-1, keepdims=True) + EPS)


def _sigmoid(x):
    return 1.0 / (1.0 + jnp.exp(-x))


def _silu(x):
    return x * _sigmoid(x)


def _log_sigmoid(x):
    return jnp.minimum(x, 0.0) - jnp.log1p(jnp.exp(-jnp.abs(x)))


def _dot(a, b):
    return jnp.dot(a, b, preferred_element_type=F32)


def _dot_nt(a, b):
    return lax.dot_general(a, b, (((1,), (1,)), ((), ())), preferred_element_type=F32)


def _dot_tn(a, b):
    return lax.dot_general(a, b, (((0,), (0,)), ((), ())), preferred_element_type=F32)


def _dot_exact(a, b):
    return jnp.dot(a, b, preferred_element_type=F32, precision=HIGHEST)


def _adaln_kernel(c_ref, w_ref, b_ref, o_ref):
    c = c_ref[...]
    o_ref[...] = _dot_exact(_silu(c), w_ref[...]) + b_ref[...]


def _adaln(c_pad, ada_w, ada_b):
    depth, d, n = ada_w.shape
    rows = c_pad.shape[0]
    tn = 1024
    return pl.pallas_call(
        _adaln_kernel,
        grid=(depth, n // tn),
        in_specs=[
            pl.BlockSpec((rows, d), lambda l, j: (0, 0)),
            pl.BlockSpec((None, d, tn), lambda l, j: (l, 0, j)),
            pl.BlockSpec((None, 1, tn), lambda l, j: (l, 0, j)),
        ],
        out_specs=pl.BlockSpec((None, rows, tn), lambda l, j: (l, 0, j)),
        out_shape=jax.ShapeDtypeStruct((depth, rows, n), F32),
        compiler_params=_cparams(("parallel", "parallel")),
        name="adaln",
    )(c_pad, ada_w, ada_b.reshape(depth, 1, n))


ROW_CHUNK = 32


def _for_row_chunks(n_rows, fn):
    def body(c, carry):
        fn(pl.ds(pl.multiple_of(c * ROW_CHUNK, ROW_CHUNK), ROW_CHUNK))
        return carry

    lax.fori_loop(0, n_rows // ROW_CHUNK, body, 0, unroll=4)


def _modulated_norm_into(h_scr, x_ref, gain, shift, scale):
    gs = gain * (1.0 + scale)

    def chunk(rows):
        h_scr[rows, :] = (_rms(x_ref[rows, :]) * gs + shift).astype(h_scr.dtype)

    _for_row_chunks(h_scr.shape[0], chunk)


def _gated_residual_into(out_ref, x_ref, acc_scr, gain, gate):
    gg = gain * gate

    def chunk(rows):
        out_ref[rows, :] = x_ref[rows, :] + _rms(acc_scr[rows, :]) * gg

    _for_row_chunks(out_ref.shape[0], chunk)


def _proj_kernel(x_ref, mod_ref, g_ref, w_ref, ws_ref, o_ref, os_ref, h_scr):
    j = pl.program_id(1)

    @pl.when(j == 0)
    def _():
        _modulated_norm_into(h_scr, x_ref, g_ref[...], mod_ref[0:1, :], mod_ref[1:2, :])
        os_ref[...] = _dot(h_scr[...], ws_ref[...])

    o_ref[...] = _dot(h_scr[...], w_ref[...]).astype(o_ref.dtype)


def _proj(x2, mod_l, gain, w_main, w_small, seq, out_dtype):
    t, d = x2.shape
    tm, tn = min(TM_PROJ, seq), TN_PROJ
    tiles_per_seq = seq // tm
    return pl.pallas_call(
        _proj_kernel,
        grid=(t // tm, MAIN_W // tn),
        in_specs=[
            pl.BlockSpec((tm, d), lambda i, j: (i, 0)),
            pl.BlockSpec((None, SUBLANES, d), lambda i, j: (i // tiles_per_seq, 0, 0)),
            pl.BlockSpec((1, d), lambda i, j: (0, 0)),
            pl.BlockSpec((d, tn), lambda i, j: (0, j)),
            pl.BlockSpec((d, SMALL_W), lambda i, j: (0, 0)),
        ],
        out_specs=[
            pl.BlockSpec((tm, tn), lambda i, j: (i, j)),
            pl.BlockSpec((tm, SMALL_W), lambda i, j: (i, 0)),
        ],
        out_shape=[
            jax.ShapeDtypeStruct((t, MAIN_W), out_dtype),
            jax.ShapeDtypeStruct((t, SMALL_W), F32),
        ],
        scratch_shapes=[pltpu.VMEM((tm, d), BF16)],
        compiler_params=_cparams(("parallel", "arbitrary")),
        name="proj",
    )(x2, mod_l, gain, w_main, w_small)


def _attn_kernel(lam_ref, q_ref, k_ref, v_ref, bias_ref, o_ref, m_scr, l_scr, acc_scr,
                 s0_scr, s1_scr, *, tq, out_scale):
    i = pl.program_id(2)
    lam = lam_ref[0]
    q = q_ref[...].astype(F32) * (DA_QK_DIM ** -0.5 * LOG2E)
    lane = lax.broadcasted_iota(jnp.int32, q.shape, 1)
    qa = jnp.where(lane < DA_QK_DIM, q, 0.0).astype(BF16)
    qb = jnp.where(lane >= DA_QK_DIM, q, 0.0).astype(BF16)
    q2 = jnp.concatenate([qa, qb], axis=0)

    m_scr[...] = jnp.full_like(m_scr, NEG)
    l_scr[...] = jnp.zeros_like(l_scr)
    acc_scr[...] = jnp.zeros_like(acc_scr)

    def scores(j):
        r0 = pl.multiple_of(j * tq, tq)
        return _dot_nt(k_ref[pl.ds(r0, tq), :].astype(BF16), q2)

    def accumulate(j, s):
        r0 = pl.multiple_of(j * tq, tq)
        vt = v_ref[pl.ds(r0, tq), :].astype(BF16)
        m_old = m_scr[...]
        m_new = jnp.maximum(m_old, jnp.max(s, axis=0, keepdims=True))
        p = jnp.exp2(s - m_new)
        alpha = jnp.exp2(m_old - m_new)
        l_scr[...] = alpha * l_scr[...] + jnp.sum(p, axis=0, keepdims=True)
        acc_scr[...] = alpha * acc_scr[...] + _dot_tn(vt, p.astype(BF16))
        m_scr[...] = m_new

    def biased_scores(j, bias):
        return scores(j) + jnp.concatenate([bias, bias], axis=1)

    n_far = jnp.maximum(i - 1, 0)
    pairs = n_far // 2
    j_prev = jnp.maximum(i - 1, 0)
    s0_scr[...] = biased_scores(i, bias_ref[1])
    s1_scr[...] = biased_scores(j_prev, bias_ref[jnp.where(i >= 1, 0, 2)])
    accumulate(i, s0_scr[...])
    s0_scr[...] = scores(0)
    accumulate(j_prev, s1_scr[...])

    def far_pair(g, carry):
        s1_scr[...] = scores(2 * g + 1)
        accumulate(2 * g, s0_scr[...])
        s0_scr[...] = scores(jnp.minimum(2 * g + 2, n_far - 1))
        accumulate(2 * g + 1, s1_scr[...])
        return carry

    lax.fori_loop(0, pairs, far_pair, 0)

    @pl.when(n_far > 2 * pairs)
    def _():
        accumulate(n_far - 1, s0_scr[...])

    on = acc_scr[...] / l_scr[...]
    ot = on[:, 0:tq] - lam * on[:, tq:2 * tq]
    ot = ot * (lax.rsqrt(jnp.mean(ot * ot, axis=0, keepdims=True) + EPS) * out_scale)
    o_ref[...] = ot.T.astype(o_ref.dtype)


def _attention(lam, pm, bias_tiles, batch, seq, lam_init, out_dtype):
    tq = min(TQ_ATT, seq)
    nq = seq // tq
    kern = functools.partial(_attn_kernel, tq=tq, out_scale=1.0 - lam_init)
    scratch = [pltpu.VMEM((1, 2 * tq), F32), pltpu.VMEM((1, 2 * tq), F32),
               pltpu.VMEM((DA_V_DIM, 2 * tq), F32),
               pltpu.VMEM((tq, 2 * tq), F32), pltpu.VMEM((tq, 2 * tq), F32)]
    qb, kb, vb = OFF_DA_Q // LANES, OFF_DA_K // LANES, OFF_DA_V // LANES
    return pl.pallas_call(
        kern,
        grid=(batch, DA_HEADS, nq),
        in_specs=[
            pl.BlockSpec(memory_space=pltpu.SMEM),
            pl.BlockSpec((tq, LANES), lambda b, h, i: (b * nq + i, qb + h)),
            pl.BlockSpec((seq, LANES), lambda b, h, i: (b, kb + h)),
            pl.BlockSpec((seq, LANES), lambda b, h, i: (b, vb + h)),
            pl.BlockSpec((None, 3, tq, tq), lambda b, h, i: (h, 0, 0, 0)),
        ],
        out_specs=pl.BlockSpec((tq, LANES), lambda b, h, i: (b * nq + i, h)),
        out_shape=jax.ShapeDtypeStruct((batch * seq, DA_HEADS * DA_V_DIM), out_dtype),
        scratch_shapes=scratch,
        compiler_params=_cparams(("parallel", "parallel", "arbitrary")),
        name="diff_attn",
    )(lam, pm, pm, pm, bias_tiles)


def _t5_bucket_table(n_max):
    n = np.arange(n_max)
    exact = N_BUCKETS // 2
    nf = np.maximum(n, 1).astype(np.float64)
    large = exact + (np.log(nf / exact) / math.log(MAX_DISTANCE / exact)
                     * (N_BUCKETS - exact)).astype(np.int64)
    return np.where(n < exact, n, np.minimum(large, N_BUCKETS - 1)).astype(np.int32)


def _bias_tiles(rel_bias, tq):
    assert tq >= MAX_DISTANCE
    n = tq
    heads = rel_bias.shape[1]
    rb = rel_bias.astype(F32)
    near = (rb[_t5_bucket_table(MAX_DISTANCE)] - rb[N_BUCKETS - 1][None, :]) * LOG2E
    f = jnp.concatenate([near, jnp.zeros((2 * n - MAX_DISTANCE, heads), F32)], axis=0)

    def toeplitz(v):
        vp = jnp.concatenate([v, jnp.zeros((1, heads), F32)], axis=0)
        flat = jnp.tile(vp, (n, 1))[: n * (2 * n - 1)]
        return jnp.transpose(flat.reshape(n, 2 * n - 1, heads)[:, n - 1:, :], (2, 0, 1))

    prev = toeplitz(f[1:2 * n])
    diag = toeplitz(jnp.concatenate([jnp.full((n - 1, heads), NEG, F32), f[0:n]], axis=0))
    masked = jnp.full_like(prev, NEG)
    return jnp.stack([prev, diag, masked], axis=1)


def _mlstm_kernel(q_ref, k_ref, v_ref, og_ref, sm_ref, qh_ref, kh_ref, cw_ref, gb_ref,
                  tril_ref, out_ref, xq_scr, xk_scr, c_scr, n_scr, m_scr, *, L):
    ci = pl.program_id(1)

    @pl.when(ci == 0)
    def _():
        c_scr[...] = jnp.zeros_like(c_scr)
        n_scr[...] = jnp.zeros_like(n_scr)
        m_scr[...] = jnp.zeros_like(m_scr)

    halo = BF16_ROWS
    keep = (ci > 0).astype(F32)
    xq_scr[0:halo, :] = qh_ref[...].astype(F32) * keep
    xk_scr[0:halo, :] = kh_ref[...].astype(F32) * keep
    xq_scr[halo:halo + L, :] = q_ref[...].astype(F32)
    xk_scr[halo:halo + L, :] = k_ref[...].astype(F32)

    def conv_silu(scr, w):
        y = scr[halo:halo + L, :] * w[ML_CONV - 1:ML_CONV, :]
        for j in range(ML_CONV - 1):
            off = halo - (ML_CONV - 1) + j
            y = y + scr[off:off + L, :] * w[j:j + 1, :]
        return _silu(y)

    w_all = cw_ref[...]
    width = ML_HEADS * ML_DIM
    q_all = conv_silu(xq_scr, w_all[:, 0:width])
    k_all = conv_silu(xk_scr, w_all[:, width:2 * width]) * (ML_DIM ** -0.5)

    g = sm_ref[...] + gb_ref[...]
    lane = lax.broadcasted_iota(jnp.int32, g.shape, 1)
    is_f = (lane >= SM_ML_F) & (lane < SM_ML_F + ML_HEADS)
    g = jnp.where(is_f, _log_sigmoid(g), g)
    bcum = _dot_exact(tril_ref[...], g)
    g_t = g.T
    b_t = bcum.T
    row = lax.broadcasted_iota(jnp.int32, (L, L), 0)
    colj = lax.broadcasted_iota(jnp.int32, (L, L), 1)
    causal = colj <= row

    for h in range(ML_HEADS):
        sl = slice(h * ML_DIM, (h + 1) * ML_DIM)
        qh = q_all[:, sl]
        kh = k_all[:, sl]
        vh = v_ref[:, sl].astype(F32)
        qb, kb, vb = qh.astype(BF16), kh.astype(BF16), vh.astype(BF16)
        ig_col = g[:, SM_ML_I + h:SM_ML_I + h + 1]
        b_col = bcum[:, SM_ML_F + h:SM_ML_F + h + 1]
        ig_row = g_t[SM_ML_I + h:SM_ML_I + h + 1, :]
        b_row = b_t[SM_ML_F + h:SM_ML_F + h + 1, :]
        m_old = m_scr[h][:, 0:1]
        c_old = c_scr[h]
        n_old = n_scr[h]

        dm = jnp.where(causal, b_col - b_row + ig_row, NEG)
        inter = b_col + m_old
        m_t = jnp.maximum(inter, jnp.max(dm, axis=-1, keepdims=True))
        s = _dot_nt(qb, kb) * jnp.exp(dm - m_t)
        a = jnp.exp(inter - m_t)
        num = a * _dot(qb, c_old.astype(BF16)) + _dot(s.astype(BF16), vb)
        den = (a * jnp.sum(qh * n_old, axis=-1, keepdims=True)
               + jnp.sum(s, axis=-1, keepdims=True))
        hv = num / jnp.maximum(jnp.abs(den), jnp.exp(-m_t))
        out_ref[:, sl] = (_sigmoid(og_ref[:, sl].astype(F32)) * hv).astype(out_ref.dtype)

        m_new = m_t[L - 1:L, :]
        b_last = b_col[L - 1:L, :]
        a_state = jnp.exp(b_last + m_old - m_new)
        w_col = jnp.exp(b_last - b_col + ig_col - m_new)
        kw = kh * w_col
        c_scr[h] = a_state * c_old + _dot_tn(kw.astype(BF16), vb)
        n_scr[h] = a_state * n_old + jnp.sum(kw, axis=0, keepdims=True)
        m_scr[h] = jnp.broadcast_to(m_new, (1, LANES))


def _mlstm(pm, ps, conv_w, gate_row, batch, seq, out_dtype):
    L = min(L_MLSTM, seq)
    nc = seq // L
    width = ML_HEADS * ML_DIM
    qb, kb, vb, ob = (OFF_ML_Q // width, OFF_ML_K // width, OFF_ML_V // width, OFF_ML_O // width)
    lb = L // BF16_ROWS
    tril = jnp.asarray(np.tril(np.ones((L, L), np.float32)))

    def halo_map(colblk):
        return lambda b, c: (jnp.maximum(b * (seq // BF16_ROWS) + c * lb - 1, 0), colblk)

    return pl.pallas_call(
        functools.partial(_mlstm_kernel, L=L),
        grid=(batch, nc),
        in_specs=[
            pl.BlockSpec((L, width), lambda b, c: (b * nc + c, qb)),
            pl.BlockSpec((L, width), lambda b, c: (b * nc + c, kb)),
            pl.BlockSpec((L, width), lambda b, c: (b * nc + c, vb)),
            pl.BlockSpec((L, width), lambda b, c: (b * nc + c, ob)),
            pl.BlockSpec((L, SMALL_W), lambda b, c: (b * nc + c, 0)),
            pl.BlockSpec((BF16_ROWS, width), halo_map(qb)),
            pl.BlockSpec((BF16_ROWS, width), halo_map(kb)),
            pl.BlockSpec((ML_CONV, 2 * width), lambda b, c: (0, 0)),
            pl.BlockSpec((1, SMALL_W), lambda b, c: (0, 0)),
            pl.BlockSpec((L, L), lambda b, c: (0, 0)),
        ],
        out_specs=pl.BlockSpec((L, width), lambda b, c: (b * nc + c, 0)),
        out_shape=jax.ShapeDtypeStruct((batch * seq, width), out_dtype),
        scratch_shapes=[
            pltpu.VMEM((L + BF16_ROWS, width), F32),
            pltpu.VMEM((L + BF16_ROWS, width), F32),
            pltpu.VMEM((ML_HEADS, ML_DIM, ML_DIM), F32),
            pltpu.VMEM((ML_HEADS, 1, ML_DIM), F32),
            pltpu.VMEM((ML_HEADS, 1, LANES), F32),
        ],
        compiler_params=_cparams(("parallel", "arbitrary")),
        name="mlstm",
    )(pm, pm, pm, pm, ps, pm, pm, conv_w, gate_row, tril)


def _gla_kernel(q_ref, k_ref, v_ref, r_ref, sm_ref, wa_ref, ba_ref, tril_ref, mexp_ref,
                out_ref, bc_scr, a_scr, st_scr, *, L, c):
    ci = pl.program_id(1)

    @pl.when(ci == 0)
    def _():
        st_scr[...] = jnp.zeros_like(st_scr)

    la = _log_sigmoid(_dot_exact(sm_ref[...], wa_ref[...]) + ba_ref[...]) * (1.0 / GLA_TAU)
    bc_scr[...] = _dot_exact(tril_ref[...], la)
    kw = GLA_HEADS * GLA_DK
    rowc = lax.broadcasted_iota(jnp.int32, (c, kw), 0)
    lane_head = lax.broadcasted_iota(jnp.int32, (c, kw), 1) // GLA_DK
    head_masks = [lane_head == h for h in range(GLA_HEADS)]

    def stack_heads(x):
        return jnp.concatenate([jnp.where(mk, x, 0.0) for mk in head_masks], axis=0)

    def sub(i, carry):
        r0 = pl.multiple_of(i * c, c)
        qs = q_ref[pl.ds(r0, c), :].astype(F32) * (GLA_DK ** -0.5)
        ks = k_ref[pl.ds(r0, c), :].astype(F32)
        vs = v_ref[pl.ds(r0, c), :].astype(BF16).astype(F32)
        bcs = bc_scr[pl.ds(r0, c), :]
        e_end = bcs[c - 1:c, :]
        st = st_scr[...]
        o_stack = _dot_nt(stack_heads(qs * jnp.exp(bcs)).astype(BF16), st.astype(BF16))
        o = jnp.concatenate([o_stack[h * c:(h + 1) * c] for h in range(GLA_HEADS)], axis=1)

        for t in range(c):
            dec = jnp.exp(jnp.minimum(bcs[t:t + 1, :] - bcs, 0.0))
            a_t = jnp.where(rowc <= t, qs[t:t + 1, :] * ks * dec, 0.0)
            a_scr[t * c:(t + 1) * c, :] = a_t.astype(BF16)
        p = _dot(a_scr[...], mexp_ref[...])
        o = o + jnp.sum(p.reshape(c, c, GLA_HEADS * GLA_DV) * vs[None, :, :], axis=1)

        outs = [_rms(o[:, h * GLA_DV:(h + 1) * GLA_DV]) for h in range(GLA_HEADS)]
        on = jnp.concatenate(outs, axis=1)
        out_ref[pl.ds(r0, c), :] = (on * _silu(r_ref[pl.ds(r0, c), :].astype(F32))).astype(out_ref.dtype)

        khat = stack_heads(ks * jnp.exp(e_end - bcs)).astype(BF16)
        v_stack = jnp.concatenate([vs[:, h * GLA_DV:(h + 1) * GLA_DV] for h in range(GLA_HEADS)],
                                  axis=0).astype(BF16)
        st_scr[...] = st * jnp.exp(e_end) + _dot_tn(v_stack, khat)
        return carry

    lax.fori_loop(0, L // c, sub, 0, unroll=2)


def _gla(pm, ps, wa_pad, ba_row, batch, seq, out_dtype):
    L = min(L_GLA, seq)
    c = C_GLA
    nc = seq // L
    kw, vw = GLA_HEADS * GLA_DK, GLA_HEADS * GLA_DV
    qb, kb, vb, rb = OFF_GL_Q // kw, OFF_GL_K // kw, OFF_GL_V // vw, OFF_GL_R // vw
    idx = np.arange(L)
    tril = ((idx[:, None] >= idx[None, :]) & (idx[:, None] // c == idx[None, :] // c))
    tril = jnp.asarray(tril.astype(np.float32))
    mexp = np.zeros((kw, vw), np.float32)
    for h in range(GLA_HEADS):
        mexp[h * GLA_DK:(h + 1) * GLA_DK, h * GLA_DV:(h + 1) * GLA_DV] = 1.0
    mexp = jnp.asarray(mexp, dtype=BF16)
    return pl.pallas_call(
        functools.partial(_gla_kernel, L=L, c=c),
        grid=(batch, nc),
        in_specs=[
            pl.BlockSpec((L, kw), lambda b, i: (b * nc + i, qb)),
            pl.BlockSpec((L, kw), lambda b, i: (b * nc + i, kb)),
            pl.BlockSpec((L, vw), lambda b, i: (b * nc + i, vb)),
            pl.BlockSpec((L, vw), lambda b, i: (b * nc + i, rb)),
            pl.BlockSpec((L, SMALL_W), lambda b, i: (b * nc + i, 0)),
            pl.BlockSpec((SMALL_W, kw), lambda b, i: (0, 0)),
            pl.BlockSpec((1, kw), lambda b, i: (0, 0)),
            pl.BlockSpec((L, L), lambda b, i: (0, 0)),
            pl.BlockSpec((kw, vw), lambda b, i: (0, 0)),
        ],
        out_specs=pl.BlockSpec((L, vw), lambda b, i: (b * nc + i, 0)),
        out_shape=jax.ShapeDtypeStruct((batch * seq, vw), out_dtype),
        scratch_shapes=[
            pltpu.VMEM((L, kw), F32),
            pltpu.VMEM((c * c, kw), BF16),
            pltpu.VMEM((GLA_DV, kw), F32),
        ],
        compiler_params=_cparams(("parallel", "arbitrary")),
        name="gla",
    )(pm, pm, pm, pm, ps, wa_pad, ba_row, tril, mexp)


S5_NSTATE = S5_GROUPS * S5_STATE
S5_BLK = 4
S5_BLK_STATE = S5_NSTATE // S5_BLK
S5_SHIFTS = (1, 2, 4)


def _gelu_tanh(x):
    return 0.5 * x * (1.0 + jnp.tanh(math.sqrt(2.0 / math.pi) * (x + 0.044715 * (x * x * x))))


def _s5_kernel(u_ref, bre_ref, bim_ref, cre_ref, cim_ref, as_ref, pw_ref, d_ref, gw_ref,
               gb_ref, out_ref, xr_scr, xi_scr, cr_scr, ci_scr, *, tm):
    ti = pl.program_id(1)

    @pl.when(ti == 0)
    def _():
        cr_scr[...] = jnp.zeros_like(cr_scr)
        ci_scr[...] = jnp.zeros_like(ci_scr)

    u = u_ref[...].astype(F32)
    ub = u.astype(BF16)
    nb = S5_BLK_STATE
    for q in range(S5_BLK):
        uq = ub[:, q * LANES:(q + 1) * LANES]
        xr_scr[:, q * nb:(q + 1) * nb] = _dot(uq, bre_ref[q])
        xi_scr[:, q * nb:(q + 1) * nb] = _dot(uq, bim_ref[q])

    for cc in range(S5_BLK):
        cols = slice(cc * nb, (cc + 1) * nb)

        def body(g, carry, cols=cols):
            cr, ci = carry
            r0 = pl.multiple_of(g * SUBLANES, SUBLANES)
            zr = xr_scr[pl.ds(r0, SUBLANES), cols]
            zi = xi_scr[pl.ds(r0, SUBLANES), cols]
            for si, s in enumerate(S5_SHIFTS):
                sr = pltpu.roll(zr, s, 0)
                sim = pltpu.roll(zi, s, 0)
                ar = as_ref[0, si, :, cols]
                ai = as_ref[1, si, :, cols]
                zr, zi = zr + ar * sr - ai * sim, zi + ar * sim + ai * sr
            p_r = pw_ref[0, :, cols]
            p_i = pw_ref[1, :, cols]
            xr = zr + p_r * cr - p_i * ci
            xi = zi + p_r * ci + p_i * cr
            xr_scr[pl.ds(r0, SUBLANES), cols] = xr
            xi_scr[pl.ds(r0, SUBLANES), cols] = xi
            return xr[SUBLANES - 1:SUBLANES, :], xi[SUBLANES - 1:SUBLANES, :]

        cr, ci = lax.fori_loop(0, tm // SUBLANES, body, (cr_scr[:, cols], ci_scr[:, cols]))
        cr_scr[:, cols] = cr
        ci_scr[:, cols] = ci

    ys = []
    for q in range(S5_BLK):
        xr = xr_scr[:, q * nb:(q + 1) * nb].astype(BF16)
        xi = xi_scr[:, q * nb:(q + 1) * nb].astype(BF16)
        ys.append(_dot(xr, cre_ref[q]) + _dot(xi, cim_ref[q]))
    y = jnp.concatenate(ys, axis=1) + d_ref[...] * u
    z = _gelu_tanh(y)
    gate = _sigmoid(_dot(z.astype(BF16), gw_ref[...]) + gb_ref[...])
    out_ref[...] = (z * gate).astype(out_ref.dtype)


def _s5_params(a_re, a_im, log_dt, b_re, b_im, c_re, c_im):
    a_re, a_im = a_re.astype(F32), a_im.astype(F32)
    dt = jnp.exp(log_dt.astype(F32))[:, None]
    mag = jnp.exp(dt * a_re)
    ab_re, ab_im = mag * jnp.cos(dt * a_im), mag * jnp.sin(dt * a_im)
    nr, ni = ab_re - 1.0, ab_im
    den = a_re * a_re + a_im * a_im
    f_re = (nr * a_re + ni * a_im) / den
    f_im = (ni * a_re - nr * a_im) / den
    b_re, b_im = b_re.astype(F32), b_im.astype(F32)
    bb_re = f_re[..., None] * b_re - f_im[..., None] * b_im
    bb_im = f_re[..., None] * b_im + f_im[..., None] * b_re

    def apow(k):
        mk = jnp.exp(k * dt * a_re)
        return (mk * jnp.cos(k * dt * a_im)).reshape(-1), (mk * jnp.sin(k * dt * a_im)).reshape(-1)

    rows = np.arange(SUBLANES)[:, None]

    def shift_table(s, part):
        return jnp.where(jnp.asarray(rows >= s), apow(float(s))[part][None, :], 0.0)

    as_arr = jnp.stack([jnp.stack([shift_table(s, part) for s in S5_SHIFTS])
                        for part in (0, 1)])
    pws = [apow(float(k + 1)) for k in range(SUBLANES)]
    pw_arr = jnp.stack([jnp.stack([p[0] for p in pws]), jnp.stack([p[1] for p in pws])])

    gpb = S5_GROUPS // S5_BLK
    eye = jnp.eye(gpb, dtype=F32)

    def pack_b(bb):
        bb = bb.reshape(S5_BLK, gpb, S5_STATE, S5_CH)
        return jnp.einsum('qgpc,gh->qgchp', bb, eye).reshape(S5_BLK, gpb * S5_CH, gpb * S5_STATE)

    def pack_c(cc):
        cc = cc.reshape(S5_BLK, gpb, S5_CH, S5_STATE)
        return jnp.einsum('qgcp,gh->qgphc', cc, eye).reshape(S5_BLK, gpb * S5_STATE, gpb * S5_CH)

    return (pack_b(bb_re).astype(BF16), pack_b(bb_im).astype(BF16),
            pack_c(c_re.astype(F32)).astype(BF16), pack_c(-c_im.astype(F32)).astype(BF16),
            as_arr, pw_arr)


def _s5(pm, packed, d_row, glu_w, glu_b, batch, seq, out_dtype):
    tm = min(TM_S5, seq)
    nt = seq // tm
    bre, bim, cre, cim, as_arr, pw_arr = packed
    w = BRANCH_WIDTH
    ub = OFF_S5_U // w
    full = lambda *shape: pl.BlockSpec(shape, lambda b, i: (0,) * len(shape))
    return pl.pallas_call(
        functools.partial(_s5_kernel, tm=tm),
        grid=(batch, nt),
        in_specs=[
            pl.BlockSpec((tm, w), lambda b, i: (b * nt + i, ub)),
            full(S5_BLK, LANES, S5_BLK_STATE), full(S5_BLK, LANES, S5_BLK_STATE),
            full(S5_BLK, S5_BLK_STATE, LANES), full(S5_BLK, S5_BLK_STATE, LANES),
            full(2, len(S5_SHIFTS), SUBLANES, S5_NSTATE), full(2, SUBLANES, S5_NSTATE),
            full(1, w), full(w, w), full(1, w),
        ],
        out_specs=pl.BlockSpec((tm, w), lambda b, i: (b * nt + i, 0)),
        out_shape=jax.ShapeDtypeStruct((batch * seq, w), out_dtype),
        scratch_shapes=[
            pltpu.VMEM((tm, S5_NSTATE), F32),
            pltpu.VMEM((tm, S5_NSTATE), F32),
            pltpu.VMEM((1, S5_NSTATE), F32),
            pltpu.VMEM((1, S5_NSTATE), F32),
        ],
        compiler_params=_cparams(("parallel", "arbitrary")),
        name="s5",
    )(pm, bre, bim, cre, cim, as_arr, pw_arr, d_row, glu_w, glu_b)


def _merge_kernel(x_ref, mod_ref, g_ref, oa_ref, ob_ref, oc_ref, od_ref, wg_ref, bg_ref,
                  wb_ref, wo_ref, out_ref, h_scr, acc_scr):
    n = pl.program_id(1)

    @pl.when(n == 0)
    def _():
        _modulated_norm_into(h_scr, x_ref, g_ref[0:1, :], mod_ref[0:1, :], mod_ref[1:2, :])
        acc_scr[...] = jnp.zeros_like(acc_scr)

    hb = h_scr[...]
    merged = None
    for i, o_ref in enumerate((oa_ref, ob_ref, oc_ref, od_ref)):
        gate = _sigmoid(_dot(hb, wg_ref[i]) + bg_ref[i])
        term = gate * _dot(o_ref[...].astype(BF16), wb_ref[i])
        merged = term if merged is None else merged + term
    acc_scr[...] += _dot(merged.astype(BF16), wo_ref[...])

    @pl.when(n == pl.num_programs(1) - 1)
    def _():
        _gated_residual_into(out_ref, x_ref, acc_scr, g_ref[1:2, :], mod_ref[2:3, :])


def _merge(x2, mod_l, gains, oa, ob, oc, od, w_gate, b_gate, w_branch, w_out, seq):
    t, d = x2.shape
    tm, tn = min(TM_MERGE, seq), TN_MERGE
    tiles_per_seq = seq // tm
    w = BRANCH_WIDTH
    br_spec = pl.BlockSpec((tm, w), lambda i, n: (i, 0))
    return pl.pallas_call(
        _merge_kernel,
        grid=(t // tm, d // tn),
        in_specs=[
            pl.BlockSpec((tm, d), lambda i, n: (i, 0)),
            pl.BlockSpec((None, SUBLANES, d), lambda i, n: (i // tiles_per_seq, 0, 0)),
            pl.BlockSpec((2, d), lambda i, n: (0, 0)),
            br_spec, br_spec, br_spec, br_spec,
            pl.BlockSpec((N_BRANCH, d, tn), lambda i, n: (0, 0, n)),
            pl.BlockSpec((N_BRANCH, 1, tn), lambda i, n: (0, 0, n)),
            pl.BlockSpec((N_BRANCH, w, tn), lambda i, n: (0, 0, n)),
            pl.BlockSpec((tn, d), lambda i, n: (n, 0)),
        ],
        out_specs=pl.BlockSpec((tm, d), lambda i, n: (i, 0)),
        out_shape=jax.ShapeDtypeStruct((t, d), F32),
        scratch_shapes=[pltpu.VMEM((tm, d), BF16), pltpu.VMEM((tm, d), F32)],
        compiler_params=_cparams(("parallel", "arbitrary")),
        name="merge",
    )(x2, mod_l, gains, oa, ob, oc, od, w_gate, b_gate, w_branch, w_out)


def _ffn_kernel(x_ref, mod_ref, g_ref, wa_ref, wg_ref, wo_ref, out_ref, h_scr, acc_scr):
    j = pl.program_id(1)

    @pl.when(j == 0)
    def _():
        _modulated_norm_into(h_scr, x_ref, g_ref[0:1, :], mod_ref[3:4, :], mod_ref[4:5, :])
        acc_scr[...] = jnp.zeros_like(acc_scr)

    hb = h_scr[...]
    a = _dot(hb, wa_ref[...])
    g = _dot(hb, wg_ref[...])
    acc_scr[...] += _dot((_silu(a) * g).astype(BF16), wo_ref[...])

    @pl.when(j == pl.num_programs(1) - 1)
    def _():
        _gated_residual_into(out_ref, x_ref, acc_scr, g_ref[1:2, :], mod_ref[5:6, :])


def _ffn(x2, mod_l, gains, w_in, w_out, seq):
    t, d = x2.shape
    tm, th = min(TM_FFN, seq), TH_FFN
    tiles_per_seq = seq // tm
    nh = FFN_HIDDEN // th
    return pl.pallas_call(
        _ffn_kernel,
        grid=(t // tm, nh),
        in_specs=[
            pl.BlockSpec((tm, d), lambda i, j: (i, 0)),
            pl.BlockSpec((None, SUBLANES, d), lambda i, j: (i // tiles_per_seq, 0, 0)),
            pl.BlockSpec((2, d), lambda i, j: (0, 0)),
            pl.BlockSpec((d, th), lambda i, j: (0, j)),
            pl.BlockSpec((d, th), lambda i, j: (0, j + nh)),
            pl.BlockSpec((th, d), lambda i, j: (j, 0)),
        ],
        out_specs=pl.BlockSpec((tm, d), lambda i, j: (i, 0)),
        out_shape=jax.ShapeDtypeStruct((t, d), F32),
        scratch_shapes=[pltpu.VMEM((tm, d), BF16), pltpu.VMEM((tm, d), F32)],
        compiler_params=_cparams(("parallel", "arbitrary")),
        name="ffn",
    )(x2, mod_l, gains, w_in, w_in, w_out)


def _split_w_in(w):
    d = w.shape[0]
    main = jnp.concatenate([w[:, :3584], w[:, 3592:5128], w[:, 5144:5656]], axis=1)
    small = jnp.concatenate([w[:, 3584:3592], w[:, 5128:5144],
                             jnp.zeros((d, SMALL_W - 2 * ML_HEADS - GLA_RANK), w.dtype)], axis=1)
    return main.astype(BF16), small.astype(BF16)


def _pad_row(vals, offset):
    row = jnp.zeros((1, SMALL_W), F32)
    return lax.dynamic_update_slice(row, vals.reshape(1, -1).astype(F32), (0, offset))


ACT_DTYPE = BF16


def kernel(x, c, ada_w, ada_b, norm_g, w_in, rel_bias, diff_lambda, ml_conv, ml_gate_b,
           gla_wa2, gla_ba, s5_a_re, s5_a_im, s5_log_dt, s5_b_re, s5_b_im, s5_c_re, s5_c_im,
           s5_d, s5_glu_w, s5_glu_b, w_branch, w_gate, b_gate, w_out, ffn_w_in, ffn_w_out):
    batch, seq, d = x.shape
    depth = ada_w.shape[0]
    t = batch * seq

    c_pad = jnp.concatenate([c, jnp.zeros((SUBLANES - batch, d), c.dtype)], axis=0)
    mod = _adaln(c_pad, ada_w, ada_b)[:, :batch]
    mod = mod.reshape(depth, batch, N_MOD, d)
    mod = jnp.concatenate([mod, jnp.zeros((depth, batch, SUBLANES - N_MOD, d), F32)], axis=2)

    bias_tiles = _bias_tiles(rel_bias, min(TQ_ATT, seq))

    x2 = x.reshape(t, d)
    for l in range(depth):
        w_main, w_small = _split_w_in(w_in[l])
        pm, ps = _proj(x2, mod[l], norm_g[l, 0:1], w_main, w_small, seq, ACT_DTYPE)

        lam_init = 0.8 - 0.6 * math.exp(-0.3 * l)
        lp = diff_lambda[l].astype(F32)
        lam = (jnp.exp(jnp.sum(lp[0] * lp[1])) - jnp.exp(jnp.sum(lp[2] * lp[3])) + lam_init)
        o_a = _attention(lam.reshape(1), pm, bias_tiles, batch, seq, lam_init, BF16)

        gate_row = (_pad_row(ml_gate_b[l, 0], SM_ML_I) + _pad_row(ml_gate_b[l, 1], SM_ML_F))
        o_b = _mlstm(pm, ps, ml_conv[l].astype(F32), gate_row, batch, seq, BF16)

        wa_pad = jnp.zeros((SMALL_W, GLA_HEADS * GLA_DK), F32)
        wa_pad = lax.dynamic_update_slice(wa_pad, gla_wa2[l].astype(F32), (SM_GL_A, 0))
        o_c = _gla(pm, ps, wa_pad, gla_ba[l].reshape(1, -1).astype(F32), batch, seq, BF16)

        packed = _s5_params(s5_a_re[l], s5_a_im[l], s5_log_dt[l], s5_b_re[l], s5_b_im[l],
                            s5_c_re[l], s5_c_im[l])
        o_d = _s5(pm, packed, s5_d[l].reshape(1, -1).astype(F32), s5_glu_w[l].astype(BF16),
                  s5_glu_b[l].reshape(1, -1).astype(F32), batch, seq, BF16)

        x2 = _merge(x2, mod[l], norm_g[l, 0:2], o_a, o_b, o_c, o_d,
                    w_gate[l].astype(BF16), b_gate[l].reshape(N_BRANCH, 1, d).astype(F32),
                    w_branch[l].astype(BF16), w_out[l].astype(BF16), seq)
        x2 = _ffn(x2, mod[l], norm_g[l, 2:4], ffn_w_in[l].astype(BF16),
                  ffn_w_out[l].astype(BF16), seq)
    return x2.reshape(batch, seq, d)
```

```python
import functools
import math

import numpy as np
import jax
import jax.numpy as jnp
from jax import lax
from jax.experimental import pallas as pl
from jax.experimental.pallas import tpu as pltpu

F32 = jnp.float32
BF16 = jnp.bfloat16
HIGHEST = lax.Precision.HIGHEST

D_MODEL = 2048
DEPTH = 4
EPS = 1e-6
N_MOD = 6
N_BRANCH = 4
BRANCH_WIDTH = 512
DA_HEADS = 4
DA_QK_DIM = 64
DA_V_DIM = 128
N_BUCKETS = 32
MAX_DISTANCE = 128
ML_HEADS = 4
ML_DIM = 128
ML_CONV = 4
GLA_HEADS = 4
GLA_DK = 64
GLA_DV = 128
GLA_RANK = 16
GLA_TAU = 16.0
S5_CH = 16
S5_GROUPS = BRANCH_WIDTH // S5_CH
S5_STATE = 64
FFN_HIDDEN = -(-(8 * D_MODEL) // (3 * 256)) * 256

LANES = 128
SUBLANES = 8
BF16_ROWS = 16
VMEM_LIMIT = 56 * 1024 * 1024

MAIN_W = 5632
SMALL_W = LANES
OFF_DA_Q, OFF_DA_K, OFF_DA_V = 0, 512, 1024
OFF_ML_Q, OFF_ML_K, OFF_ML_V, OFF_ML_O = 1536, 2048, 2560, 3072
OFF_GL_Q, OFF_GL_K, OFF_GL_V, OFF_GL_R = 3584, 3840, 4096, 4608
OFF_S5_U = 5120
SM_ML_I, SM_ML_F, SM_GL_A = 0, 4, 8

NEG = -1e30
LOG2E = math.log2(math.e)

TM_PROJ, TN_PROJ = 1024, 512
TQ_ATT = 512
L_MLSTM = 256
L_GLA, C_GLA = 256, 16
TM_S5 = 256
TM_MERGE, TN_MERGE = 512, 256
TM_FFN, TH_FFN = 512, 512


def _cparams(sem):
    return pltpu.CompilerParams(dimension_semantics=sem, vmem_limit_bytes=VMEM_LIMIT)


def _rms(x):
    return x * lax.rsqrt(jnp.mean(x * x, axis=-1, keepdims=True) + EPS)


def _sigmoid(x):
    return 1.0 / (1.0 + jnp.exp(-x))


def _silu(x):
    return x * _sigmoid(x)


def _log_sigmoid(x):
    return jnp.minimum(x, 0.0) - jnp.log1p(jnp.exp(-jnp.abs(x)))


def _dot(a, b):
    return jnp.dot(a, b, preferred_element_type=F32)


def _dot_nt(a, b):
    return lax.dot_general(a, b, (((1,), (1,)), ((), ())), preferred_element_type=F32)


def _dot_tn(a, b):
    return lax.dot_general(a, b, (((0,), (0,)), ((), ())), preferred_element_type=F32)


def _dot_exact(a, b):
    return jnp.dot(a, b, preferred_element_type=F32, precision=HIGHEST)


def _adaln_kernel(c_ref, w_ref, b_ref, o_ref):
    c = c_ref[...]
    o_ref[...] = _dot_exact(_silu(c), w_ref[...]) + b_ref[...]


def _adaln(c_pad, ada_w, ada_b):
    depth, d, n = ada_w.shape
    rows = c_pad.shape[0]
    tn = 1024
    return pl.pallas_call(
        _adaln_kernel,
        grid=(depth, n // tn),
        in_specs=[
            pl.BlockSpec((rows, d), lambda l, j: (0, 0)),
            pl.BlockSpec((None, d, tn), lambda l, j: (l, 0, j)),
            pl.BlockSpec((None, 1, tn), lambda l, j: (l, 0, j)),
        ],
        out_specs=pl.BlockSpec((None, rows, tn), lambda l, j: (l, 0, j)),
        out_shape=jax.ShapeDtypeStruct((depth, rows, n), F32),
        compiler_params=_cparams(("parallel", "parallel")),
        name="adaln",
    )(c_pad, ada_w, ada_b.reshape(depth, 1, n))


ROW_CHUNK = 32


def _for_row_chunks(n_rows, fn):
    def body(c, carry):
        fn(pl.ds(pl.multiple_of(c * ROW_CHUNK, ROW_CHUNK), ROW_CHUNK))
        return carry

    lax.fori_loop(0, n_rows // ROW_CHUNK, body, 0, unroll=4)


def _modulated_norm_into(h_scr, x_ref, gain, shift, scale):
    gs = gain * (1.0 + scale)

    def chunk(rows):
        h_scr[rows, :] = (_rms(x_ref[rows, :]) * gs + shift).astype(h_scr.dtype)

    _for_row_chunks(h_scr.shape[0], chunk)


def _gated_residual_into(out_ref, x_ref, acc_scr, gain, gate):
    gg = gain * gate

    def chunk(rows):
        out_ref[rows, :] = x_ref[rows, :] + _rms(acc_scr[rows, :]) * gg

    _for_row_chunks(out_ref.shape[0], chunk)


def _proj_kernel(x_ref, mod_ref, g_ref, w_ref, ws_ref, o_ref, os_ref, h_scr):
    j = pl.program_id(1)

    @pl.when(j == 0)
    def _():
        gs = g_ref[...] * (1.0 + mod_ref[1:2, :])
        hb = (_rms(x_ref[...]) * gs + mod_ref[0:1, :]).astype(BF16)
        h_scr[...] = hb
        os_ref[...] = _dot(hb, ws_ref[...])

    o_ref[...] = _dot(h_scr[...], w_ref[...]).astype(o_ref.dtype)


def _proj(x2, mod_l, gain, w_main, w_small, seq, out_dtype):
    t, d = x2.shape
    tm, tn = min(TM_PROJ, seq), TN_PROJ
    tiles_per_seq = seq // tm
    return pl.pallas_call(
        _proj_kernel,
        grid=(t // tm, MAIN_W // tn),
        in_specs=[
            pl.BlockSpec((tm, d), lambda i, j: (i, 0)),
            pl.BlockSpec((None, SUBLANES, d), lambda i, j: (i // tiles_per_seq, 0, 0)),
            pl.BlockSpec((1, d), lambda i, j: (0, 0)),
            pl.BlockSpec((d, tn), lambda i, j: (0, j)),
            pl.BlockSpec((d, SMALL_W), lambda i, j: (0, 0)),
        ],
        out_specs=[
            pl.BlockSpec((tm, tn), lambda i, j: (i, j)),
            pl.BlockSpec((tm, SMALL_W), lambda i, j: (i, 0)),
        ],
        out_shape=[
            jax.ShapeDtypeStruct((t, MAIN_W), out_dtype),
            jax.ShapeDtypeStruct((t, SMALL_W), F32),
        ],
        scratch_shapes=[pltpu.VMEM((tm, d), BF16)],
        compiler_params=_cparams(("parallel", "arbitrary")),
        name="proj",
    )(x2, mod_l, gain, w_main, w_small)


def _attn_kernel(lam_ref, q_ref, k_ref, v_ref, bias_ref, o_ref, m_scr, l_scr, acc_scr,
                 s0_scr, s1_scr, *, tq, out_scale):
    i = pl.program_id(2)
    lam = lam_ref[0]
    q = q_ref[...].astype(F32) * (DA_QK_DIM ** -0.5 * LOG2E)
    lane = lax.broadcasted_iota(jnp.int32, q.shape, 1)
    qa = jnp.where(lane < DA_QK_DIM, q, 0.0).astype(BF16)
    qb = jnp.where(lane >= DA_QK_DIM, q, 0.0).astype(BF16)
    q2 = jnp.concatenate([qa, qb], axis=0)

    m_scr[...] = jnp.full_like(m_scr, NEG)
    l_scr[...] = jnp.zeros_like(l_scr)
    acc_scr[...] = jnp.zeros_like(acc_scr)

    def scores(j):
        r0 = pl.multiple_of(j * tq, tq)
        return _dot_nt(k_ref[pl.ds(r0, tq), :].astype(BF16), q2)

    def accumulate(j, s):
        r0 = pl.multiple_of(j * tq, tq)
        vt = v_ref[pl.ds(r0, tq), :].astype(BF16)
        m_old = m_scr[...]
        m_new = jnp.maximum(m_old, jnp.max(s, axis=0, keepdims=True))
        p = jnp.exp2(s - m_new)
        alpha = jnp.exp2(m_old - m_new)
        l_scr[...] = alpha * l_scr[...] + jnp.sum(p, axis=0, keepdims=True)
        acc_scr[...] = alpha * acc_scr[...] + _dot_tn(vt, p.astype(BF16))
        m_scr[...] = m_new

    def biased_scores(j, bias):
        return scores(j) + jnp.concatenate([bias, bias], axis=1)

    n_far = jnp.maximum(i - 1, 0)
    pairs = n_far // 2
    j_prev = jnp.maximum(i - 1, 0)
    s0_scr[...] = biased_scores(i, bias_ref[1])
    s1_scr[...] = biased_scores(j_prev, bias_ref[jnp.where(i >= 1, 0, 2)])
    accumulate(i, s0_scr[...])
    s0_scr[...] = scores(0)
    accumulate(j_prev, s1_scr[...])

    def far_pair(g, carry):
        s1_scr[...] = scores(2 * g + 1)
        accumulate(2 * g, s0_scr[...])
        s0_scr[...] = scores(jnp.minimum(2 * g + 2, n_far - 1))
        accumulate(2 * g + 1, s1_scr[...])
        return carry

    lax.fori_loop(0, pairs, far_pair, 0)

    @pl.when(n_far > 2 * pairs)
    def _():
        accumulate(n_far - 1, s0_scr[...])

    on = acc_scr[...] / l_scr[...]
    ot = on[:, 0:tq] - lam * on[:, tq:2 * tq]
    ot = ot * (lax.rsqrt(jnp.mean(ot * ot, axis=0, keepdims=True) + EPS) * out_scale)
    o_ref[...] = ot.T.astype(o_ref.dtype)


def _attention(lam, pm, bias_tiles, batch, seq, lam_init, out_dtype):
    tq = min(TQ_ATT, seq)
    nq = seq // tq
    kern = functools.partial(_attn_kernel, tq=tq, out_scale=1.0 - lam_init)
    scratch = [pltpu.VMEM((1, 2 * tq), F32), pltpu.VMEM((1, 2 * tq), F32),
               pltpu.VMEM((DA_V_DIM, 2 * tq), F32),
               pltpu.VMEM((tq, 2 * tq), F32), pltpu.VMEM((tq, 2 * tq), F32)]
    qb, kb, vb = OFF_DA_Q // LANES, OFF_DA_K // LANES, OFF_DA_V // LANES
    return pl.pallas_call(
        kern,
        grid=(batch, DA_HEADS, nq),
        in_specs=[
            pl.BlockSpec(memory_space=pltpu.SMEM),
            pl.BlockSpec((tq, LANES), lambda b, h, i: (b * nq + i, qb + h)),
            pl.BlockSpec((seq, LANES), lambda b, h, i: (b, kb + h)),
            pl.BlockSpec((seq, LANES), lambda b, h, i: (b, vb + h)),
            pl.BlockSpec((None, 3, tq, tq), lambda b, h, i: (h, 0, 0, 0)),
        ],
        out_specs=pl.BlockSpec((tq, LANES), lambda b, h, i: (b * nq + i, h)),
        out_shape=jax.ShapeDtypeStruct((batch * seq, DA_HEADS * DA_V_DIM), out_dtype),
        scratch_shapes=scratch,
        compiler_params=_cparams(("parallel", "parallel", "arbitrary")),
        name="diff_attn",
    )(lam, pm, pm, pm, bias_tiles)


def _t5_bucket_table(n_max):
    n = np.arange(n_max)
    exact = N_BUCKETS // 2
    nf = np.maximum(n, 1).astype(np.float64)
    large = exact + (np.log(nf / exact) / math.log(MAX_DISTANCE / exact)
                     * (N_BUCKETS - exact)).astype(np.int64)
    return np.where(n < exact, n, np.minimum(large, N_BUCKETS - 1)).astype(np.int32)


def _bias_tiles(rel_bias, tq):
    assert tq >= MAX_DISTANCE
    n = tq
    heads = rel_bias.shape[1]
    rb = rel_bias.astype(F32)
    near = (rb[_t5_bucket_table(MAX_DISTANCE)] - rb[N_BUCKETS - 1][None, :]) * LOG2E
    f = jnp.concatenate([near, jnp.zeros((2 * n - MAX_DISTANCE, heads), F32)], axis=0)

    def toeplitz(v):
        vp = jnp.concatenate([v, jnp.zeros((1, heads), F32)], axis=0)
        flat = jnp.tile(vp, (n, 1))[: n * (2 * n - 1)]
        return jnp.transpose(flat.reshape(n, 2 * n - 1, heads)[:, n - 1:, :], (2, 0, 1))

    prev = toeplitz(f[1:2 * n])
    diag = toeplitz(jnp.concatenate([jnp.full((n - 1, heads), NEG, F32), f[0:n]], axis=0))
    masked = jnp.full_like(prev, NEG)
    return jnp.stack([prev, diag, masked], axis=1)


def _mlstm_kernel(q_ref, k_ref, v_ref, og_ref, sm_ref, qh_ref, kh_ref, cw_ref, gb_ref,
                  tril_ref, out_ref, xq_scr, xk_scr, c_scr, n_scr, m_scr, *, L):
    ci = pl.program_id(1)

    @pl.when(ci == 0)
    def _():
        c_scr[...] = jnp.zeros_like(c_scr)
        n_scr[...] = jnp.zeros_like(n_scr)
        m_scr[...] = jnp.zeros_like(m_scr)

    halo = BF16_ROWS
    keep = (ci > 0).astype(F32)
    xq_scr[0:halo, :] = qh_ref[...].astype(F32) * keep
    xk_scr[0:halo, :] = kh_ref[...].astype(F32) * keep
    xq_scr[halo:halo + L, :] = q_ref[...].astype(F32)
    xk_scr[halo:halo + L, :] = k_ref[...].astype(F32)

    def conv_silu(scr, w):
        y = scr[halo:halo + L, :] * w[ML_CONV - 1:ML_CONV, :]
        for j in range(ML_CONV - 1):
            off = halo - (ML_CONV - 1) + j
            y = y + scr[off:off + L, :] * w[j:j + 1, :]
        return _silu(y)

    w_all = cw_ref[...]
    width = ML_HEADS * ML_DIM
    q_all = conv_silu(xq_scr, w_all[:, 0:width])
    k_all = conv_silu(xk_scr, w_all[:, width:2 * width]) * (ML_DIM ** -0.5)

    g = sm_ref[...] + gb_ref[...]
    lane = lax.broadcasted_iota(jnp.int32, g.shape, 1)
    is_f = (lane >= SM_ML_F) & (lane < SM_ML_F + ML_HEADS)
    g = jnp.where(is_f, _log_sigmoid(g), g)
    bcum = _dot_exact(tril_ref[...], g)
    g_t = g.T
    b_t = bcum.T
    row = lax.broadcasted_iota(jnp.int32, (L, L), 0)
    colj = lax.broadcasted_iota(jnp.int32, (L, L), 1)
    causal = colj <= row

    for h in range(ML_HEADS):
        sl = slice(h * ML_DIM, (h + 1) * ML_DIM)
        qh = q_all[:, sl]
        kh = k_all[:, sl]
        vh = v_ref[:, sl].astype(F32)
        qb, kb, vb = qh.astype(BF16), kh.astype(BF16), vh.astype(BF16)
        ig_col = g[:, SM_ML_I + h:SM_ML_I + h + 1]
        b_col = bcum[:, SM_ML_F + h:SM_ML_F + h + 1]
        ig_row = g_t[SM_ML_I + h:SM_ML_I + h + 1, :]
        b_row = b_t[SM_ML_F + h:SM_ML_F + h + 1, :]
        m_old = m_scr[h][:, 0:1]
        c_old = c_scr[h]
        n_old = n_scr[h]

        dm = jnp.where(causal, b_col - b_row + ig_row, NEG)
        inter = b_col + m_old
        m_t = jnp.maximum(inter, jnp.max(dm, axis=-1, keepdims=True))
        s = _dot_nt(qb, kb) * jnp.exp(dm - m_t)
        a = jnp.exp(inter - m_t)
        num = a * _dot(qb, c_old.astype(BF16)) + _dot(s.astype(BF16), vb)
        den = (a * jnp.sum(qh * n_old, axis=-1, keepdims=True)
               + jnp.sum(s, axis=-1, keepdims=True))
        hv = num / jnp.maximum(jnp.abs(den), jnp.exp(-m_t))
        out_ref[:, sl] = (_sigmoid(og_ref[:, sl].astype(F32)) * hv).astype(out_ref.dtype)

        m_new = m_t[L - 1:L, :]
        b_last = b_col[L - 1:L, :]
        a_state = jnp.exp(b_last + m_old - m_new)
        w_col = jnp.exp(b_last - b_col + ig_col - m_new)
        kw = kh * w_col
        c_scr[h] = a_state * c_old + _dot_tn(kw.astype(BF16), vb)
        n_scr[h] = a_state * n_old + jnp.sum(kw, axis=0, keepdims=True)
        m_scr[h] = jnp.broadcast_to(m_new, (1, LANES))


def _mlstm(pm, ps, conv_w, gate_row, batch, seq, out_dtype):
    L = min(L_MLSTM, seq)
    nc = seq // L
    width = ML_HEADS * ML_DIM
    qb, kb, vb, ob = (OFF_ML_Q // width, OFF_ML_K // width, OFF_ML_V // width, OFF_ML_O // width)
    lb = L // BF16_ROWS
    tril = jnp.asarray(np.tril(np.ones((L, L), np.float32)))

    def halo_map(colblk):
        return lambda b, c: (jnp.maximum(b * (seq // BF16_ROWS) + c * lb - 1, 0), colblk)

    return pl.pallas_call(
        functools.partial(_mlstm_kernel, L=L),
        grid=(batch, nc),
        in_specs=[
            pl.BlockSpec((L, width), lambda b, c: (b * nc + c, qb)),
            pl.BlockSpec((L, width), lambda b, c: (b * nc + c, kb)),
            pl.BlockSpec((L, width), lambda b, c: (b * nc + c, vb)),
            pl.BlockSpec((L, width), lambda b, c: (b * nc + c, ob)),
            pl.BlockSpec((L, SMALL_W), lambda b, c: (b * nc + c, 0)),
            pl.BlockSpec((BF16_ROWS, width), halo_map(qb)),
            pl.BlockSpec((BF16_ROWS, width), halo_map(kb)),
            pl.BlockSpec((ML_CONV, 2 * width), lambda b, c: (0, 0)),
            pl.BlockSpec((1, SMALL_W), lambda b, c: (0, 0)),
            pl.BlockSpec((L, L), lambda b, c: (0, 0)),
        ],
        out_specs=pl.BlockSpec((L, width), lambda b, c: (b * nc + c, 0)),
        out_shape=jax.ShapeDtypeStruct((batch * seq, width), out_dtype),
        scratch_shapes=[
            pltpu.VMEM((L + BF16_ROWS, width), F32),
            pltpu.VMEM((L + BF16_ROWS, width), F32),
            pltpu.VMEM((ML_HEADS, ML_DIM, ML_DIM), F32),
            pltpu.VMEM((ML_HEADS, 1, ML_DIM), F32),
            pltpu.VMEM((ML_HEADS, 1, LANES), F32),
        ],
        compiler_params=_cparams(("parallel", "arbitrary")),
        name="mlstm",
    )(pm, pm, pm, pm, ps, pm, pm, conv_w, gate_row, tril)


def _gla_kernel(q_ref, k_ref, v_ref, r_ref, sm_ref, wa_ref, ba_ref, tril_ref, mexp_ref,
                out_ref, bc_scr, a_scr, st_scr, *, L, c):
    ci = pl.program_id(1)

    @pl.when(ci == 0)
    def _():
        st_scr[...] = jnp.zeros_like(st_scr)

    la = _log_sigmoid(_dot_exact(sm_ref[...], wa_ref[...]) + ba_ref[...]) * (1.0 / GLA_TAU)
    bc_scr[...] = _dot_exact(tril_ref[...], la)
    kw = GLA_HEADS * GLA_DK
    rowc = lax.broadcasted_iota(jnp.int32, (c, kw), 0)
    lane_head = lax.broadcasted_iota(jnp.int32, (c, kw), 1) // GLA_DK
    head_masks = [lane_head == h for h in range(GLA_HEADS)]

    def stack_heads(x):
        return jnp.concatenate([jnp.where(mk, x, 0.0) for mk in head_masks], axis=0)

    def sub(i, carry):
        r0 = pl.multiple_of(i * c, c)
        qs = q_ref[pl.ds(r0, c), :].astype(F32) * (GLA_DK ** -0.5)
        ks = k_ref[pl.ds(r0, c), :].astype(F32)
        vs = v_ref[pl.ds(r0, c), :].astype(BF16).astype(F32)
        bcs = bc_scr[pl.ds(r0, c), :]
        e_end = bcs[c - 1:c, :]
        st = st_scr[...]
        o_stack = _dot_nt(stack_heads(qs * jnp.exp(bcs)).astype(BF16), st.astype(BF16))
        o = jnp.concatenate([o_stack[h * c:(h + 1) * c] for h in range(GLA_HEADS)], axis=1)

        for t in range(c):
            dec = jnp.exp(jnp.minimum(bcs[t:t + 1, :] - bcs, 0.0))
            a_t = jnp.where(rowc <= t, qs[t:t + 1, :] * ks * dec, 0.0)
            a_scr[t * c:(t + 1) * c, :] = a_t.astype(BF16)
        p = _dot(a_scr[...], mexp_ref[...])
        o = o + jnp.sum(p.reshape(c, c, GLA_HEADS * GLA_DV) * vs[None, :, :], axis=1)

        outs = [_rms(o[:, h * GLA_DV:(h + 1) * GLA_DV]) for h in range(GLA_HEADS)]
        on = jnp.concatenate(outs, axis=1)
        out_ref[pl.ds(r0, c), :] = (on * _silu(r_ref[pl.ds(r0, c), :].astype(F32))).astype(out_ref.dtype)

        khat = stack_heads(ks * jnp.exp(e_end - bcs)).astype(BF16)
        v_stack = jnp.concatenate([vs[:, h * GLA_DV:(h + 1) * GLA_DV] for h in range(GLA_HEADS)],
                                  axis=0).astype(BF16)
        st_scr[...] = st * jnp.exp(e_end) + _dot_tn(v_stack, khat)
        return carry

    lax.fori_loop(0, L // c, sub, 0, unroll=4)


def _gla(pm, ps, wa_pad, ba_row, batch, seq, out_dtype):
    L = min(L_GLA, seq)
    c = C_GLA
    nc = seq // L
    kw, vw = GLA_HEADS * GLA_DK, GLA_HEADS * GLA_DV
    qb, kb, vb, rb = OFF_GL_Q // kw, OFF_GL_K // kw, OFF_GL_V // vw, OFF_GL_R // vw
    idx = np.arange(L)
    tril = ((idx[:, None] >= idx[None, :]) & (idx[:, None] // c == idx[None, :] // c))
    tril = jnp.asarray(tril.astype(np.float32))
    mexp = np.zeros((kw, vw), np.float32)
    for h in range(GLA_HEADS):
        mexp[h * GLA_DK:(h + 1) * GLA_DK, h * GLA_DV:(h + 1) * GLA_DV] = 1.0
    mexp = jnp.asarray(mexp, dtype=BF16)
    return pl.pallas_call(
        functools.partial(_gla_kernel, L=L, c=c),
        grid=(batch, nc),
        in_specs=[
            pl.BlockSpec((L, kw), lambda b, i: (b * nc + i, qb)),
            pl.BlockSpec((L, kw), lambda b, i: (b * nc + i, kb)),
            pl.BlockSpec((L, vw), lambda b, i: (b * nc + i, vb)),
            pl.BlockSpec((L, vw), lambda b, i: (b * nc + i, rb)),
            pl.BlockSpec((L, SMALL_W), lambda b, i: (b * nc + i, 0)),
            pl.BlockSpec((SMALL_W, kw), lambda b, i: (0, 0)),
            pl.BlockSpec((1, kw), lambda b, i: (0, 0)),
            pl.BlockSpec((L, L), lambda b, i: (0, 0)),
            pl.BlockSpec((kw, vw), lambda b, i: (0, 0)),
        ],
        out_specs=pl.BlockSpec((L, vw), lambda b, i: (b * nc + i, 0)),
        out_shape=jax.ShapeDtypeStruct((batch * seq, vw), out_dtype),
        scratch_shapes=[
            pltpu.VMEM((L, kw), F32),
            pltpu.VMEM((c * c, kw), BF16),
            pltpu.VMEM((GLA_DV, kw), F32),
        ],
        compiler_params=_cparams(("parallel", "arbitrary")),
        name="gla",
    )(pm, pm, pm, pm, ps, wa_pad, ba_row, tril, mexp)


S5_NSTATE = S5_GROUPS * S5_STATE
S5_BLK = 4
S5_BLK_STATE = S5_NSTATE // S5_BLK


def _gelu_tanh(x):
    return 0.5 * x * (1.0 + jnp.tanh(math.sqrt(2.0 / math.pi) * (x + 0.044715 * (x * x * x))))


def _s5_kernel(u_ref, bre_ref, bim_ref, cre_ref, cim_ref, as_ref, pw_ref, d_ref, gw_ref,
               gb_ref, out_ref, xr_scr, xi_scr, cr_scr, ci_scr, *, rows, batch):
    ti = pl.program_id(0)

    @pl.when(ti == 0)
    def _():
        cr_scr[...] = jnp.zeros_like(cr_scr)
        ci_scr[...] = jnp.zeros_like(ci_scr)

    u = u_ref[...].astype(F32)
    ub = u.astype(BF16)
    nb = S5_BLK_STATE
    for q in range(S5_BLK):
        uq = ub[:, q * LANES:(q + 1) * LANES]
        xr_scr[:, q * nb:(q + 1) * nb] = _dot(uq, bre_ref[q])
        xi_scr[:, q * nb:(q + 1) * nb] = _dot(uq, bim_ref[q])

    rowi = lax.broadcasted_iota(jnp.int32, (SUBLANES, nb), 0)
    shifts = _s5_row_shifts(batch)
    for cc in range(S5_BLK):
        cols = slice(cc * nb, (cc + 1) * nb)

        def body(g, carry, cols=cols):
            cr, ci = carry
            for r in shifts:
                cr = jnp.where(rowi < SUBLANES - r, pltpu.roll(cr, SUBLANES - r, 0), cr)
                ci = jnp.where(rowi < SUBLANES - r, pltpu.roll(ci, SUBLANES - r, 0), ci)
            r0 = pl.multiple_of(g * SUBLANES, SUBLANES)
            zr = xr_scr[pl.ds(r0, SUBLANES), cols]
            zi = xi_scr[pl.ds(r0, SUBLANES), cols]
            for si, r in enumerate(shifts):
                sr = pltpu.roll(zr, r, 0)
                sim = pltpu.roll(zi, r, 0)
                ar = as_ref[0, si, :, cols]
                ai = as_ref[1, si, :, cols]
                zr, zi = zr + ar * sr - ai * sim, zi + ar * sim + ai * sr
            p_r = pw_ref[0, :, cols]
            p_i = pw_ref[1, :, cols]
            xr = zr + p_r * cr - p_i * ci
            xi = zi + p_r * ci + p_i * cr
            xr_scr[pl.ds(r0, SUBLANES), cols] = xr
            xi_scr[pl.ds(r0, SUBLANES), cols] = xi
            return xr, xi

        cr, ci = lax.fori_loop(0, rows // SUBLANES, body, (cr_scr[:, cols], ci_scr[:, cols]))
        cr_scr[:, cols] = cr
        ci_scr[:, cols] = ci

    ys = []
    for q in range(S5_BLK):
        xr = xr_scr[:, q * nb:(q + 1) * nb].astype(BF16)
        xi = xi_scr[:, q * nb:(q + 1) * nb].astype(BF16)
        ys.append(_dot(xr, cre_ref[q]) + _dot(xi, cim_ref[q]))
    y = jnp.concatenate(ys, axis=1) + d_ref[...] * u
    z = _gelu_tanh(y)
    gate = _sigmoid(_dot(z.astype(BF16), gw_ref[...]) + gb_ref[...])
    out_ref[...] = (z * gate).astype(out_ref.dtype)


def _s5_row_shifts(batch):
    assert SUBLANES % batch == 0
    return tuple(batch * (1 << k) for k in range(8) if batch * (1 << k) < SUBLANES)


def _s5_params(a_re, a_im, log_dt, b_re, b_im, c_re, c_im, batch):
    a_re, a_im = a_re.astype(F32), a_im.astype(F32)
    dt = jnp.exp(log_dt.astype(F32))[:, None]
    mag = jnp.exp(dt * a_re)
    ab_re, ab_im = mag * jnp.cos(dt * a_im), mag * jnp.sin(dt * a_im)
    nr, ni = ab_re - 1.0, ab_im
    den = a_re * a_re + a_im * a_im
    f_re = (nr * a_re + ni * a_im) / den
    f_im = (ni * a_re - nr * a_im) / den
    b_re, b_im = b_re.astype(F32), b_im.astype(F32)
    bb_re = f_re[..., None] * b_re - f_im[..., None] * b_im
    bb_im = f_re[..., None] * b_im + f_im[..., None] * b_re

    def apow(k):
        mk = jnp.exp(k * dt * a_re)
        return (mk * jnp.cos(k * dt * a_im)).reshape(-1), (mk * jnp.sin(k * dt * a_im)).reshape(-1)

    rows = np.arange(SUBLANES)[:, None]

    def shift_table(r, part):
        return jnp.where(jnp.asarray(rows >= r), apow(float(r // batch))[part][None, :], 0.0)

    as_arr = jnp.stack([jnp.stack([shift_table(r, part) for r in _s5_row_shifts(batch)])
                        for part in (0, 1)])
    pws = [apow(float(k // batch + 1)) for k in range(SUBLANES)]
    pw_arr = jnp.stack([jnp.stack([p[0] for p in pws]), jnp.stack([p[1] for p in pws])])

    gpb = S5_GROUPS // S5_BLK
    eye = jnp.eye(gpb, dtype=F32)

    def pack_b(bb):
        bb = bb.reshape(S5_BLK, gpb, S5_STATE, S5_CH)
        return jnp.einsum('qgpc,gh->qgchp', bb, eye).reshape(S5_BLK, gpb * S5_CH, gpb * S5_STATE)

    def pack_c(cc):
        cc = cc.reshape(S5_BLK, gpb, S5_CH, S5_STATE)
        return jnp.einsum('qgcp,gh->qgphc', cc, eye).reshape(S5_BLK, gpb * S5_STATE, gpb * S5_CH)

    return (pack_b(bb_re).astype(BF16), pack_b(bb_im).astype(BF16),
            pack_c(c_re.astype(F32)).astype(BF16), pack_c(-c_im.astype(F32)).astype(BF16),
            as_arr, pw_arr)


def _s5(pm, packed, d_row, glu_w, glu_b, batch, seq, out_dtype):
    rows = min(TM_S5, seq) * batch
    bre, bim, cre, cim, as_arr, pw_arr = packed
    w = BRANCH_WIDTH
    u_tb = pm[:, OFF_S5_U:OFF_S5_U + w].reshape(batch, seq, w).transpose(1, 0, 2).reshape(seq * batch, w)
    full = lambda *shape: pl.BlockSpec(shape, lambda i: (0,) * len(shape))
    out_tb = pl.pallas_call(
        functools.partial(_s5_kernel, rows=rows, batch=batch),
        grid=(seq * batch // rows,),
        in_specs=[
            pl.BlockSpec((rows, w), lambda i: (i, 0)),
            full(S5_BLK, LANES, S5_BLK_STATE), full(S5_BLK, LANES, S5_BLK_STATE),
            full(S5_BLK, S5_BLK_STATE, LANES), full(S5_BLK, S5_BLK_STATE, LANES),
            full(*as_arr.shape), full(2, SUBLANES, S5_NSTATE),
            full(1, w), full(w, w), full(1, w),
        ],
        out_specs=pl.BlockSpec((rows, w), lambda i: (i, 0)),
        out_shape=jax.ShapeDtypeStruct((seq * batch, w), out_dtype),
        scratch_shapes=[
            pltpu.VMEM((rows, S5_NSTATE), F32),
            pltpu.VMEM((rows, S5_NSTATE), F32),
            pltpu.VMEM((SUBLANES, S5_NSTATE), F32),
            pltpu.VMEM((SUBLANES, S5_NSTATE), F32),
        ],
        compiler_params=_cparams(("arbitrary",)),
        name="s5",
    )(u_tb, bre, bim, cre, cim, as_arr, pw_arr, d_row, glu_w, glu_b)
    return out_tb.reshape(seq, batch, w).transpose(1, 0, 2).reshape(batch * seq, w)


def _merge_kernel(x_ref, mod_ref, g_ref, oa_ref, ob_ref, oc_ref, od_ref, wg_ref, bg_ref,
                  wb_ref, wo_ref, out_ref, h_scr, acc_scr):
    n = pl.program_id(1)

    @pl.when(n == 0)
    def _():
        _modulated_norm_into(h_scr, x_ref, g_ref[0:1, :], mod_ref[0:1, :], mod_ref[1:2, :])
        acc_scr[...] = jnp.zeros_like(acc_scr)

    hb = h_scr[...]
    merged = None
    for i, o_ref in enumerate((oa_ref, ob_ref, oc_ref, od_ref)):
        gate = _sigmoid(_dot(hb, wg_ref[i]) + bg_ref[i])
        term = gate * _dot(o_ref[...].astype(BF16), wb_ref[i])
        merged = term if merged is None else merged + term
    acc_scr[...] += _dot(merged.astype(BF16), wo_ref[...])

    @pl.when(n == pl.num_programs(1) - 1)
    def _():
        _gated_residual_into(out_ref, x_ref, acc_scr, g_ref[1:2, :], mod_ref[2:3, :])


def _merge(x2, mod_l, gains, oa, ob, oc, od, w_gate, b_gate, w_branch, w_out, seq):
    t, d = x2.shape
    tm, tn = min(TM_MERGE, seq), TN_MERGE
    tiles_per_seq = seq // tm
    w = BRANCH_WIDTH
    br_spec = pl.BlockSpec((tm, w), lambda i, n: (i, 0))
    return pl.pallas_call(
        _merge_kernel,
        grid=(t // tm, d // tn),
        in_specs=[
            pl.BlockSpec((tm, d), lambda i, n: (i, 0)),
            pl.BlockSpec((None, SUBLANES, d), lambda i, n: (i // tiles_per_seq, 0, 0)),
            pl.BlockSpec((2, d), lambda i, n: (0, 0)),
            br_spec, br_spec, br_spec, br_spec,
            pl.BlockSpec((N_BRANCH, d, tn), lambda i, n: (0, 0, n)),
            pl.BlockSpec((N_BRANCH, 1, tn), lambda i, n: (0, 0, n)),
            pl.BlockSpec((N_BRANCH, w, tn), lambda i, n: (0, 0, n)),
            pl.BlockSpec((tn, d), lambda i, n: (n, 0)),
        ],
        out_specs=pl.BlockSpec((tm, d), lambda i, n: (i, 0)),
        out_shape=jax.ShapeDtypeStruct((t, d), F32),
        scratch_shapes=[pltpu.VMEM((tm, d), BF16), pltpu.VMEM((tm, d), F32)],
        compiler_params=_cparams(("parallel", "arbitrary")),
        name="merge",
    )(x2, mod_l, gains, oa, ob, oc, od, w_gate, b_gate, w_branch, w_out)


def _ffn_kernel(x_ref, mod_ref, g_ref, wa_ref, wg_ref, wo_ref, out_ref, h_scr, acc_scr):
    j = pl.program_id(1)

    @pl.when(j == 0)
    def _():
        _modulated_norm_into(h_scr, x_ref, g_ref[0:1, :], mod_ref[3:4, :], mod_ref[4:5, :])
        acc_scr[...] = jnp.zeros_like(acc_scr)

    hb = h_scr[...]
    a = _dot(hb, wa_ref[...])
    g = _dot(hb, wg_ref[...])
    acc_scr[...] += _dot((_silu(a) * g).astype(BF16), wo_ref[...])

    @pl.when(j == pl.num_programs(1) - 1)
    def _():
        _gated_residual_into(out_ref, x_ref, acc_scr, g_ref[1:2, :], mod_ref[5:6, :])


def _ffn(x2, mod_l, gains, w_in, w_out, seq):
    t, d = x2.shape
    tm, th = min(TM_FFN, seq), TH_FFN
    tiles_per_seq = seq // tm
    nh = FFN_HIDDEN // th
    return pl.pallas_call(
        _ffn_kernel,
        grid=(t // tm, nh),
        in_specs=[
            pl.BlockSpec((tm, d), lambda i, j: (i, 0)),
            pl.BlockSpec((None, SUBLANES, d), lambda i, j: (i // tiles_per_seq, 0, 0)),
            pl.BlockSpec((2, d), lambda i, j: (0, 0)),
            pl.BlockSpec((d, th), lambda i, j: (0, j)),
            pl.BlockSpec((d, th), lambda i, j: (0, j + nh)),
            pl.BlockSpec((th, d), lambda i, j: (j, 0)),
        ],
        out_specs=pl.BlockSpec((tm, d), lambda i, j: (i, 0)),
        out_shape=jax.ShapeDtypeStruct((t, d), F32),
        scratch_shapes=[pltpu.VMEM((tm, d), BF16), pltpu.VMEM((tm, d), F32)],
        compiler_params=_cparams(("parallel", "arbitrary")),
        name="ffn",
    )(x2, mod_l, gains, w_in, w_in, w_out)


def _split_w_in(w):
    d = w.shape[0]
    main = jnp.concatenate([w[:, :3584], w[:, 3592:5128], w[:, 5144:5656]], axis=1)
    small = jnp.concatenate([w[:, 3584:3592], w[:, 5128:5144],
                             jnp.zeros((d, SMALL_W - 2 * ML_HEADS - GLA_RANK), w.dtype)], axis=1)
    return main.astype(BF16), small.astype(BF16)


def _pad_row(vals, offset):
    row = jnp.zeros((1, SMALL_W), F32)
    return lax.dynamic_update_slice(row, vals.reshape(1, -1).astype(F32), (0, offset))


ACT_DTYPE = BF16


def kernel(x, c, ada_w, ada_b, norm_g, w_in, rel_bias, diff_lambda, ml_conv, ml_gate_b,
           gla_wa2, gla_ba, s5_a_re, s5_a_im, s5_log_dt, s5_b_re, s5_b_im, s5_c_re, s5_c_im,
           s5_d, s5_glu_w, s5_glu_b, w_branch, w_gate, b_gate, w_out, ffn_w_in, ffn_w_out):
    batch, seq, d = x.shape
    depth = ada_w.shape[0]
    t = batch * seq

    c_pad = jnp.concatenate([c, jnp.zeros((SUBLANES - batch, d), c.dtype)], axis=0)
    mod = _adaln(c_pad, ada_w, ada_b)[:, :batch]
    mod = mod.reshape(depth, batch, N_MOD, d)
    mod = jnp.concatenate([mod, jnp.zeros((depth, batch, SUBLANES - N_MOD, d), F32)], axis=2)

    bias_tiles = _bias_tiles(rel_bias, min(TQ_ATT, seq))

    x2 = x.reshape(t, d)
    for l in range(depth):
        w_main, w_small = _split_w_in(w_in[l])
        pm, ps = _proj(x2, mod[l], norm_g[l, 0:1], w_main, w_small, seq, ACT_DTYPE)

        lam_init = 0.8 - 0.6 * math.exp(-0.3 * l)
        lp = diff_lambda[l].astype(F32)
        lam = (jnp.exp(jnp.sum(lp[0] * lp[1])) - jnp.exp(jnp.sum(lp[2] * lp[3])) + lam_init)
        o_a = _attention(lam.reshape(1), pm, bias_tiles, batch, seq, lam_init, BF16)

        gate_row = (_pad_row(ml_gate_b[l, 0], SM_ML_I) + _pad_row(ml_gate_b[l, 1], SM_ML_F))
        o_b = _mlstm(pm, ps, ml_conv[l].astype(F32), gate_row, batch, seq, BF16)

        wa_pad = jnp.zeros((SMALL_W, GLA_HEADS * GLA_DK), F32)
        wa_pad = lax.dynamic_update_slice(wa_pad, gla_wa2[l].astype(F32), (SM_GL_A, 0))
        o_c = _gla(pm, ps, wa_pad, gla_ba[l].reshape(1, -1).astype(F32), batch, seq, BF16)

        packed = _s5_params(s5_a_re[l], s5_a_im[l], s5_log_dt[l], s5_b_re[l], s5_b_im[l],
                            s5_c_re[l], s5_c_im[l], batch)
        o_d = _s5(pm, packed, s5_d[l].reshape(1, -1).astype(F32), s5_glu_w[l].astype(BF16),
                  s5_glu_b[l].reshape(1, -1).astype(F32), batch, seq, BF16)

        x2 = _merge(x2, mod[l], norm_g[l, 0:2], o_a, o_b, o_c, o_d,
                    w_gate[l].astype(BF16), b_gate[l].reshape(N_BRANCH, 1, d).astype(F32),
                    w_branch[l].astype(BF16), w_out[l].astype(BF16), seq)
        x2 = _ffn(x2, mod[l], norm_g[l, 2:4], ffn_w_in[l].astype(BF16),
                  ffn_w_out[l].astype(BF16), seq)
    return x2.reshape(batch, seq, d)
```

```python
import functools
import math

import numpy as np
import jax
import jax.numpy as jnp
from jax import lax
from jax.experimental import pallas as pl
from jax.experimental.pallas import tpu as pltpu

F32 = jnp.float32
BF16 = jnp.bfloat16
HIGHEST = lax.Precision.HIGHEST

D_MODEL = 2048
DEPTH = 4
EPS = 1e-6
N_MOD = 6
N_BRANCH = 4
BRANCH_WIDTH = 512
DA_HEADS = 4
DA_QK_DIM = 64
DA_V_DIM = 128
N_BUCKETS = 32
MAX_DISTANCE = 128
ML_HEADS = 4
ML_DIM = 128
ML_CONV = 4
GLA_HEADS = 4
GLA_DK = 64
GLA_DV = 128
GLA_RANK = 16
GLA_TAU = 16.0
S5_CH = 16
S5_GROUPS = BRANCH_WIDTH // S5_CH
S5_STATE = 64
FFN_HIDDEN = -(-(8 * D_MODEL) // (3 * 256)) * 256

LANES = 128
SUBLANES = 8
BF16_ROWS = 16
VMEM_LIMIT = 56 * 1024 * 1024

MAIN_W = 5632
SMALL_W = LANES
OFF_DA_Q, OFF_DA_K, OFF_DA_V = 0, 512, 1024
OFF_ML_Q, OFF_ML_K, OFF_ML_V, OFF_ML_O = 1536, 2048, 2560, 3072
OFF_GL_Q, OFF_GL_K, OFF_GL_V, OFF_GL_R = 3584, 3840, 4096, 4608
OFF_S5_U = 5120
SM_ML_I, SM_ML_F, SM_GL_A = 0, 4, 8

NEG = -1e30
LOG2E = math.log2(math.e)

TM_PROJ, TN_PROJ = 1024, 512
TQ_ATT = 512
L_MLSTM = 256
L_GLA, C_GLA = 256, 16
TM_S5 = 256
TM_MERGE, TN_MERGE = 512, 256
TM_FFN, TH_FFN = 512, 512


def _cparams(sem):
    return pltpu.CompilerParams(dimension_semantics=sem, vmem_limit_bytes=VMEM_LIMIT)


def _rms(x):
    return x * lax.rsqrt(jnp.mean(x * x, axis=-1, keepdims=True) + EPS)


def _sigmoid(x):
    return 1.0 / (1.0 + jnp.exp(-x))


def _silu(x):
    return x * _sigmoid(x)


def _log_sigmoid(x):
    return jnp.minimum(x, 0.0) - jnp.log1p(jnp.exp(-jnp.abs(x)))


def _dot(a, b):
    return jnp.dot(a, b, preferred_element_type=F32)


def _dot_nt(a, b):
    return lax.dot_general(a, b, (((1,), (1,)), ((), ())), preferred_element_type=F32)


def _dot_tn(a, b):
    return lax.dot_general(a, b, (((0,), (0,)), ((), ())), preferred_element_type=F32)


def _dot_exact(a, b):
    return jnp.dot(a, b, preferred_element_type=F32, precision=HIGHEST)


def _adaln_kernel(c_ref, w_ref, b_ref, o_ref):
    c = c_ref[...]
    o_ref[...] = _dot_exact(_silu(c), w_ref[...]) + b_ref[...]


def _adaln(c_pad, ada_w, ada_b):
    depth, d, n = ada_w.shape
    rows = c_pad.shape[0]
    tn = 1024
    return pl.pallas_call(
        _adaln_kernel,
        grid=(depth, n // tn),
        in_specs=[
            pl.BlockSpec((rows, d), lambda l, j: (0, 0)),
            pl.BlockSpec((None, d, tn), lambda l, j: (l, 0, j)),
            pl.BlockSpec((None, 1, tn), lambda l, j: (l, 0, j)),
        ],
        out_specs=pl.BlockSpec((None, rows, tn), lambda l, j: (l, 0, j)),
        out_shape=jax.ShapeDtypeStruct((depth, rows, n), F32),
        compiler_params=_cparams(("parallel", "parallel")),
        name="adaln",
    )(c_pad, ada_w, ada_b.reshape(depth, 1, n))


ROW_CHUNK = 32


def _for_row_chunks(n_rows, fn):
    def body(c, carry):
        fn(pl.ds(pl.multiple_of(c * ROW_CHUNK, ROW_CHUNK), ROW_CHUNK))
        return carry

    lax.fori_loop(0, n_rows // ROW_CHUNK, body, 0, unroll=4)


def _modulated_norm_into(h_scr, x_ref, gain, shift, scale):
    gs = gain * (1.0 + scale)

    def chunk(rows):
        h_scr[rows, :] = (_rms(x_ref[rows, :]) * gs + shift).astype(h_scr.dtype)

    _for_row_chunks(h_scr.shape[0], chunk)


def _gated_residual_into(out_ref, x_ref, acc_scr, gain, gate):
    gg = gain * gate

    def chunk(rows):
        out_ref[rows, :] = x_ref[rows, :] + _rms(acc_scr[rows, :]) * gg

    _for_row_chunks(out_ref.shape[0], chunk)


def _proj_kernel(x_ref, mod_ref, g_ref, w_ref, ws_ref, o_ref, os_ref, ou_ref, h_scr, *, j_s5):
    j = pl.program_id(1)

    @pl.when(j == 0)
    def _():
        gs = g_ref[...] * (1.0 + mod_ref[1:2, :])
        hb = (_rms(x_ref[...]) * gs + mod_ref[0:1, :]).astype(BF16)
        h_scr[...] = hb
        os_ref[...] = _dot(hb, ws_ref[...])

    res = _dot(h_scr[...], w_ref[...]).astype(o_ref.dtype)
    o_ref[...] = res

    @pl.when(j == j_s5)
    def _():
        ou_ref[...] = res


def _proj(x2, mod_l, gain, w_main, w_small, seq, out_dtype):
    t, d = x2.shape
    tm, tn = min(TM_PROJ, seq), TN_PROJ
    tiles_per_seq = seq // tm
    assert tn == BRANCH_WIDTH and OFF_S5_U % tn == 0
    return pl.pallas_call(
        functools.partial(_proj_kernel, j_s5=OFF_S5_U // tn),
        grid=(t // tm, MAIN_W // tn),
        in_specs=[
            pl.BlockSpec((tm, d), lambda i, j: (i, 0)),
            pl.BlockSpec((None, SUBLANES, d), lambda i, j: (i // tiles_per_seq, 0, 0)),
            pl.BlockSpec((1, d), lambda i, j: (0, 0)),
            pl.BlockSpec((d, tn), lambda i, j: (0, j)),
            pl.BlockSpec((d, SMALL_W), lambda i, j: (0, 0)),
        ],
        out_specs=[
            pl.BlockSpec((tm, tn), lambda i, j: (i, j)),
            pl.BlockSpec((tm, SMALL_W), lambda i, j: (i, 0)),
            pl.BlockSpec((tm, tn), lambda i, j: (i % tiles_per_seq, i // tiles_per_seq)),
        ],
        out_shape=[
            jax.ShapeDtypeStruct((t, MAIN_W), out_dtype),
            jax.ShapeDtypeStruct((t, SMALL_W), F32),
            jax.ShapeDtypeStruct((seq, (t // seq) * tn), out_dtype),
        ],
        scratch_shapes=[pltpu.VMEM((tm, d), BF16)],
        compiler_params=_cparams(("parallel", "arbitrary")),
        name="proj",
    )(x2, mod_l, gain, w_main, w_small)


def _attn_kernel(lam_ref, q_ref, k_ref, v_ref, bias_ref, o_ref, m_scr, l_scr, acc_scr,
                 s0_scr, s1_scr, *, tq, out_scale):
    i = pl.program_id(2)
    lam = lam_ref[0]
    q = q_ref[...].astype(F32) * (DA_QK_DIM ** -0.5 * LOG2E)
    lane = lax.broadcasted_iota(jnp.int32, q.shape, 1)
    qa = jnp.where(lane < DA_QK_DIM, q, 0.0).astype(BF16)
    qb = jnp.where(lane >= DA_QK_DIM, q, 0.0).astype(BF16)
    q2 = jnp.concatenate([qa, qb], axis=0)

    m_scr[...] = jnp.full_like(m_scr, NEG)
    l_scr[...] = jnp.zeros_like(l_scr)
    acc_scr[...] = jnp.zeros_like(acc_scr)

    def scores(j):
        r0 = pl.multiple_of(j * tq, tq)
        return _dot_nt(k_ref[pl.ds(r0, tq), :].astype(BF16), q2)

    def accumulate(j, s):
        r0 = pl.multiple_of(j * tq, tq)
        vt = v_ref[pl.ds(r0, tq), :].astype(BF16)
        m_old = m_scr[...]
        m_new = jnp.maximum(m_old, jnp.max(s, axis=0, keepdims=True))
        p = jnp.exp2(s - m_new)
        alpha = jnp.exp2(m_old - m_new)
        l_scr[...] = alpha * l_scr[...] + jnp.sum(p, axis=0, keepdims=True)
        acc_scr[...] = alpha * acc_scr[...] + _dot_tn(vt, p.astype(BF16))
        m_scr[...] = m_new

    def biased_scores(j, bias):
        return scores(j) + jnp.concatenate([bias, bias], axis=1)

    n_far = jnp.maximum(i - 1, 0)
    pairs = n_far // 2
    j_prev = jnp.maximum(i - 1, 0)
    s0_scr[...] = biased_scores(i, bias_ref[1])
    s1_scr[...] = biased_scores(j_prev, bias_ref[jnp.where(i >= 1, 0, 2)])
    accumulate(i, s0_scr[...])
    s0_scr[...] = scores(0)
    accumulate(j_prev, s1_scr[...])

    def far_pair(g, carry):
        s1_scr[...] = scores(2 * g + 1)
        accumulate(2 * g, s0_scr[...])
        s0_scr[...] = scores(jnp.minimum(2 * g + 2, n_far - 1))
        accumulate(2 * g + 1, s1_scr[...])
        return carry

    lax.fori_loop(0, pairs, far_pair, 0)

    @pl.when(n_far > 2 * pairs)
    def _():
        accumulate(n_far - 1, s0_scr[...])

    on = acc_scr[...] / l_scr[...]
    ot = on[:, 0:tq] - lam * on[:, tq:2 * tq]
    ot = ot * (lax.rsqrt(jnp.mean(ot * ot, axis=0, keepdims=True) + EPS) * out_scale)
    o_ref[...] = ot.T.astype(o_ref.dtype)


def _attention(lam, pm, bias_tiles, batch, seq, lam_init, out_dtype):
    tq = min(TQ_ATT, seq)
    nq = seq // tq
    kern = functools.partial(_attn_kernel, tq=tq, out_scale=1.0 - lam_init)
    scratch = [pltpu.VMEM((1, 2 * tq), F32), pltpu.VMEM((1, 2 * tq), F32),
               pltpu.VMEM((DA_V_DIM, 2 * tq), F32),
               pltpu.VMEM((tq, 2 * tq), F32), pltpu.VMEM((tq, 2 * tq), F32)]
    qb, kb, vb = OFF_DA_Q // LANES, OFF_DA_K // LANES, OFF_DA_V // LANES
    return pl.pallas_call(
        kern,
        grid=(batch, DA_HEADS, nq),
        in_specs=[
            pl.BlockSpec(memory_space=pltpu.SMEM),
            pl.BlockSpec((tq, LANES), lambda b, h, i: (b * nq + i, qb + h)),
            pl.BlockSpec((seq, LANES), lambda b, h, i: (b, kb + h)),
            pl.BlockSpec((seq, LANES), lambda b, h, i: (b, vb + h)),
            pl.BlockSpec((None, 3, tq, tq), lambda b, h, i: (h, 0, 0, 0)),
        ],
        out_specs=pl.BlockSpec((tq, LANES), lambda b, h, i: (b * nq + i, h)),
        out_shape=jax.ShapeDtypeStruct((batch * seq, DA_HEADS * DA_V_DIM), out_dtype),
        scratch_shapes=scratch,
        compiler_params=_cparams(("parallel", "parallel", "arbitrary")),
        name="diff_attn",
    )(lam, pm, pm, pm, bias_tiles)


def _t5_bucket_table(n_max):
    n = np.arange(n_max)
    exact = N_BUCKETS // 2
    nf = np.maximum(n, 1).astype(np.float64)
    large = exact + (np.log(nf / exact) / math.log(MAX_DISTANCE / exact)
                     * (N_BUCKETS - exact)).astype(np.int64)
    return np.where(n < exact, n, np.minimum(large, N_BUCKETS - 1)).astype(np.int32)


def _bias_tiles(rel_bias, tq):
    assert tq >= MAX_DISTANCE
    n = tq
    heads = rel_bias.shape[1]
    rb = rel_bias.astype(F32)
    near = (rb[_t5_bucket_table(MAX_DISTANCE)] - rb[N_BUCKETS - 1][None, :]) * LOG2E
    f = jnp.concatenate([near, jnp.zeros((2 * n - MAX_DISTANCE, heads), F32)], axis=0)

    def toeplitz(v):
        vp = jnp.concatenate([v, jnp.zeros((1, heads), F32)], axis=0)
        flat = jnp.tile(vp, (n, 1))[: n * (2 * n - 1)]
        return jnp.transpose(flat.reshape(n, 2 * n - 1, heads)[:, n - 1:, :], (2, 0, 1))

    prev = toeplitz(f[1:2 * n])
    diag = toeplitz(jnp.concatenate([jnp.full((n - 1, heads), NEG, F32), f[0:n]], axis=0))
    masked = jnp.full_like(prev, NEG)
    return jnp.stack([prev, diag, masked], axis=1)


def _mlstm_kernel(q_ref, k_ref, v_ref, og_ref, sm_ref, qh_ref, kh_ref, cw_ref, gb_ref,
                  tril_ref, out_ref, xq_scr, xk_scr, c_scr, n_scr, m_scr, *, L):
    ci = pl.program_id(1)

    @pl.when(ci == 0)
    def _():
        c_scr[...] = jnp.zeros_like(c_scr)
        n_scr[...] = jnp.zeros_like(n_scr)
        m_scr[...] = jnp.zeros_like(m_scr)

    halo = BF16_ROWS
    keep = (ci > 0).astype(F32)
    xq_scr[0:halo, :] = qh_ref[...].astype(F32) * keep
    xk_scr[0:halo, :] = kh_ref[...].astype(F32) * keep
    xq_scr[halo:halo + L, :] = q_ref[...].astype(F32)
    xk_scr[halo:halo + L, :] = k_ref[...].astype(F32)

    def conv_silu(scr, w):
        y = scr[halo:halo + L, :] * w[ML_CONV - 1:ML_CONV, :]
        for j in range(ML_CONV - 1):
            off = halo - (ML_CONV - 1) + j
            y = y + scr[off:off + L, :] * w[j:j + 1, :]
        return _silu(y)

    w_all = cw_ref[...]
    width = ML_HEADS * ML_DIM
    q_all = conv_silu(xq_scr, w_all[:, 0:width])
    k_all = conv_silu(xk_scr, w_all[:, width:2 * width]) * (ML_DIM ** -0.5)

    g = sm_ref[...] + gb_ref[...]
    lane = lax.broadcasted_iota(jnp.int32, g.shape, 1)
    is_f = (lane >= SM_ML_F) & (lane < SM_ML_F + ML_HEADS)
    g = jnp.where(is_f, _log_sigmoid(g), g)
    bcum = _dot_exact(tril_ref[...], g)
    g_t = g.T
    b_t = bcum.T
    row = lax.broadcasted_iota(jnp.int32, (L, L), 0)
    colj = lax.broadcasted_iota(jnp.int32, (L, L), 1)
    causal = colj <= row

    for h in range(ML_HEADS):
        sl = slice(h * ML_DIM, (h + 1) * ML_DIM)
        qh = q_all[:, sl]
        kh = k_all[:, sl]
        vh = v_ref[:, sl].astype(F32)
        qb, kb, vb = qh.astype(BF16), kh.astype(BF16), vh.astype(BF16)
        ig_col = g[:, SM_ML_I + h:SM_ML_I + h + 1]
        b_col = bcum[:, SM_ML_F + h:SM_ML_F + h + 1]
        ig_row = g_t[SM_ML_I + h:SM_ML_I + h + 1, :]
        b_row = b_t[SM_ML_F + h:SM_ML_F + h + 1, :]
        m_old = m_scr[h][:, 0:1]
        c_old = c_scr[h]
        n_old = n_scr[h]

        dm = jnp.where(causal, b_col - b_row + ig_row, NEG)
        inter = b_col + m_old
        m_t = jnp.maximum(inter, jnp.max(dm, axis=-1, keepdims=True))
        s = _dot_nt(qb, kb) * jnp.exp(dm - m_t)
        a = jnp.exp(inter - m_t)
        num = a * _dot(qb, c_old.astype(BF16)) + _dot(s.astype(BF16), vb)
        den = (a * jnp.sum(qh * n_old, axis=-1, keepdims=True)
               + jnp.sum(s, axis=-1, keepdims=True))
        hv = num / jnp.maximum(jnp.abs(den), jnp.exp(-m_t))
        out_ref[:, sl] = (_sigmoid(og_ref[:, sl].astype(F32)) * hv).astype(out_ref.dtype)

        m_new = m_t[L - 1:L, :]
        b_last = b_col[L - 1:L, :]
        a_state = jnp.exp(b_last + m_old - m_new)
        w_col = jnp.exp(b_last - b_col + ig_col - m_new)
        kw = kh * w_col
        c_scr[h] = a_state * c_old + _dot_tn(kw.astype(BF16), vb)
        n_scr[h] = a_state * n_old + jnp.sum(kw, axis=0, keepdims=True)
        m_scr[h] = jnp.broadcast_to(m_new, (1, LANES))


def _mlstm(pm, ps, conv_w, gate_row, batch, seq, out_dtype):
    L = min(L_MLSTM, seq)
    nc = seq // L
    width = ML_HEADS * ML_DIM
    qb, kb, vb, ob = (OFF_ML_Q // width, OFF_ML_K // width, OFF_ML_V // width, OFF_ML_O // width)
    lb = L // BF16_ROWS
    tril = jnp.asarray(np.tril(np.ones((L, L), np.float32)))

    def halo_map(colblk):
        return lambda b, c: (jnp.maximum(b * (seq // BF16_ROWS) + c * lb - 1, 0), colblk)

    return pl.pallas_call(
        functools.partial(_mlstm_kernel, L=L),
        grid=(batch, nc),
        in_specs=[
            pl.BlockSpec((L, width), lambda b, c: (b * nc + c, qb)),
            pl.BlockSpec((L, width), lambda b, c: (b * nc + c, kb)),
            pl.BlockSpec((L, width), lambda b, c: (b * nc + c, vb)),
            pl.BlockSpec((L, width), lambda b, c: (b * nc + c, ob)),
            pl.BlockSpec((L, SMALL_W), lambda b, c: (b * nc + c, 0)),
            pl.BlockSpec((BF16_ROWS, width), halo_map(qb)),
            pl.BlockSpec((BF16_ROWS, width), halo_map(kb)),
            pl.BlockSpec((ML_CONV, 2 * width), lambda b, c: (0, 0)),
            pl.BlockSpec((1, SMALL_W), lambda b, c: (0, 0)),
            pl.BlockSpec((L, L), lambda b, c: (0, 0)),
        ],
        out_specs=pl.BlockSpec((L, width), lambda b, c: (b * nc + c, 0)),
        out_shape=jax.ShapeDtypeStruct((batch * seq, width), out_dtype),
        scratch_shapes=[
            pltpu.VMEM((L + BF16_ROWS, width), F32),
            pltpu.VMEM((L + BF16_ROWS, width), F32),
            pltpu.VMEM((ML_HEADS, ML_DIM, ML_DIM), F32),
            pltpu.VMEM((ML_HEADS, 1, ML_DIM), F32),
            pltpu.VMEM((ML_HEADS, 1, LANES), F32),
        ],
        compiler_params=_cparams(("parallel", "arbitrary")),
        name="mlstm",
    )(pm, pm, pm, pm, ps, pm, pm, conv_w, gate_row, tril)


def _gla_kernel(q_ref, k_ref, v_ref, r_ref, sm_ref, wa_ref, ba_ref, tril_ref, mexp_ref,
                out_ref, bc_scr, a_scr, st_scr, *, L, c):
    ci = pl.program_id(1)

    @pl.when(ci == 0)
    def _():
        st_scr[...] = jnp.zeros_like(st_scr)

    la = _log_sigmoid(_dot_exact(sm_ref[...], wa_ref[...]) + ba_ref[...]) * (1.0 / GLA_TAU)
    bc_scr[...] = _dot_exact(tril_ref[...], la)
    kw = GLA_HEADS * GLA_DK
    rowc = lax.broadcasted_iota(jnp.int32, (c, kw), 0)
    lane_head = lax.broadcasted_iota(jnp.int32, (c, kw), 1) // GLA_DK
    head_masks = [lane_head == h for h in range(GLA_HEADS)]

    def stack_heads(x):
        return jnp.concatenate([jnp.where(mk, x, 0.0) for mk in head_masks], axis=0)

    def sub(i, carry):
        r0 = pl.multiple_of(i * c, c)
        qs = q_ref[pl.ds(r0, c), :].astype(F32) * (GLA_DK ** -0.5)
        ks = k_ref[pl.ds(r0, c), :].astype(F32)
        vs = v_ref[pl.ds(r0, c), :].astype(BF16).astype(F32)
        bcs = bc_scr[pl.ds(r0, c), :]
        e_end = bcs[c - 1:c, :]
        st = st_scr[...]
        o_stack = _dot_nt(stack_heads(qs * jnp.exp(bcs)).astype(BF16), st.astype(BF16))
        o = jnp.concatenate([o_stack[h * c:(h + 1) * c] for h in range(GLA_HEADS)], axis=1)

        for t in range(c):
            dec = jnp.exp(jnp.minimum(bcs[t:t + 1, :] - bcs, 0.0))
            a_t = jnp.where(rowc <= t, qs[t:t + 1, :] * ks * dec, 0.0)
            a_scr[t * c:(t + 1) * c, :] = a_t.astype(BF16)
        p = _dot(a_scr[...], mexp_ref[...])
        o = o + jnp.sum(p.reshape(c, c, GLA_HEADS * GLA_DV) * vs[None, :, :], axis=1)

        outs = [_rms(o[:, h * GLA_DV:(h + 1) * GLA_DV]) for h in range(GLA_HEADS)]
        on = jnp.concatenate(outs, axis=1)
        out_ref[pl.ds(r0, c), :] = (on * _silu(r_ref[pl.ds(r0, c), :].astype(F32))).astype(out_ref.dtype)

        khat = stack_heads(ks * jnp.exp(e_end - bcs)).astype(BF16)
        v_stack = jnp.concatenate([vs[:, h * GLA_DV:(h + 1) * GLA_DV] for h in range(GLA_HEADS)],
                                  axis=0).astype(BF16)
        st_scr[...] = st * jnp.exp(e_end) + _dot_tn(v_stack, khat)
        return carry

    lax.fori_loop(0, L // c, sub, 0, unroll=4)


def _gla(pm, ps, wa_pad, ba_row, batch, seq, out_dtype):
    L = min(L_GLA, seq)
    c = C_GLA
    nc = seq // L
    kw, vw = GLA_HEADS * GLA_DK, GLA_HEADS * GLA_DV
    qb, kb, vb, rb = OFF_GL_Q // kw, OFF_GL_K // kw, OFF_GL_V // vw, OFF_GL_R // vw
    idx = np.arange(L)
    tril = ((idx[:, None] >= idx[None, :]) & (idx[:, None] // c == idx[None, :] // c))
    tril = jnp.asarray(tril.astype(np.float32))
    mexp = np.zeros((kw, vw), np.float32)
    for h in range(GLA_HEADS):
        mexp[h * GLA_DK:(h + 1) * GLA_DK, h * GLA_DV:(h + 1) * GLA_DV] = 1.0
    mexp = jnp.asarray(mexp, dtype=BF16)
    return pl.pallas_call(
        functools.partial(_gla_kernel, L=L, c=c),
        grid=(batch, nc),
        in_specs=[
            pl.BlockSpec((L, kw), lambda b, i: (b * nc + i, qb)),
            pl.BlockSpec((L, kw), lambda b, i: (b * nc + i, kb)),
            pl.BlockSpec((L, vw), lambda b, i: (b * nc + i, vb)),
            pl.BlockSpec((L, vw), lambda b, i: (b * nc + i, rb)),
            pl.BlockSpec((L, SMALL_W), lambda b, i: (b * nc + i, 0)),
            pl.BlockSpec((SMALL_W, kw), lambda b, i: (0, 0)),
            pl.BlockSpec((1, kw), lambda b, i: (0, 0)),
            pl.BlockSpec((L, L), lambda b, i: (0, 0)),
            pl.BlockSpec((kw, vw), lambda b, i: (0, 0)),
        ],
        out_specs=pl.BlockSpec((L, vw), lambda b, i: (b * nc + i, 0)),
        out_shape=jax.ShapeDtypeStruct((batch * seq, vw), out_dtype),
        scratch_shapes=[
            pltpu.VMEM((L, kw), F32),
            pltpu.VMEM((c * c, kw), BF16),
            pltpu.VMEM((GLA_DV, kw), F32),
        ],
        compiler_params=_cparams(("parallel", "arbitrary")),
        name="gla",
    )(pm, pm, pm, pm, ps, wa_pad, ba_row, tril, mexp)


S5_NSTATE = S5_GROUPS * S5_STATE
S5_BLK = 4
S5_BLK_STATE = S5_NSTATE // S5_BLK


def _gelu_tanh(x):
    return 0.5 * x * (1.0 + jnp.tanh(math.sqrt(2.0 / math.pi) * (x + 0.044715 * (x * x * x))))


def _s5_kernel(u_ref, bre_ref, bim_ref, cre_ref, cim_ref, as_ref, pw_ref, d_ref, gw_ref,
               gb_ref, out_ref, xr_scr, xi_scr, cr_scr, ci_scr, *, rows, batch):
    ti = pl.program_id(0)

    @pl.when(ti == 0)
    def _():
        cr_scr[...] = jnp.zeros_like(cr_scr)
        ci_scr[...] = jnp.zeros_like(ci_scr)

    u = u_ref[...].astype(F32)
    ub = u.astype(BF16)
    nb = S5_BLK_STATE
    for q in range(S5_BLK):
        uq = ub[:, q * LANES:(q + 1) * LANES]
        xr_scr[:, q * nb:(q + 1) * nb] = _dot(uq, bre_ref[q])
        xi_scr[:, q * nb:(q + 1) * nb] = _dot(uq, bim_ref[q])

    rowi = lax.broadcasted_iota(jnp.int32, (SUBLANES, nb), 0)
    shifts = _s5_row_shifts(batch)
    for cc in range(S5_BLK):
        cols = slice(cc * nb, (cc + 1) * nb)

        def body(g, carry, cols=cols):
            cr, ci = carry
            for r in shifts:
                cr = jnp.where(rowi < SUBLANES - r, pltpu.roll(cr, SUBLANES - r, 0), cr)
                ci = jnp.where(rowi < SUBLANES - r, pltpu.roll(ci, SUBLANES - r, 0), ci)
            r0 = pl.multiple_of(g * SUBLANES, SUBLANES)
            zr = xr_scr[pl.ds(r0, SUBLANES), cols]
            zi = xi_scr[pl.ds(r0, SUBLANES), cols]
            for si, r in enumerate(shifts):
                sr = pltpu.roll(zr, r, 0)
                sim = pltpu.roll(zi, r, 0)
                ar = as_ref[0, si, :, cols]
                ai = as_ref[1, si, :, cols]
                zr, zi = zr + ar * sr - ai * sim, zi + ar * sim + ai * sr
            p_r = pw_ref[0, :, cols]
            p_i = pw_ref[1, :, cols]
            xr = zr + p_r * cr - p_i * ci
            xi = zi + p_r * ci + p_i * cr
            xr_scr[pl.ds(r0, SUBLANES), cols] = xr
            xi_scr[pl.ds(r0, SUBLANES), cols] = xi
            return xr, xi

        cr, ci = lax.fori_loop(0, rows // SUBLANES, body, (cr_scr[:, cols], ci_scr[:, cols]))
        cr_scr[:, cols] = cr
        ci_scr[:, cols] = ci

    ys = []
    for q in range(S5_BLK):
        xr = xr_scr[:, q * nb:(q + 1) * nb].astype(BF16)
        xi = xi_scr[:, q * nb:(q + 1) * nb].astype(BF16)
        ys.append(_dot(xr, cre_ref[q]) + _dot(xi, cim_ref[q]))
    y = jnp.concatenate(ys, axis=1) + d_ref[...] * u
    z = _gelu_tanh(y)
    gate = _sigmoid(_dot(z.astype(BF16), gw_ref[...]) + gb_ref[...])
    out_ref[...] = (z * gate).astype(out_ref.dtype)


def _s5_row_shifts(batch):
    assert SUBLANES % batch == 0
    return tuple(batch * (1 << k) for k in range(8) if batch * (1 << k) < SUBLANES)


def _s5_params(a_re, a_im, log_dt, b_re, b_im, c_re, c_im, batch):
    a_re, a_im = a_re.astype(F32), a_im.astype(F32)
    dt = jnp.exp(log_dt.astype(F32))[:, None]
    mag = jnp.exp(dt * a_re)
    ab_re, ab_im = mag * jnp.cos(dt * a_im), mag * jnp.sin(dt * a_im)
    nr, ni = ab_re - 1.0, ab_im
    den = a_re * a_re + a_im * a_im
    f_re = (nr * a_re + ni * a_im) / den
    f_im = (ni * a_re - nr * a_im) / den
    b_re, b_im = b_re.astype(F32), b_im.astype(F32)
    bb_re = f_re[..., None] * b_re - f_im[..., None] * b_im
    bb_im = f_re[..., None] * b_im + f_im[..., None] * b_re

    def apow(k):
        mk = jnp.exp(k * dt * a_re)
        return (mk * jnp.cos(k * dt * a_im)).reshape(-1), (mk * jnp.sin(k * dt * a_im)).reshape(-1)

    rows = np.arange(SUBLANES)[:, None]

    def shift_table(r, part):
        return jnp.where(jnp.asarray(rows >= r), apow(float(r // batch))[part][None, :], 0.0)

    as_arr = jnp.stack([jnp.stack([shift_table(r, part) for r in _s5_row_shifts(batch)])
                        for part in (0, 1)])
    pws = [apow(float(k // batch + 1)) for k in range(SUBLANES)]
    pw_arr = jnp.stack([jnp.stack([p[0] for p in pws]), jnp.stack([p[1] for p in pws])])

    gpb = S5_GROUPS // S5_BLK
    eye = jnp.eye(gpb, dtype=F32)

    def pack_b(bb):
        bb = bb.reshape(S5_BLK, gpb, S5_STATE, S5_CH)
        return jnp.einsum('qgpc,gh->qgchp', bb, eye).reshape(S5_BLK, gpb * S5_CH, gpb * S5_STATE)

    def pack_c(cc):
        cc = cc.reshape(S5_BLK, gpb, S5_CH, S5_STATE)
        return jnp.einsum('qgcp,gh->qgphc', cc, eye).reshape(S5_BLK, gpb * S5_STATE, gpb * S5_CH)

    return (pack_b(bb_re).astype(BF16), pack_b(bb_im).astype(BF16),
            pack_c(c_re.astype(F32)).astype(BF16), pack_c(-c_im.astype(F32)).astype(BF16),
            as_arr, pw_arr)


def _s5(u_tb, packed, d_row, glu_w, glu_b, batch, seq, out_dtype):
    rows = min(TM_S5, seq) * batch
    bre, bim, cre, cim, as_arr, pw_arr = packed
    w = BRANCH_WIDTH
    full = lambda *shape: pl.BlockSpec(shape, lambda i: (0,) * len(shape))
    return pl.pallas_call(
        functools.partial(_s5_kernel, rows=rows, batch=batch),
        grid=(seq * batch // rows,),
        in_specs=[
            pl.BlockSpec((rows, w), lambda i: (i, 0)),
            full(S5_BLK, LANES, S5_BLK_STATE), full(S5_BLK, LANES, S5_BLK_STATE),
            full(S5_BLK, S5_BLK_STATE, LANES), full(S5_BLK, S5_BLK_STATE, LANES),
            full(*as_arr.shape), full(2, SUBLANES, S5_NSTATE),
            full(1, w), full(w, w), full(1, w),
        ],
        out_specs=pl.BlockSpec((rows, w), lambda i: (i, 0)),
        out_shape=jax.ShapeDtypeStruct((seq * batch, w), out_dtype),
        scratch_shapes=[
            pltpu.VMEM((rows, S5_NSTATE), F32),
            pltpu.VMEM((rows, S5_NSTATE), F32),
            pltpu.VMEM((SUBLANES, S5_NSTATE), F32),
            pltpu.VMEM((SUBLANES, S5_NSTATE), F32),
        ],
        compiler_params=_cparams(("arbitrary",)),
        name="s5",
    )(u_tb, bre, bim, cre, cim, as_arr, pw_arr, d_row, glu_w, glu_b)


def _merge_kernel(x_ref, mod_ref, g_ref, oa_ref, ob_ref, oc_ref, od_ref, wg_ref, bg_ref,
                  wb_ref, wo_ref, out_ref, h_scr, acc_scr):
    n = pl.program_id(1)

    @pl.when(n == 0)
    def _():
        _modulated_norm_into(h_scr, x_ref, g_ref[0:1, :], mod_ref[0:1, :], mod_ref[1:2, :])
        acc_scr[...] = jnp.zeros_like(acc_scr)

    hb = h_scr[...]
    merged = None
    for i, o_ref in enumerate((oa_ref, ob_ref, oc_ref, od_ref)):
        gate = _sigmoid(_dot(hb, wg_ref[i]) + bg_ref[i])
        term = gate * _dot(o_ref[...].astype(BF16), wb_ref[i])
        merged = term if merged is None else merged + term
    acc_scr[...] += _dot(merged.astype(BF16), wo_ref[...])

    @pl.when(n == pl.num_programs(1) - 1)
    def _():
        _gated_residual_into(out_ref, x_ref, acc_scr, g_ref[1:2, :], mod_ref[2:3, :])


def _merge(x2, mod_l, gains, oa, ob, oc, od, w_gate, b_gate, w_branch, w_out, seq):
    t, d = x2.shape
    tm, tn = min(TM_MERGE, seq), TN_MERGE
    tiles_per_seq = seq // tm
    w = BRANCH_WIDTH
    br_spec = pl.BlockSpec((tm, w), lambda i, n: (i, 0))
    return pl.pallas_call(
        _merge_kernel,
        grid=(t // tm, d // tn),
        in_specs=[
            pl.BlockSpec((tm, d), lambda i, n: (i, 0)),
            pl.BlockSpec((None, SUBLANES, d), lambda i, n: (i // tiles_per_seq, 0, 0)),
            pl.BlockSpec((2, d), lambda i, n: (0, 0)),
            br_spec, br_spec, br_spec,
            pl.BlockSpec((tm, w), lambda i, n: (i % tiles_per_seq, i // tiles_per_seq)),
            pl.BlockSpec((N_BRANCH, d, tn), lambda i, n: (0, 0, n)),
            pl.BlockSpec((N_BRANCH, 1, tn), lambda i, n: (0, 0, n)),
            pl.BlockSpec((N_BRANCH, w, tn), lambda i, n: (0, 0, n)),
            pl.BlockSpec((tn, d), lambda i, n: (n, 0)),
        ],
        out_specs=pl.BlockSpec((tm, d), lambda i, n: (i, 0)),
        out_shape=jax.ShapeDtypeStruct((t, d), F32),
        scratch_shapes=[pltpu.VMEM((tm, d), BF16), pltpu.VMEM((tm, d), F32)],
        compiler_params=_cparams(("parallel", "arbitrary")),
        name="merge",
    )(x2, mod_l, gains, oa, ob, oc, od, w_gate, b_gate, w_branch, w_out)


def _ffn_kernel(x_ref, mod_ref, g_ref, wa_ref, wg_ref, wo_ref, out_ref, h_scr, acc_scr):
    j = pl.program_id(1)

    @pl.when(j == 0)
    def _():
        _modulated_norm_into(h_scr, x_ref, g_ref[0:1, :], mod_ref[3:4, :], mod_ref[4:5, :])
        acc_scr[...] = jnp.zeros_like(acc_scr)

    hb = h_scr[...]
    a = _dot(hb, wa_ref[...])
    g = _dot(hb, wg_ref[...])
    acc_scr[...] += _dot((_silu(a) * g).astype(BF16), wo_ref[...])

    @pl.when(j == pl.num_programs(1) - 1)
    def _():
        _gated_residual_into(out_ref, x_ref, acc_scr, g_ref[1:2, :], mod_ref[5:6, :])


def _ffn(x2, mod_l, gains, w_in, w_out, seq):
    t, d = x2.shape
    tm, th = min(TM_FFN, seq), TH_FFN
    tiles_per_seq = seq // tm
    nh = FFN_HIDDEN // th
    return pl.pallas_call(
        _ffn_kernel,
        grid=(t // tm, nh),
        in_specs=[
            pl.BlockSpec((tm, d), lambda i, j: (i, 0)),
            pl.BlockSpec((None, SUBLANES, d), lambda i, j: (i // tiles_per_seq, 0, 0)),
            pl.BlockSpec((2, d), lambda i, j: (0, 0)),
            pl.BlockSpec((d, th), lambda i, j: (0, j)),
            pl.BlockSpec((d, th), lambda i, j: (0, j + nh)),
            pl.BlockSpec((th, d), lambda i, j: (j, 0)),
        ],
        out_specs=pl.BlockSpec((tm, d), lambda i, j: (i, 0)),
        out_shape=jax.ShapeDtypeStruct((t, d), F32),
        scratch_shapes=[pltpu.VMEM((tm, d), BF16), pltpu.VMEM((tm, d), F32)],
        compiler_params=_cparams(("parallel", "arbitrary")),
        name="ffn",
    )(x2, mod_l, gains, w_in, w_in, w_out)


def _split_w_in(w):
    d = w.shape[0]
    main = jnp.concatenate([w[:, :3584], w[:, 3592:5128], w[:, 5144:5656]], axis=1)
    small = jnp.concatenate([w[:, 3584:3592], w[:, 5128:5144],
                             jnp.zeros((d, SMALL_W - 2 * ML_HEADS - GLA_RANK), w.dtype)], axis=1)
    return main.astype(BF16), small.astype(BF16)


def _pad_row(vals, offset):
    row = jnp.zeros((1, SMALL_W), F32)
    return lax.dynamic_update_slice(row, vals.reshape(1, -1).astype(F32), (0, offset))


ACT_DTYPE = BF16


def kernel(x, c, ada_w, ada_b, norm_g, w_in, rel_bias, diff_lambda, ml_conv, ml_gate_b,
           gla_wa2, gla_ba, s5_a_re, s5_a_im, s5_log_dt, s5_b_re, s5_b_im, s5_c_re, s5_c_im,
           s5_d, s5_glu_w, s5_glu_b, w_branch, w_gate, b_gate, w_out, ffn_w_in, ffn_w_out):
    batch, seq, d = x.shape
    depth = ada_w.shape[0]
    t = batch * seq

    c_pad = jnp.concatenate([c, jnp.zeros((SUBLANES - batch, d), c.dtype)], axis=0)
    mod = _adaln(c_pad, ada_w, ada_b)[:, :batch]
    mod = mod.reshape(depth, batch, N_MOD, d)
    mod = jnp.concatenate([mod, jnp.zeros((depth, batch, SUBLANES - N_MOD, d), F32)], axis=2)

    bias_tiles = _bias_tiles(rel_bias, min(TQ_ATT, seq))

    x2 = x.reshape(t, d)
    for l in range(depth):
        w_main, w_small = _split_w_in(w_in[l])
        pm, ps, pu = _proj(x2, mod[l], norm_g[l, 0:1], w_main, w_small, seq, ACT_DTYPE)

        lam_init = 0.8 - 0.6 * math.exp(-0.3 * l)
        lp = diff_lambda[l].astype(F32)
        lam = (jnp.exp(jnp.sum(lp[0] * lp[1])) - jnp.exp(jnp.sum(lp[2] * lp[3])) + lam_init)
        o_a = _attention(lam.reshape(1), pm, bias_tiles, batch, seq, lam_init, BF16)

        gate_row = (_pad_row(ml_gate_b[l, 0], SM_ML_I) + _pad_row(ml_gate_b[l, 1], SM_ML_F))
        o_b = _mlstm(pm, ps, ml_conv[l].astype(F32), gate_row, batch, seq, BF16)

        wa_pad = jnp.zeros((SMALL_W, GLA_HEADS * GLA_DK), F32)
        wa_pad = lax.dynamic_update_slice(wa_pad, gla_wa2[l].astype(F32), (SM_GL_A, 0))
        o_c = _gla(pm, ps, wa_pad, gla_ba[l].reshape(1, -1).astype(F32), batch, seq, BF16)

        packed = _s5_params(s5_a_re[l], s5_a_im[l], s5_log_dt[l], s5_b_re[l], s5_b_im[l],
                            s5_c_re[l], s5_c_im[l], batch)
        o_d = _s5(pu.reshape(seq * batch, BRANCH_WIDTH), packed, s5_d[l].reshape(1, -1).astype(F32),
                  s5_glu_w[l].astype(BF16), s5_glu_b[l].reshape(1, -1).astype(F32), batch, seq, BF16)
        o_d = o_d.reshape(seq, batch * BRANCH_WIDTH)

        x2 = _merge(x2, mod[l], norm_g[l, 0:2], o_a, o_b, o_c, o_d,
                    w_gate[l].astype(BF16), b_gate[l].reshape(N_BRANCH, 1, d).astype(F32),
                    w_branch[l].astype(BF16), w_out[l].astype(BF16), seq)
        x2 = _ffn(x2, mod[l], norm_g[l, 2:4], ffn_w_in[l].astype(BF16),
                  ffn_w_out[l].astype(BF16), seq)
    return x2.reshape(batch, seq, d)
```

```python
import functools
import math

import numpy as np
import jax
import jax.numpy as jnp
from jax import lax
from jax.experimental import pallas as pl
from jax.experimental.pallas import tpu as pltpu

F32 = jnp.float32
BF16 = jnp.bfloat16
HIGHEST = lax.Precision.HIGHEST

D_MODEL = 2048
DEPTH = 4
EPS = 1e-6
N_MOD = 6
N_BRANCH = 4
BRANCH_WIDTH = 512
DA_HEADS = 4
DA_QK_DIM = 64
DA_V_DIM = 128
N_BUCKETS = 32
MAX_DISTANCE = 128
ML_HEADS = 4
ML_DIM = 128
ML_CONV = 4
GLA_HEADS = 4
GLA_DK = 64
GLA_DV = 128
GLA_RANK = 16
GLA_TAU = 16.0
S5_CH = 16
S5_GROUPS = BRANCH_WIDTH // S5_CH
S5_STATE = 64
FFN_HIDDEN = -(-(8 * D_MODEL) // (3 * 256)) * 256

LANES = 128
SUBLANES = 8
BF16_ROWS = 16
VMEM_LIMIT = 56 * 1024 * 1024

MAIN_W = 5632
SMALL_W = LANES
OFF_DA_Q, OFF_DA_K, OFF_DA_V = 0, 512, 1024
OFF_ML_Q, OFF_ML_K, OFF_ML_V, OFF_ML_O = 1536, 2048, 2560, 3072
OFF_GL_Q, OFF_GL_K, OFF_GL_V, OFF_GL_R = 3584, 3840, 4096, 4608
OFF_S5_U = 5120
SM_ML_I, SM_ML_F, SM_GL_A = 0, 4, 8

NEG = -1e30
LOG2E = math.log2(math.e)

TM_PROJ, TN_PROJ = 1024, 512
TQ_ATT = 512
L_MLSTM = 256
L_GLA, C_GLA = 256, 16
TM_S5 = 256
TM_MERGE, TN_MERGE = 512, 256
TM_FFN, TH_FFN = 512, 512


def _cparams(sem):
    return pltpu.CompilerParams(dimension_semantics=sem, vmem_limit_bytes=VMEM_LIMIT)


def _rms(x):
    return x * lax.rsqrt(jnp.mean(x * x, axis=-1, keepdims=True) + EPS)


def _sigmoid(x):
    return 1.0 / (1.0 + jnp.exp(-x))


def _silu(x):
    return x * _sigmoid(x)


def _log_sigmoid(x):
    return jnp.minimum(x, 0.0) - jnp.log1p(jnp.exp(-jnp.abs(x)))


def _dot(a, b):
    return jnp.dot(a, b, preferred_element_type=F32)


def _dot_nt(a, b):
    return lax.dot_general(a, b, (((1,), (1,)), ((), ())), preferred_element_type=F32)


def _dot_tn(a, b):
    return lax.dot_general(a, b, (((0,), (0,)), ((), ())), preferred_element_type=F32)


def _dot_exact(a, b):
    return jnp.dot(a, b, preferred_element_type=F32, precision=HIGHEST)


def _adaln_kernel(c_ref, w_ref, b_ref, o_ref):
    c = c_ref[...]
    o_ref[...] = _dot(_silu(c).astype(BF16), w_ref[...].astype(BF16)) + b_ref[...]


def _adaln(c_pad, ada_w, ada_b):
    depth, d, n = ada_w.shape
    rows = c_pad.shape[0]
    tn = 2048
    return pl.pallas_call(
        _adaln_kernel,
        grid=(depth, n // tn),
        in_specs=[
            pl.BlockSpec((rows, d), lambda l, j: (0, 0)),
            pl.BlockSpec((None, d, tn), lambda l, j: (l, 0, j)),
            pl.BlockSpec((None, 1, tn), lambda l, j: (l, 0, j)),
        ],
        out_specs=pl.BlockSpec((None, rows, tn), lambda l, j: (l, 0, j)),
        out_shape=jax.ShapeDtypeStruct((depth, rows, n), F32),
        compiler_params=_cparams(("parallel", "parallel")),
        name="adaln",
    )(c_pad, ada_w, ada_b.reshape(depth, 1, n))


ROW_CHUNK = 32


def _for_row_chunks(n_rows, fn):
    def body(c, carry):
        fn(pl.ds(pl.multiple_of(c * ROW_CHUNK, ROW_CHUNK), ROW_CHUNK))
        return carry

    lax.fori_loop(0, n_rows // ROW_CHUNK, body, 0, unroll=4)


def _modulated_norm_into(h_scr, x_ref, gain, shift, scale):
    gs = gain * (1.0 + scale)

    def chunk(rows):
        h_scr[rows, :] = (_rms(x_ref[rows, :]) * gs + shift).astype(h_scr.dtype)

    _for_row_chunks(h_scr.shape[0], chunk)


def _gated_residual_into(out_ref, x_ref, acc_scr, gain, gate):
    gg = gain * gate

    def chunk(rows):
        out_ref[rows, :] = x_ref[rows, :] + _rms(acc_scr[rows, :]) * gg

    _for_row_chunks(out_ref.shape[0], chunk)


def _proj_kernel(x_ref, mod_ref, g_ref, w_ref, ws_ref, o_ref, os_ref, ou_ref, h_scr, *, j_s5):
    j = pl.program_id(1)

    @pl.when(j == 0)
    def _():
        gs = g_ref[...] * (1.0 + mod_ref[1:2, :])
        hb = (_rms(x_ref[...]) * gs + mod_ref[0:1, :]).astype(BF16)
        h_scr[...] = hb
        os_ref[...] = _dot(hb, ws_ref[...])

    res = _dot(h_scr[...], w_ref[...]).astype(o_ref.dtype)
    o_ref[...] = res

    @pl.when(j == j_s5)
    def _():
        ou_ref[...] = res


def _proj(x2, mod_l, gain, w_main, w_small, seq, out_dtype):
    t, d = x2.shape
    tm, tn = min(TM_PROJ, seq), TN_PROJ
    tiles_per_seq = seq // tm
    assert tn == BRANCH_WIDTH and OFF_S5_U % tn == 0
    return pl.pallas_call(
        functools.partial(_proj_kernel, j_s5=OFF_S5_U // tn),
        grid=(t // tm, MAIN_W // tn),
        in_specs=[
            pl.BlockSpec((tm, d), lambda i, j: (i, 0)),
            pl.BlockSpec((None, SUBLANES, d), lambda i, j: (i // tiles_per_seq, 0, 0)),
            pl.BlockSpec((1, d), lambda i, j: (0, 0)),
            pl.BlockSpec((d, tn), lambda i, j: (0, j)),
            pl.BlockSpec((d, SMALL_W), lambda i, j: (0, 0)),
        ],
        out_specs=[
            pl.BlockSpec((tm, tn), lambda i, j: (i, j)),
            pl.BlockSpec((tm, SMALL_W), lambda i, j: (i, 0)),
            pl.BlockSpec((tm, tn), lambda i, j: (i % tiles_per_seq, i // tiles_per_seq)),
        ],
        out_shape=[
            jax.ShapeDtypeStruct((t, MAIN_W), out_dtype),
            jax.ShapeDtypeStruct((t, SMALL_W), F32),
            jax.ShapeDtypeStruct((seq, (t // seq) * tn), out_dtype),
        ],
        scratch_shapes=[pltpu.VMEM((tm, d), BF16)],
        compiler_params=_cparams(("parallel", "arbitrary")),
        name="proj",
    )(x2, mod_l, gain, w_main, w_small)


def _attn_kernel(lam_ref, q_ref, k_ref, v_ref, bias_ref, o_ref, m_scr, l_scr, acc_scr,
                 s0_scr, s1_scr, *, tq, out_scale):
    i = pl.program_id(2)
    lam = lam_ref[0]
    q = q_ref[...].astype(F32) * (DA_QK_DIM ** -0.5 * LOG2E)
    lane = lax.broadcasted_iota(jnp.int32, q.shape, 1)
    qa = jnp.where(lane < DA_QK_DIM, q, 0.0).astype(BF16)
    qb = jnp.where(lane >= DA_QK_DIM, q, 0.0).astype(BF16)
    q2 = jnp.concatenate([qa, qb], axis=0)

    m_scr[...] = jnp.full_like(m_scr, NEG)
    l_scr[...] = jnp.zeros_like(l_scr)
    acc_scr[...] = jnp.zeros_like(acc_scr)

    def scores(j):
        r0 = pl.multiple_of(j * tq, tq)
        return _dot_nt(k_ref[pl.ds(r0, tq), :].astype(BF16), q2)

    def accumulate(j, s):
        r0 = pl.multiple_of(j * tq, tq)
        vt = v_ref[pl.ds(r0, tq), :].astype(BF16)
        m_old = m_scr[...]
        m_new = jnp.maximum(m_old, jnp.max(s, axis=0, keepdims=True))
        p = jnp.exp2(s - m_new)
        alpha = jnp.exp2(m_old - m_new)
        l_scr[...] = alpha * l_scr[...] + jnp.sum(p, axis=0, keepdims=True)
        acc_scr[...] = alpha * acc_scr[...] + _dot_tn(vt, p.astype(BF16))
        m_scr[...] = m_new

    def biased_scores(j, bias):
        return scores(j) + jnp.concatenate([bias, bias], axis=1)

    n_far = jnp.maximum(i - 1, 0)
    j_prev = jnp.maximum(i - 1, 0)
    s0_scr[...] = biased_scores(i, bias_ref[1])
    s1_scr[...] = biased_scores(j_prev, bias_ref[jnp.where(i >= 1, 0, 2)])
    accumulate(i, s0_scr[...])
    s0_scr[...] = scores(0)
    accumulate(j_prev, s1_scr[...])

    def far_pair(base):
        s1_scr[...] = scores(base + 1)
        accumulate(base, s0_scr[...])
        s0_scr[...] = scores(jnp.minimum(base + 2, n_far - 1))
        accumulate(base + 1, s1_scr[...])

    def far_quad(g, carry):
        far_pair(4 * g)
        far_pair(4 * g + 2)
        return carry

    quads = n_far // 4
    lax.fori_loop(0, quads, far_quad, 0)
    rest = n_far - 4 * quads

    @pl.when(rest >= 2)
    def _():
        far_pair(4 * quads)

    @pl.when(rest % 2 == 1)
    def _():
        accumulate(n_far - 1, s0_scr[...])

    on = acc_scr[...] / l_scr[...]
    ot = on[:, 0:tq] - lam * on[:, tq:2 * tq]
    ot = ot * (lax.rsqrt(jnp.mean(ot * ot, axis=0, keepdims=True) + EPS) * out_scale)
    o_ref[...] = ot.T.astype(o_ref.dtype)


def _attention(lam, pm, bias_tiles, batch, seq, lam_init, out_dtype):
    tq = min(TQ_ATT, seq)
    nq = seq // tq
    kern = functools.partial(_attn_kernel, tq=tq, out_scale=1.0 - lam_init)
    scratch = [pltpu.VMEM((1, 2 * tq), F32), pltpu.VMEM((1, 2 * tq), F32),
               pltpu.VMEM((DA_V_DIM, 2 * tq), F32),
               pltpu.VMEM((tq, 2 * tq), F32), pltpu.VMEM((tq, 2 * tq), F32)]
    qb, kb, vb = OFF_DA_Q // LANES, OFF_DA_K // LANES, OFF_DA_V // LANES
    return pl.pallas_call(
        kern,
        grid=(batch, DA_HEADS, nq),
        in_specs=[
            pl.BlockSpec(memory_space=pltpu.SMEM),
            pl.BlockSpec((tq, LANES), lambda b, h, i: (b * nq + i, qb + h)),
            pl.BlockSpec((seq, LANES), lambda b, h, i: (b, kb + h)),
            pl.BlockSpec((seq, LANES), lambda b, h, i: (b, vb + h)),
            pl.BlockSpec((None, 3, tq, tq), lambda b, h, i: (h, 0, 0, 0)),
        ],
        out_specs=pl.BlockSpec((tq, LANES), lambda b, h, i: (b * nq + i, h)),
        out_shape=jax.ShapeDtypeStruct((batch * seq, DA_HEADS * DA_V_DIM), out_dtype),
        scratch_shapes=scratch,
        compiler_params=_cparams(("parallel", "parallel", "arbitrary")),
        name="diff_attn",
    )(lam, pm, pm, pm, bias_tiles)


def _t5_bucket_table(n_max):
    n = np.arange(n_max)
    exact = N_BUCKETS // 2
    nf = np.maximum(n, 1).astype(np.float64)
    large = exact + (np.log(nf / exact) / math.log(MAX_DISTANCE / exact)
                     * (N_BUCKETS - exact)).astype(np.int64)
    return np.where(n < exact, n, np.minimum(large, N_BUCKETS - 1)).astype(np.int32)


def _bias_tiles(rel_bias, tq):
    assert tq >= MAX_DISTANCE
    n = tq
    heads = rel_bias.shape[1]
    rb = rel_bias.astype(F32)
    near = (rb[_t5_bucket_table(MAX_DISTANCE)] - rb[N_BUCKETS - 1][None, :]) * LOG2E
    f = jnp.concatenate([near, jnp.zeros((2 * n - MAX_DISTANCE, heads), F32)], axis=0)

    def toeplitz(v):
        vp = jnp.concatenate([v, jnp.zeros((1, heads), F32)], axis=0)
        flat = jnp.tile(vp, (n, 1))[: n * (2 * n - 1)]
        return jnp.transpose(flat.reshape(n, 2 * n - 1, heads)[:, n - 1:, :], (2, 0, 1))

    prev = toeplitz(f[1:2 * n])
    diag = toeplitz(jnp.concatenate([jnp.full((n - 1, heads), NEG, F32), f[0:n]], axis=0))
    masked = jnp.full_like(prev, NEG)
    return jnp.stack([prev, diag, masked], axis=1)


def _mlstm_kernel(q_ref, k_ref, v_ref, og_ref, sm_ref, qh_ref, kh_ref, cw_ref, gb_ref,
                  tril_ref, out_ref, xq_scr, xk_scr, c_scr, n_scr, m_scr, *, L):
    ci = pl.program_id(1)

    @pl.when(ci == 0)
    def _():
        c_scr[...] = jnp.zeros_like(c_scr)
        n_scr[...] = jnp.zeros_like(n_scr)
        m_scr[...] = jnp.zeros_like(m_scr)

    halo = BF16_ROWS
    keep = (ci > 0).astype(F32)
    xq_scr[0:halo, :] = qh_ref[...].astype(F32) * keep
    xk_scr[0:halo, :] = kh_ref[...].astype(F32) * keep
    xq_scr[halo:halo + L, :] = q_ref[...].astype(F32)
    xk_scr[halo:halo + L, :] = k_ref[...].astype(F32)

    def conv_silu(scr, w):
        y = scr[halo:halo + L, :] * w[ML_CONV - 1:ML_CONV, :]
        for j in range(ML_CONV - 1):
            off = halo - (ML_CONV - 1) + j
            y = y + scr[off:off + L, :] * w[j:j + 1, :]
        return _silu(y)

    w_all = cw_ref[...]
    width = ML_HEADS * ML_DIM
    q_all = conv_silu(xq_scr, w_all[:, 0:width])
    k_all = conv_silu(xk_scr, w_all[:, width:2 * width]) * (ML_DIM ** -0.5)

    g = sm_ref[...] + gb_ref[...]
    lane = lax.broadcasted_iota(jnp.int32, g.shape, 1)
    is_f = (lane >= SM_ML_F) & (lane < SM_ML_F + ML_HEADS)
    g = jnp.where(is_f, _log_sigmoid(g), g)
    bcum = _dot_exact(tril_ref[...], g)
    g_t = g.T
    b_t = bcum.T
    row = lax.broadcasted_iota(jnp.int32, (L, L), 0)
    colj = lax.broadcasted_iota(jnp.int32, (L, L), 1)
    causal = colj <= row

    for h in range(ML_HEADS):
        sl = slice(h * ML_DIM, (h + 1) * ML_DIM)
        qh = q_all[:, sl]
        kh = k_all[:, sl]
        vh = v_ref[:, sl].astype(F32)
        qb, kb, vb = qh.astype(BF16), kh.astype(BF16), vh.astype(BF16)
        ig_col = g[:, SM_ML_I + h:SM_ML_I + h + 1]
        b_col = bcum[:, SM_ML_F + h:SM_ML_F + h + 1]
        ig_row = g_t[SM_ML_I + h:SM_ML_I + h + 1, :]
        b_row = b_t[SM_ML_F + h:SM_ML_F + h + 1, :]
        m_old = m_scr[h][:, 0:1]
        c_old = c_scr[h]
        n_old = n_scr[h]

        dm = jnp.where(causal, b_col - b_row + ig_row, NEG)
        inter = b_col + m_old
        m_t = jnp.maximum(inter, jnp.max(dm, axis=-1, keepdims=True))
        s = _dot_nt(qb, kb) * jnp.exp(dm - m_t)
        a = jnp.exp(inter - m_t)
        num = a * _dot(qb, c_old.astype(BF16)) + _dot(s.astype(BF16), vb)
        den = (a * jnp.sum(qh * n_old, axis=-1, keepdims=True)
               + jnp.sum(s, axis=-1, keepdims=True))
        hv = num / jnp.maximum(jnp.abs(den), jnp.exp(-m_t))
        out_ref[:, sl] = (_sigmoid(og_ref[:, sl].astype(F32)) * hv).astype(out_ref.dtype)

        m_new = m_t[L - 1:L, :]
        b_last = b_col[L - 1:L, :]
        a_state = jnp.exp(b_last + m_old - m_new)
        w_col = jnp.exp(b_last - b_col + ig_col - m_new)
        kw = kh * w_col
        c_scr[h] = a_state * c_old + _dot_tn(kw.astype(BF16), vb)
        n_scr[h] = a_state * n_old + jnp.sum(kw, axis=0, keepdims=True)
        m_scr[h] = jnp.broadcast_to(m_new, (1, LANES))


def _mlstm(pm, ps, conv_w, gate_row, batch, seq, out_dtype):
    L = min(L_MLSTM, seq)
    nc = seq // L
    width = ML_HEADS * ML_DIM
    qb, kb, vb, ob = (OFF_ML_Q // width, OFF_ML_K // width, OFF_ML_V // width, OFF_ML_O // width)
    lb = L // BF16_ROWS
    tril = jnp.asarray(np.tril(np.ones((L, L), np.float32)))

    def halo_map(colblk):
        return lambda b, c: (jnp.maximum(b * (seq // BF16_ROWS) + c * lb - 1, 0), colblk)

    return pl.pallas_call(
        functools.partial(_mlstm_kernel, L=L),
        grid=(batch, nc),
        in_specs=[
            pl.BlockSpec((L, width), lambda b, c: (b * nc + c, qb)),
            pl.BlockSpec((L, width), lambda b, c: (b * nc + c, kb)),
            pl.BlockSpec((L, width), lambda b, c: (b * nc + c, vb)),
            pl.BlockSpec((L, width), lambda b, c: (b * nc + c, ob)),
            pl.BlockSpec((L, SMALL_W), lambda b, c: (b * nc + c, 0)),
            pl.BlockSpec((BF16_ROWS, width), halo_map(qb)),
            pl.BlockSpec((BF16_ROWS, width), halo_map(kb)),
            pl.BlockSpec((ML_CONV, 2 * width), lambda b, c: (0, 0)),
            pl.BlockSpec((1, SMALL_W), lambda b, c: (0, 0)),
            pl.BlockSpec((L, L), lambda b, c: (0, 0)),
        ],
        out_specs=pl.BlockSpec((L, width), lambda b, c: (b * nc + c, 0)),
        out_shape=jax.ShapeDtypeStruct((batch * seq, width), out_dtype),
        scratch_shapes=[
            pltpu.VMEM((L + BF16_ROWS, width), F32),
            pltpu.VMEM((L + BF16_ROWS, width), F32),
            pltpu.VMEM((ML_HEADS, ML_DIM, ML_DIM), F32),
            pltpu.VMEM((ML_HEADS, 1, ML_DIM), F32),
            pltpu.VMEM((ML_HEADS, 1, LANES), F32),
        ],
        compiler_params=_cparams(("parallel", "arbitrary")),
        name="mlstm",
    )(pm, pm, pm, pm, ps, pm, pm, conv_w, gate_row, tril)


def _gla_kernel(q_ref, k_ref, v_ref, r_ref, sm_ref, wa_ref, ba_ref, tril_ref, mexp_ref,
                out_ref, bc_scr, a_scr, st_scr, *, L, c):
    ci = pl.program_id(1)

    @pl.when(ci == 0)
    def _():
        st_scr[...] = jnp.zeros_like(st_scr)

    la = _log_sigmoid(_dot_exact(sm_ref[...], wa_ref[...]) + ba_ref[...]) * (1.0 / GLA_TAU)
    bc_scr[...] = _dot_exact(tril_ref[...], la)
    kw = GLA_HEADS * GLA_DK
    rowc = lax.broadcasted_iota(jnp.int32, (c, kw), 0)
    lane_head = lax.broadcasted_iota(jnp.int32, (c, kw), 1) // GLA_DK
    head_masks = [lane_head == h for h in range(GLA_HEADS)]

    def stack_heads(x):
        return jnp.concatenate([jnp.where(mk, x, 0.0) for mk in head_masks], axis=0)

    def sub(i, carry):
        r0 = pl.multiple_of(i * c, c)
        qs = q_ref[pl.ds(r0, c), :].astype(F32) * (GLA_DK ** -0.5)
        ks = k_ref[pl.ds(r0, c), :].astype(F32)
        vs = v_ref[pl.ds(r0, c), :].astype(BF16).astype(F32)
        bcs = bc_scr[pl.ds(r0, c), :]
        e_end = bcs[c - 1:c, :]
        st = st_scr[...]
        o_stack = _dot_nt(stack_heads(qs * jnp.exp(bcs)).astype(BF16), st.astype(BF16))
        o = jnp.concatenate([o_stack[h * c:(h + 1) * c] for h in range(GLA_HEADS)], axis=1)

        for t in range(c):
            dec = jnp.exp(jnp.minimum(bcs[t:t + 1, :] - bcs, 0.0))
            a_t = jnp.where(rowc <= t, qs[t:t + 1, :] * ks * dec, 0.0)
            a_scr[t * c:(t + 1) * c, :] = a_t.astype(BF16)
        p = _dot(a_scr[...], mexp_ref[...])
        o = o + jnp.sum(p.reshape(c, c, GLA_HEADS * GLA_DV) * vs[None, :, :], axis=1)

        outs = [_rms(o[:, h * GLA_DV:(h + 1) * GLA_DV]) for h in range(GLA_HEADS)]
        on = jnp.concatenate(outs, axis=1)
        out_ref[pl.ds(r0, c), :] = (on * _silu(r_ref[pl.ds(r0, c), :].astype(F32))).astype(out_ref.dtype)

        khat = stack_heads(ks * jnp.exp(e_end - bcs)).astype(BF16)
        v_stack = jnp.concatenate([vs[:, h * GLA_DV:(h + 1) * GLA_DV] for h in range(GLA_HEADS)],
                                  axis=0).astype(BF16)
        st_scr[...] = st * jnp.exp(e_end) + _dot_tn(v_stack, khat)
        return carry

    lax.fori_loop(0, L // c, sub, 0, unroll=8)


def _gla(pm, ps, wa_pad, ba_row, batch, seq, out_dtype):
    L = min(L_GLA, seq)
    c = C_GLA
    nc = seq // L
    kw, vw = GLA_HEADS * GLA_DK, GLA_HEADS * GLA_DV
    qb, kb, vb, rb = OFF_GL_Q // kw, OFF_GL_K // kw, OFF_GL_V // vw, OFF_GL_R // vw
    idx = np.arange(L)
    tril = ((idx[:, None] >= idx[None, :]) & (idx[:, None] // c == idx[None, :] // c))
    tril = jnp.asarray(tril.astype(np.float32))
    mexp = np.zeros((kw, vw), np.float32)
    for h in range(GLA_HEADS):
        mexp[h * GLA_DK:(h + 1) * GLA_DK, h * GLA_DV:(h + 1) * GLA_DV] = 1.0
    mexp = jnp.asarray(mexp, dtype=BF16)
    return pl.pallas_call(
        functools.partial(_gla_kernel, L=L, c=c),
        grid=(batch, nc),
        in_specs=[
            pl.BlockSpec((L, kw), lambda b, i: (b * nc + i, qb)),
            pl.BlockSpec((L, kw), lambda b, i: (b * nc + i, kb)),
            pl.BlockSpec((L, vw), lambda b, i: (b * nc + i, vb)),
            pl.BlockSpec((L, vw), lambda b, i: (b * nc + i, rb)),
            pl.BlockSpec((L, SMALL_W), lambda b, i: (b * nc + i, 0)),
            pl.BlockSpec((SMALL_W, kw), lambda b, i: (0, 0)),
            pl.BlockSpec((1, kw), lambda b, i: (0, 0)),
            pl.BlockSpec((L, L), lambda b, i: (0, 0)),
            pl.BlockSpec((kw, vw), lambda b, i: (0, 0)),
        ],
        out_specs=pl.BlockSpec((L, vw), lambda b, i: (b * nc + i, 0)),
        out_shape=jax.ShapeDtypeStruct((batch * seq, vw), out_dtype),
        scratch_shapes=[
            pltpu.VMEM((L, kw), F32),
            pltpu.VMEM((c * c, kw), BF16),
            pltpu.VMEM((GLA_DV, kw), F32),
        ],
        compiler_params=_cparams(("parallel", "arbitrary")),
        name="gla",
    )(pm, pm, pm, pm, ps, wa_pad, ba_row, tril, mexp)


S5_NSTATE = S5_GROUPS * S5_STATE
S5_BLK = 4
S5_BLK_STATE = S5_NSTATE // S5_BLK


def _gelu_tanh(x):
    return 0.5 * x * (1.0 + jnp.tanh(math.sqrt(2.0 / math.pi) * (x + 0.044715 * (x * x * x))))


def _s5_kernel(u_ref, bre_ref, bim_ref, cre_ref, cim_ref, as_ref, pw_ref, d_ref, gw_ref,
               gb_ref, out_ref, xr_scr, xi_scr, cr_scr, ci_scr, *, rows, batch):
    ti = pl.program_id(0)

    @pl.when(ti == 0)
    def _():
        cr_scr[...] = jnp.zeros_like(cr_scr)
        ci_scr[...] = jnp.zeros_like(ci_scr)

    u = u_ref[...].astype(F32)
    ub = u.astype(BF16)
    nb = S5_BLK_STATE
    for q in range(S5_BLK):
        uq = ub[:, q * LANES:(q + 1) * LANES]
        xr_scr[:, q * nb:(q + 1) * nb] = _dot(uq, bre_ref[q])
        xi_scr[:, q * nb:(q + 1) * nb] = _dot(uq, bim_ref[q])

    rowi = lax.broadcasted_iota(jnp.int32, (SUBLANES, nb), 0)
    shifts = _s5_row_shifts(batch)
    for cc in range(S5_BLK):
        cols = slice(cc * nb, (cc + 1) * nb)

        def body(g, carry, cols=cols):
            cr, ci = carry
            for r in shifts:
                cr = jnp.where(rowi < SUBLANES - r, pltpu.roll(cr, SUBLANES - r, 0), cr)
                ci = jnp.where(rowi < SUBLANES - r, pltpu.roll(ci, SUBLANES - r, 0), ci)
            r0 = pl.multiple_of(g * SUBLANES, SUBLANES)
            zr = xr_scr[pl.ds(r0, SUBLANES), cols]
            zi = xi_scr[pl.ds(r0, SUBLANES), cols]
            for si, r in enumerate(shifts):
                sr = pltpu.roll(zr, r, 0)
                sim = pltpu.roll(zi, r, 0)
                ar = as_ref[0, si, :, cols]
                ai = as_ref[1, si, :, cols]
                zr, zi = zr + ar * sr - ai * sim, zi + ar * sim + ai * sr
            p_r = pw_ref[0, :, cols]
            p_i = pw_ref[1, :, cols]
            xr = zr + p_r * cr - p_i * ci
            xi = zi + p_r * ci + p_i * cr
            xr_scr[pl.ds(r0, SUBLANES), cols] = xr
            xi_scr[pl.ds(r0, SUBLANES), cols] = xi
            return xr, xi

        cr, ci = lax.fori_loop(0, rows // SUBLANES, body, (cr_scr[:, cols], ci_scr[:, cols]))
        cr_scr[:, cols] = cr
        ci_scr[:, cols] = ci

    ys = []
    for q in range(S5_BLK):
        xr = xr_scr[:, q * nb:(q + 1) * nb].astype(BF16)
        xi = xi_scr[:, q * nb:(q + 1) * nb].astype(BF16)
        ys.append(_dot(xr, cre_ref[q]) + _dot(xi, cim_ref[q]))
    y = jnp.concatenate(ys, axis=1) + d_ref[...] * u
    z = _gelu_tanh(y)
    gate = _sigmoid(_dot(z.astype(BF16), gw_ref[...]) + gb_ref[...])
    out_ref[...] = (z * gate).astype(out_ref.dtype)


def _s5_row_shifts(batch):
    assert SUBLANES % batch == 0
    return tuple(batch * (1 << k) for k in range(8) if batch * (1 << k) < SUBLANES)


def _s5_params(a_re, a_im, log_dt, b_re, b_im, c_re, c_im, batch):
    a_re, a_im = a_re.astype(F32), a_im.astype(F32)
    dt = jnp.exp(log_dt.astype(F32))[:, None]
    mag = jnp.exp(dt * a_re)
    ab_re, ab_im = mag * jnp.cos(dt * a_im), mag * jnp.sin(dt * a_im)
    nr, ni = ab_re - 1.0, ab_im
    den = a_re * a_re + a_im * a_im
    f_re = (nr * a_re + ni * a_im) / den
    f_im = (ni * a_re - nr * a_im) / den
    b_re, b_im = b_re.astype(F32), b_im.astype(F32)
    bb_re = f_re[..., None] * b_re - f_im[..., None] * b_im
    bb_im = f_re[..., None] * b_im + f_im[..., None] * b_re

    def apow(k):
        mk = jnp.exp(k * dt * a_re)
        return (mk * jnp.cos(k * dt * a_im)).reshape(-1), (mk * jnp.sin(k * dt * a_im)).reshape(-1)

    rows = np.arange(SUBLANES)[:, None]

    def shift_table(r, part):
        return jnp.where(jnp.asarray(rows >= r), apow(float(r // batch))[part][None, :], 0.0)

    as_arr = jnp.stack([jnp.stack([shift_table(r, part) for r in _s5_row_shifts(batch)])
                        for part in (0, 1)])
    pws = [apow(float(k // batch + 1)) for k in range(SUBLANES)]
    pw_arr = jnp.stack([jnp.stack([p[0] for p in pws]), jnp.stack([p[1] for p in pws])])

    gpb = S5_GROUPS // S5_BLK
    eye = jnp.eye(gpb, dtype=F32)

    def pack_b(bb):
        bb = bb.reshape(S5_BLK, gpb, S5_STATE, S5_CH)
        return jnp.einsum('qgpc,gh->qgchp', bb, eye).reshape(S5_BLK, gpb * S5_CH, gpb * S5_STATE)

    def pack_c(cc):
        cc = cc.reshape(S5_BLK, gpb, S5_CH, S5_STATE)
        return jnp.einsum('qgcp,gh->qgphc', cc, eye).reshape(S5_BLK, gpb * S5_STATE, gpb * S5_CH)

    return (pack_b(bb_re).astype(BF16), pack_b(bb_im).astype(BF16),
            pack_c(c_re.astype(F32)).astype(BF16), pack_c(-c_im.astype(F32)).astype(BF16),
            as_arr, pw_arr)


def _s5(u_tb, packed, d_row, glu_w, glu_b, batch, seq, out_dtype):
    rows = min(TM_S5, seq) * batch
    bre, bim, cre, cim, as_arr, pw_arr = packed
    w = BRANCH_WIDTH
    full = lambda *shape: pl.BlockSpec(shape, lambda i: (0,) * len(shape))
    return pl.pallas_call(
        functools.partial(_s5_kernel, rows=rows, batch=batch),
        grid=(seq * batch // rows,),
        in_specs=[
            pl.BlockSpec((rows, w), lambda i: (i, 0)),
            full(S5_BLK, LANES, S5_BLK_STATE), full(S5_BLK, LANES, S5_BLK_STATE),
            full(S5_BLK, S5_BLK_STATE, LANES), full(S5_BLK, S5_BLK_STATE, LANES),
            full(*as_arr.shape), full(2, SUBLANES, S5_NSTATE),
            full(1, w), full(w, w), full(1, w),
        ],
        out_specs=pl.BlockSpec((rows, w), lambda i: (i, 0)),
        out_shape=jax.ShapeDtypeStruct((seq * batch, w), out_dtype),
        scratch_shapes=[
            pltpu.VMEM((rows, S5_NSTATE), F32),
            pltpu.VMEM((rows, S5_NSTATE), F32),
            pltpu.VMEM((SUBLANES, S5_NSTATE), F32),
            pltpu.VMEM((SUBLANES, S5_NSTATE), F32),
        ],
        compiler_params=_cparams(("arbitrary",)),
        name="s5",
    )(u_tb, bre, bim, cre, cim, as_arr, pw_arr, d_row, glu_w, glu_b)


def _merge_kernel(x_ref, mod_ref, g_ref, oa_ref, ob_ref, oc_ref, od_ref, wg_ref, bg_ref,
                  wb_ref, wo_ref, out_ref, h_scr, acc_scr):
    n = pl.program_id(1)

    @pl.when(n == 0)
    def _():
        _modulated_norm_into(h_scr, x_ref, g_ref[0:1, :], mod_ref[0:1, :], mod_ref[1:2, :])
        acc_scr[...] = jnp.zeros_like(acc_scr)

    hb = h_scr[...]
    merged = None
    for i, o_ref in enumerate((oa_ref, ob_ref, oc_ref, od_ref)):
        gate = _sigmoid(_dot(hb, wg_ref[i]) + bg_ref[i])
        term = gate * _dot(o_ref[...].astype(BF16), wb_ref[i])
        merged = term if merged is None else merged + term
    acc_scr[...] += _dot(merged.astype(BF16), wo_ref[...])

    @pl.when(n == pl.num_programs(1) - 1)
    def _():
        _gated_residual_into(out_ref, x_ref, acc_scr, g_ref[1:2, :], mod_ref[2:3, :])


def _merge(x2, mod_l, gains, oa, ob, oc, od, w_gate, b_gate, w_branch, w_out, seq):
    t, d = x2.shape
    tm, tn = min(TM_MERGE, seq), TN_MERGE
    tiles_per_seq = seq // tm
    w = BRANCH_WIDTH
    br_spec = pl.BlockSpec((tm, w), lambda i, n: (i, 0))
    return pl.pallas_call(
        _merge_kernel,
        grid=(t // tm, d // tn),
        in_specs=[
            pl.BlockSpec((tm, d), lambda i, n: (i, 0)),
            pl.BlockSpec((None, SUBLANES, d), lambda i, n: (i // tiles_per_seq, 0, 0)),
            pl.BlockSpec((2, d), lambda i, n: (0, 0)),
            br_spec, br_spec, br_spec,
            pl.BlockSpec((tm, w), lambda i, n: (i % tiles_per_seq, i // tiles_per_seq)),
            pl.BlockSpec((N_BRANCH, d, tn), lambda i, n: (0, 0, n)),
            pl.BlockSpec((N_BRANCH, 1, tn), lambda i, n: (0, 0, n)),
            pl.BlockSpec((N_BRANCH, w, tn), lambda i, n: (0, 0, n)),
            pl.BlockSpec((tn, d), lambda i, n: (n, 0)),
        ],
        out_specs=pl.BlockSpec((tm, d), lambda i, n: (i, 0)),
        out_shape=jax.ShapeDtypeStruct((t, d), F32),
        scratch_shapes=[pltpu.VMEM((tm, d), BF16), pltpu.VMEM((tm, d), F32)],
        compiler_params=_cparams(("parallel", "arbitrary")),
        name="merge",
    )(x2, mod_l, gains, oa, ob, oc, od, w_gate, b_gate, w_branch, w_out)


def _ffn_kernel(x_ref, mod_ref, g_ref, wa_ref, wg_ref, wo_ref, out_ref, h_scr, acc_scr):
    j = pl.program_id(1)

    @pl.when(j == 0)
    def _():
        _modulated_norm_into(h_scr, x_ref, g_ref[0:1, :], mod_ref[3:4, :], mod_ref[4:5, :])
        acc_scr[...] = jnp.zeros_like(acc_scr)

    hb = h_scr[...]
    a = _dot(hb, wa_ref[...])
    g = _dot(hb, wg_ref[...])
    acc_scr[...] += _dot((_silu(a) * g).astype(BF16), wo_ref[...])

    @pl.when(j == pl.num_programs(1) - 1)
    def _():
        _gated_residual_into(out_ref, x_ref, acc_scr, g_ref[1:2, :], mod_ref[5:6, :])


def _ffn(x2, mod_l, gains, w_in, w_out, seq):
    t, d = x2.shape
    tm, th = min(TM_FFN, seq), TH_FFN
    tiles_per_seq = seq // tm
    nh = FFN_HIDDEN // th
    return pl.pallas_call(
        _ffn_kernel,
        grid=(t // tm, nh),
        in_specs=[
            pl.BlockSpec((tm, d), lambda i, j: (i, 0)),
            pl.BlockSpec((None, SUBLANES, d), lambda i, j: (i // tiles_per_seq, 0, 0)),
            pl.BlockSpec((2, d), lambda i, j: (0, 0)),
            pl.BlockSpec((d, th), lambda i, j: (0, j)),
            pl.BlockSpec((d, th), lambda i, j: (0, j + nh)),
            pl.BlockSpec((th, d), lambda i, j: (j, 0)),
        ],
        out_specs=pl.BlockSpec((tm, d), lambda i, j: (i, 0)),
        out_shape=jax.ShapeDtypeStruct((t, d), F32),
        scratch_shapes=[pltpu.VMEM((tm, d), BF16), pltpu.VMEM((tm, d), F32)],
        compiler_params=_cparams(("parallel", "arbitrary")),
        name="ffn",
    )(x2, mod_l, gains, w_in, w_in, w_out)


def _split_w_in(w):
    w = w.astype(BF16)
    d = w.shape[0]
    main = jnp.concatenate([w[:, :3584], w[:, 3592:5128], w[:, 5144:5656]], axis=1)
    small = jnp.concatenate([w[:, 3584:3592], w[:, 5128:5144],
                             jnp.zeros((d, SMALL_W - 2 * ML_HEADS - GLA_RANK), w.dtype)], axis=1)
    return main, small


def _pad_row(vals, offset):
    row = jnp.zeros((1, SMALL_W), F32)
    return lax.dynamic_update_slice(row, vals.reshape(1, -1).astype(F32), (0, offset))


ACT_DTYPE = BF16


def kernel(x, c, ada_w, ada_b, norm_g, w_in, rel_bias, diff_lambda, ml_conv, ml_gate_b,
           gla_wa2, gla_ba, s5_a_re, s5_a_im, s5_log_dt, s5_b_re, s5_b_im, s5_c_re, s5_c_im,
           s5_d, s5_glu_w, s5_glu_b, w_branch, w_gate, b_gate, w_out, ffn_w_in, ffn_w_out):
    batch, seq, d = x.shape
    depth = ada_w.shape[0]
    t = batch * seq

    c_pad = jnp.concatenate([c, jnp.zeros((SUBLANES - batch, d), c.dtype)], axis=0)
    mod = _adaln(c_pad, ada_w, ada_b)[:, :batch]
    mod = mod.reshape(depth, batch, N_MOD, d)
    mod = jnp.concatenate([mod, jnp.zeros((depth, batch, SUBLANES - N_MOD, d), F32)], axis=2)

    bias_tiles = _bias_tiles(rel_bias, min(TQ_ATT, seq))

    x2 = x.reshape(t, d)
    for l in range(depth):
        w_main, w_small = _split_w_in(w_in[l])
        pm, ps, pu = _proj(x2, mod[l], norm_g[l, 0:1], w_main, w_small, seq, ACT_DTYPE)

        lam_init = 0.8 - 0.6 * math.exp(-0.3 * l)
        lp = diff_lambda[l].astype(F32)
        lam = (jnp.exp(jnp.sum(lp[0] * lp[1])) - jnp.exp(jnp.sum(lp[2] * lp[3])) + lam_init)
        o_a = _attention(lam.reshape(1), pm, bias_tiles, batch, seq, lam_init, BF16)

        gate_row = (_pad_row(ml_gate_b[l, 0], SM_ML_I) + _pad_row(ml_gate_b[l, 1], SM_ML_F))
        o_b = _mlstm(pm, ps, ml_conv[l].astype(F32), gate_row, batch, seq, BF16)

        wa_pad = jnp.zeros((SMALL_W, GLA_HEADS * GLA_DK), F32)
        wa_pad = lax.dynamic_update_slice(wa_pad, gla_wa2[l].astype(F32), (SM_GL_A, 0))
        o_c = _gla(pm, ps, wa_pad, gla_ba[l].reshape(1, -1).astype(F32), batch, seq, BF16)

        packed = _s5_params(s5_a_re[l], s5_a_im[l], s5_log_dt[l], s5_b_re[l], s5_b_im[l],
                            s5_c_re[l], s5_c_im[l], batch)
        o_d = _s5(pu.reshape(seq * batch, BRANCH_WIDTH), packed, s5_d[l].reshape(1, -1).astype(F32),
                  s5_glu_w[l].astype(BF16), s5_glu_b[l].reshape(1, -1).astype(F32), batch, seq, BF16)
        o_d = o_d.reshape(seq, batch * BRANCH_WIDTH)

        x2 = _merge(x2, mod[l], norm_g[l, 0:2], o_a, o_b, o_c, o_d,
                    w_gate[l].astype(BF16), b_gate[l].reshape(N_BRANCH, 1, d).astype(F32),
                    w_branch[l].astype(BF16), w_out[l].astype(BF16), seq)
        x2 = _ffn(x2, mod[l], norm_g[l, 2:4], ffn_w_in[l].astype(BF16),
                  ffn_w_out[l].astype(BF16), seq)
    return x2.reshape(batch, seq, d)
```

```python
import functools
import math

import numpy as np
import jax
import jax.numpy as jnp
from jax import lax
from jax.experimental import pallas as pl
from jax.experimental.pallas import tpu as pltpu

F32 = jnp.float32
BF16 = jnp.bfloat16
HIGHEST = lax.Precision.HIGHEST

D_MODEL = 2048
DEPTH = 4
EPS = 1e-6
N_MOD = 6
N_BRANCH = 4
BRANCH_WIDTH = 512
DA_HEADS = 4
DA_QK_DIM = 64
DA_V_DIM = 128
N_BUCKETS = 32
MAX_DISTANCE = 128
ML_HEADS = 4
ML_DIM = 128
ML_CONV = 4
GLA_HEADS = 4
GLA_DK = 64
GLA_DV = 128
GLA_RANK = 16
GLA_TAU = 16.0
S5_CH = 16
S5_GROUPS = BRANCH_WIDTH // S5_CH
S5_STATE = 64
FFN_HIDDEN = -(-(8 * D_MODEL) // (3 * 256)) * 256

LANES = 128
SUBLANES = 8
BF16_ROWS = 16
VMEM_LIMIT = 56 * 1024 * 1024

MAIN_W = 5632
SMALL_W = LANES
OFF_DA_Q, OFF_DA_K, OFF_DA_V = 0, 512, 1024
OFF_ML_Q, OFF_ML_K, OFF_ML_V, OFF_ML_O = 1536, 2048, 2560, 3072
OFF_GL_Q, OFF_GL_K, OFF_GL_V, OFF_GL_R = 3584, 3840, 4096, 4608
OFF_S5_U = 5120
SM_ML_I, SM_ML_F, SM_GL_A = 0, 4, 8

NEG = -1e30
LOG2E = math.log2(math.e)

TM_PROJ, TN_PROJ = 1024, 512
TQ_ATT = 512
L_MLSTM = 256
L_GLA, C_GLA = 256, 16
TM_S5 = 256
TM_MERGE, TN_MERGE = 512, 256
TM_FFN, TH_FFN = 1024, 256


def _cparams(sem):
    return pltpu.CompilerParams(dimension_semantics=sem, vmem_limit_bytes=VMEM_LIMIT)


def _rms(x):
    return x * lax.rsqrt(jnp.mean(x * x, axis=-1, keepdims=True) + EPS)


def _sigmoid(x):
    return 1.0 / (1.0 + jnp.exp(-x))


def _silu(x):
    return x * _sigmoid(x)


def _log_sigmoid(x):
    return jnp.minimum(x, 0.0) - jnp.log1p(jnp.exp(-jnp.abs(x)))


def _dot(a, b):
    return jnp.dot(a, b, preferred_element_type=F32)


def _dot_nt(a, b):
    return lax.dot_general(a, b, (((1,), (1,)), ((), ())), preferred_element_type=F32)


def _dot_tn(a, b):
    return lax.dot_general(a, b, (((0,), (0,)), ((), ())), preferred_element_type=F32)


def _dot_exact(a, b):
    return jnp.dot(a, b, preferred_element_type=F32, precision=HIGHEST)


def _adaln_kernel(c_ref, w_ref, b_ref, o_ref):
    c = c_ref[...]
    o_ref[...] = _dot(_silu(c).astype(BF16), w_ref[...].astype(BF16)) + b_ref[...]


def _adaln(c_pad, ada_w, ada_b):
    depth, d, n = ada_w.shape
    rows = c_pad.shape[0]
    tn = 2048
    return pl.pallas_call(
        _adaln_kernel,
        grid=(depth, n // tn),
        in_specs=[
            pl.BlockSpec((rows, d), lambda l, j: (0, 0)),
            pl.BlockSpec((None, d, tn), lambda l, j: (l, 0, j)),
            pl.BlockSpec((None, 1, tn), lambda l, j: (l, 0, j)),
        ],
        out_specs=pl.BlockSpec((None, rows, tn), lambda l, j: (l, 0, j)),
        out_shape=jax.ShapeDtypeStruct((depth, rows, n), F32),
        compiler_params=_cparams(("parallel", "parallel")),
        name="adaln",
    )(c_pad, ada_w, ada_b.reshape(depth, 1, n))


ROW_CHUNK = 32


def _for_row_chunks(n_rows, fn):
    def body(c, carry):
        fn(pl.ds(pl.multiple_of(c * ROW_CHUNK, ROW_CHUNK), ROW_CHUNK))
        return carry

    lax.fori_loop(0, n_rows // ROW_CHUNK, body, 0, unroll=4)


def _modulated_norm_into(h_scr, x_ref, gain, shift, scale):
    gs = gain * (1.0 + scale)

    def chunk(rows):
        h_scr[rows, :] = (_rms(x_ref[rows, :]) * gs + shift).astype(h_scr.dtype)

    _for_row_chunks(h_scr.shape[0], chunk)


def _gated_residual_into(out_ref, x_ref, acc_scr, gain, gate):
    gg = gain * gate

    def chunk(rows):
        out_ref[rows, :] = x_ref[rows, :] + _rms(acc_scr[rows, :]) * gg

    _for_row_chunks(out_ref.shape[0], chunk)


def _proj_kernel(x_ref, mod_ref, g_ref, w_ref, ws_ref, o_ref, os_ref, ou_ref, h_scr,
                 *, j_s5, off_s5):
    j = pl.program_id(1)

    @pl.when(j == 0)
    def _():
        gs = g_ref[...] * (1.0 + mod_ref[1:2, :])
        hb = (_rms(x_ref[...]) * gs + mod_ref[0:1, :]).astype(BF16)
        h_scr[...] = hb
        os_ref[...] = _dot(hb, ws_ref[...])

    res = _dot(h_scr[...], w_ref[...]).astype(o_ref.dtype)
    o_ref[...] = res

    @pl.when(j == j_s5)
    def _():
        ou_ref[...] = res[:, off_s5:off_s5 + BRANCH_WIDTH]


def _proj(x2, mod_l, gain, w_main, w_small, seq, out_dtype):
    t, d = x2.shape
    tm, tn = min(TM_PROJ, seq), TN_PROJ
    tiles_per_seq = seq // tm
    w = BRANCH_WIDTH
    j_s5, off_s5 = divmod(OFF_S5_U, tn)
    assert MAIN_W % tn == 0 and off_s5 % LANES == 0 and off_s5 + w <= tn
    return pl.pallas_call(
        functools.partial(_proj_kernel, j_s5=j_s5, off_s5=off_s5),
        grid=(t // tm, MAIN_W // tn),
        in_specs=[
            pl.BlockSpec((tm, d), lambda i, j: (i, 0)),
            pl.BlockSpec((None, SUBLANES, d), lambda i, j: (i // tiles_per_seq, 0, 0)),
            pl.BlockSpec((1, d), lambda i, j: (0, 0)),
            pl.BlockSpec((d, tn), lambda i, j: (0, j)),
            pl.BlockSpec((d, SMALL_W), lambda i, j: (0, 0)),
        ],
        out_specs=[
            pl.BlockSpec((tm, tn), lambda i, j: (i, j)),
            pl.BlockSpec((tm, SMALL_W), lambda i, j: (i, 0)),
            pl.BlockSpec((tm, w), lambda i, j: (i % tiles_per_seq, i // tiles_per_seq)),
        ],
        out_shape=[
            jax.ShapeDtypeStruct((t, MAIN_W), out_dtype),
            jax.ShapeDtypeStruct((t, SMALL_W), F32),
            jax.ShapeDtypeStruct((seq, (t // seq) * w), out_dtype),
        ],
        scratch_shapes=[pltpu.VMEM((tm, d), BF16)],
        compiler_params=_cparams(("parallel", "arbitrary")),
        name="proj",
    )(x2, mod_l, gain, w_main, w_small)


def _attn_kernel(lam_ref, q_ref, k_ref, v_ref, bias_ref, o_ref, m_scr, l_scr, acc_scr,
                 s0_scr, s1_scr, *, tq, out_scale):
    i = pl.program_id(2)
    lam = lam_ref[0]
    q = q_ref[...].astype(F32) * (DA_QK_DIM ** -0.5 * LOG2E)
    lane = lax.broadcasted_iota(jnp.int32, q.shape, 1)
    qa = jnp.where(lane < DA_QK_DIM, q, 0.0).astype(BF16)
    qb = jnp.where(lane >= DA_QK_DIM, q, 0.0).astype(BF16)
    q2 = jnp.concatenate([qa, qb], axis=0)

    m_scr[...] = jnp.full_like(m_scr, NEG)
    l_scr[...] = jnp.zeros_like(l_scr)
    acc_scr[...] = jnp.zeros_like(acc_scr)

    def scores(j):
        r0 = pl.multiple_of(j * tq, tq)
        return _dot_nt(k_ref[pl.ds(r0, tq), :].astype(BF16), q2)

    def accumulate(j, s):
        r0 = pl.multiple_of(j * tq, tq)
        vt = v_ref[pl.ds(r0, tq), :].astype(BF16)
        m_old = m_scr[...]
        m_new = jnp.maximum(m_old, jnp.max(s, axis=0, keepdims=True))
        p = jnp.exp2(s - m_new)
        alpha = jnp.exp2(m_old - m_new)
        l_scr[...] = alpha * l_scr[...] + jnp.sum(p, axis=0, keepdims=True)
        acc_scr[...] = alpha * acc_scr[...] + _dot_tn(vt, p.astype(BF16))
        m_scr[...] = m_new

    def biased_scores(j, bias):
        return scores(j) + jnp.concatenate([bias, bias], axis=1)

    n_far = jnp.maximum(i - 1, 0)
    j_prev = jnp.maximum(i - 1, 0)
    s0_scr[...] = biased_scores(i, bias_ref[1])
    s1_scr[...] = biased_scores(j_prev, bias_ref[jnp.where(i >= 1, 0, 2)])
    accumulate(i, s0_scr[...])
    s0_scr[...] = scores(0)
    accumulate(j_prev, s1_scr[...])

    def far_pair(base):
        s1_scr[...] = scores(base + 1)
        accumulate(base, s0_scr[...])
        s0_scr[...] = scores(jnp.minimum(base + 2, n_far - 1))
        accumulate(base + 1, s1_scr[...])

    def far_quad(g, carry):
        far_pair(4 * g)
        far_pair(4 * g + 2)
        return carry

    quads = n_far // 4
    lax.fori_loop(0, quads, far_quad, 0)
    rest = n_far - 4 * quads

    @pl.when(rest >= 2)
    def _():
        far_pair(4 * quads)

    @pl.when(rest % 2 == 1)
    def _():
        accumulate(n_far - 1, s0_scr[...])

    on = acc_scr[...] / l_scr[...]
    ot = on[:, 0:tq] - lam * on[:, tq:2 * tq]
    ot = ot * (lax.rsqrt(jnp.mean(ot * ot, axis=0, keepdims=True) + EPS) * out_scale)
    o_ref[...] = ot.T.astype(o_ref.dtype)


def _attention(lam, pm, bias_tiles, batch, seq, lam_init, out_dtype):
    tq = min(TQ_ATT, seq)
    nq = seq // tq
    kern = functools.partial(_attn_kernel, tq=tq, out_scale=1.0 - lam_init)
    scratch = [pltpu.VMEM((1, 2 * tq), F32), pltpu.VMEM((1, 2 * tq), F32),
               pltpu.VMEM((DA_V_DIM, 2 * tq), F32),
               pltpu.VMEM((tq, 2 * tq), F32), pltpu.VMEM((tq, 2 * tq), F32)]
    qb, kb, vb = OFF_DA_Q // LANES, OFF_DA_K // LANES, OFF_DA_V // LANES
    return pl.pallas_call(
        kern,
        grid=(batch, DA_HEADS, nq),
        in_specs=[
            pl.BlockSpec(memory_space=pltpu.SMEM),
            pl.BlockSpec((tq, LANES), lambda b, h, i: (b * nq + i, qb + h)),
            pl.BlockSpec((seq, LANES), lambda b, h, i: (b, kb + h)),
            pl.BlockSpec((seq, LANES), lambda b, h, i: (b, vb + h)),
            pl.BlockSpec((None, 3, tq, tq), lambda b, h, i: (h, 0, 0, 0)),
        ],
        out_specs=pl.BlockSpec((tq, LANES), lambda b, h, i: (b * nq + i, h)),
        out_shape=jax.ShapeDtypeStruct((batch * seq, DA_HEADS * DA_V_DIM), out_dtype),
        scratch_shapes=scratch,
        compiler_params=_cparams(("parallel", "parallel", "arbitrary")),
        name="diff_attn",
    )(lam, pm, pm, pm, bias_tiles)


def _t5_bucket_table(n_max):
    n = np.arange(n_max)
    exact = N_BUCKETS // 2
    nf = np.maximum(n, 1).astype(np.float64)
    large = exact + (np.log(nf / exact) / math.log(MAX_DISTANCE / exact)
                     * (N_BUCKETS - exact)).astype(np.int64)
    return np.where(n < exact, n, np.minimum(large, N_BUCKETS - 1)).astype(np.int32)


def _bias_tiles(rel_bias, tq):
    assert tq >= MAX_DISTANCE
    n = tq
    heads = rel_bias.shape[1]
    rb = rel_bias.astype(F32)
    near = (rb[_t5_bucket_table(MAX_DISTANCE)] - rb[N_BUCKETS - 1][None, :]) * LOG2E
    f = jnp.concatenate([near, jnp.zeros((2 * n - MAX_DISTANCE, heads), F32)], axis=0)

    def toeplitz(v):
        vp = jnp.concatenate([v, jnp.zeros((1, heads), F32)], axis=0)
        flat = jnp.tile(vp, (n, 1))[: n * (2 * n - 1)]
        return jnp.transpose(flat.reshape(n, 2 * n - 1, heads)[:, n - 1:, :], (2, 0, 1))

    prev = toeplitz(f[1:2 * n])
    diag = toeplitz(jnp.concatenate([jnp.full((n - 1, heads), NEG, F32), f[0:n]], axis=0))
    masked = jnp.full_like(prev, NEG)
    return jnp.stack([prev, diag, masked], axis=1)


def _mlstm_kernel(q_ref, k_ref, v_ref, og_ref, sm_ref, qh_ref, kh_ref, cw_ref, gb_ref,
                  tril_ref, out_ref, xq_scr, xk_scr, c_scr, n_scr, m_scr, *, L):
    ci = pl.program_id(1)

    @pl.when(ci == 0)
    def _():
        c_scr[...] = jnp.zeros_like(c_scr)
        n_scr[...] = jnp.zeros_like(n_scr)
        m_scr[...] = jnp.zeros_like(m_scr)

    halo = BF16_ROWS
    keep = (ci > 0).astype(F32)
    xq_scr[0:halo, :] = qh_ref[...].astype(F32) * keep
    xk_scr[0:halo, :] = kh_ref[...].astype(F32) * keep
    xq_scr[halo:halo + L, :] = q_ref[...].astype(F32)
    xk_scr[halo:halo + L, :] = k_ref[...].astype(F32)

    def conv_silu(scr, w):
        y = scr[halo:halo + L, :] * w[ML_CONV - 1:ML_CONV, :]
        for j in range(ML_CONV - 1):
            off = halo - (ML_CONV - 1) + j
            y = y + scr[off:off + L, :] * w[j:j + 1, :]
        return _silu(y)

    w_all = cw_ref[...]
    width = ML_HEADS * ML_DIM
    q_all = conv_silu(xq_scr, w_all[:, 0:width])
    k_all = conv_silu(xk_scr, w_all[:, width:2 * width]) * (ML_DIM ** -0.5)

    g = sm_ref[...] + gb_ref[...]
    lane = lax.broadcasted_iota(jnp.int32, g.shape, 1)
    is_f = (lane >= SM_ML_F) & (lane < SM_ML_F + ML_HEADS)
    g = jnp.where(is_f, _log_sigmoid(g), g)
    bcum = _dot_exact(tril_ref[...], g)
    g_t = g.T
    b_t = bcum.T
    row = lax.broadcasted_iota(jnp.int32, (L, L), 0)
    colj = lax.broadcasted_iota(jnp.int32, (L, L), 1)
    causal = colj <= row

    for h in range(ML_HEADS):
        sl = slice(h * ML_DIM, (h + 1) * ML_DIM)
        qh = q_all[:, sl]
        kh = k_all[:, sl]
        vh = v_ref[:, sl].astype(F32)
        qb, kb, vb = qh.astype(BF16), kh.astype(BF16), vh.astype(BF16)
        ig_col = g[:, SM_ML_I + h:SM_ML_I + h + 1]
        b_col = bcum[:, SM_ML_F + h:SM_ML_F + h + 1]
        ig_row = g_t[SM_ML_I + h:SM_ML_I + h + 1, :]
        b_row = b_t[SM_ML_F + h:SM_ML_F + h + 1, :]
        m_old = m_scr[h][:, 0:1]
        c_old = c_scr[h]
        n_old = n_scr[h]

        dm = jnp.where(causal, b_col - b_row + ig_row, NEG)
        inter = b_col + m_old
        m_t = jnp.maximum(inter, jnp.max(dm, axis=-1, keepdims=True))
        s = _dot_nt(qb, kb) * jnp.exp(dm - m_t)
        a = jnp.exp(inter - m_t)
        num = a * _dot(qb, c_old.astype(BF16)) + _dot(s.astype(BF16), vb)
        den = (a * jnp.sum(qh * n_old, axis=-1, keepdims=True)
               + jnp.sum(s, axis=-1, keepdims=True))
        hv = num / jnp.maximum(jnp.abs(den), jnp.exp(-m_t))
        out_ref[:, sl] = (_sigmoid(og_ref[:, sl].astype(F32)) * hv).astype(out_ref.dtype)

        m_new = m_t[L - 1:L, :]
        b_last = b_col[L - 1:L, :]
        a_state = jnp.exp(b_last + m_old - m_new)
        w_col = jnp.exp(b_last - b_col + ig_col - m_new)
        kw = kh * w_col
        c_scr[h] = a_state * c_old + _dot_tn(kw.astype(BF16), vb)
        n_scr[h] = a_state * n_old + jnp.sum(kw, axis=0, keepdims=True)
        m_scr[h] = jnp.broadcast_to(m_new, (1, LANES))


def _mlstm(pm, ps, conv_w, gate_row, batch, seq, out_dtype):
    L = min(L_MLSTM, seq)
    nc = seq // L
    width = ML_HEADS * ML_DIM
    qb, kb, vb, ob = (OFF_ML_Q // width, OFF_ML_K // width, OFF_ML_V // width, OFF_ML_O // width)
    lb = L // BF16_ROWS
    tril = jnp.asarray(np.tril(np.ones((L, L), np.float32)))

    def halo_map(colblk):
        return lambda b, c: (jnp.maximum(b * (seq // BF16_ROWS) + c * lb - 1, 0), colblk)

    return pl.pallas_call(
        functools.partial(_mlstm_kernel, L=L),
        grid=(batch, nc),
        in_specs=[
            pl.BlockSpec((L, width), lambda b, c: (b * nc + c, qb)),
            pl.BlockSpec((L, width), lambda b, c: (b * nc + c, kb)),
            pl.BlockSpec((L, width), lambda b, c: (b * nc + c, vb)),
            pl.BlockSpec((L, width), lambda b, c: (b * nc + c, ob)),
            pl.BlockSpec((L, SMALL_W), lambda b, c: (b * nc + c, 0)),
            pl.BlockSpec((BF16_ROWS, width), halo_map(qb)),
            pl.BlockSpec((BF16_ROWS, width), halo_map(kb)),
            pl.BlockSpec((ML_CONV, 2 * width), lambda b, c: (0, 0)),
            pl.BlockSpec((1, SMALL_W), lambda b, c: (0, 0)),
            pl.BlockSpec((L, L), lambda b, c: (0, 0)),
        ],
        out_specs=pl.BlockSpec((L, width), lambda b, c: (b * nc + c, 0)),
        out_shape=jax.ShapeDtypeStruct((batch * seq, width), out_dtype),
        scratch_shapes=[
            pltpu.VMEM((L + BF16_ROWS, width), F32),
            pltpu.VMEM((L + BF16_ROWS, width), F32),
            pltpu.VMEM((ML_HEADS, ML_DIM, ML_DIM), F32),
            pltpu.VMEM((ML_HEADS, 1, ML_DIM), F32),
            pltpu.VMEM((ML_HEADS, 1, LANES), F32),
        ],
        compiler_params=_cparams(("parallel", "arbitrary")),
        name="mlstm",
    )(pm, pm, pm, pm, ps, pm, pm, conv_w, gate_row, tril)


def _gla_kernel(q_ref, k_ref, v_ref, r_ref, sm_ref, wa_ref, ba_ref, tril_ref, mexp_ref,
                out_ref, bc_scr, a_scr, st_scr, *, L, c):
    ci = pl.program_id(1)

    @pl.when(ci == 0)
    def _():
        st_scr[...] = jnp.zeros_like(st_scr)

    la = _log_sigmoid(_dot_exact(sm_ref[...], wa_ref[...]) + ba_ref[...]) * (1.0 / GLA_TAU)
    bc_scr[...] = _dot_exact(tril_ref[...], la)
    kw = GLA_HEADS * GLA_DK
    rowc = lax.broadcasted_iota(jnp.int32, (c, kw), 0)
    lane_head = lax.broadcasted_iota(jnp.int32, (c, kw), 1) // GLA_DK
    head_masks = [lane_head == h for h in range(GLA_HEADS)]

    def stack_heads(x):
        return jnp.concatenate([jnp.where(mk, x, 0.0) for mk in head_masks], axis=0)

    def sub(i, carry):
        r0 = pl.multiple_of(i * c, c)
        qs = q_ref[pl.ds(r0, c), :].astype(F32) * (GLA_DK ** -0.5)
        ks = k_ref[pl.ds(r0, c), :].astype(F32)
        vs = v_ref[pl.ds(r0, c), :].astype(BF16).astype(F32)
        bcs = bc_scr[pl.ds(r0, c), :]
        e_end = bcs[c - 1:c, :]
        st = st_scr[...]
        o_stack = _dot_nt(stack_heads(qs * jnp.exp(bcs)).astype(BF16), st.astype(BF16))
        o = jnp.concatenate([o_stack[h * c:(h + 1) * c] for h in range(GLA_HEADS)], axis=1)

        for t in range(c):
            dec = jnp.exp(jnp.minimum(bcs[t:t + 1, :] - bcs, 0.0))
            a_t = jnp.where(rowc <= t, qs[t:t + 1, :] * ks * dec, 0.0)
            a_scr[t * c:(t + 1) * c, :] = a_t.astype(BF16)
        p = _dot(a_scr[...], mexp_ref[...])
        o = o + jnp.sum(p.reshape(c, c, GLA_HEADS * GLA_DV) * vs[None, :, :], axis=1)

        outs = [_rms(o[:, h * GLA_DV:(h + 1) * GLA_DV]) for h in range(GLA_HEADS)]
        on = jnp.concatenate(outs, axis=1)
        out_ref[pl.ds(r0, c), :] = (on * _silu(r_ref[pl.ds(r0, c), :].astype(F32))).astype(out_ref.dtype)

        khat = stack_heads(ks * jnp.exp(e_end - bcs)).astype(BF16)
        v_stack = jnp.concatenate([vs[:, h * GLA_DV:(h + 1) * GLA_DV] for h in range(GLA_HEADS)],
                                  axis=0).astype(BF16)
        st_scr[...] = st * jnp.exp(e_end) + _dot_tn(v_stack, khat)
        return carry

    lax.fori_loop(0, L // c, sub, 0, unroll=8)


def _gla(pm, ps, wa_pad, ba_row, batch, seq, out_dtype):
    L = min(L_GLA, seq)
    c = C_GLA
    nc = seq // L
    kw, vw = GLA_HEADS * GLA_DK, GLA_HEADS * GLA_DV
    qb, kb, vb, rb = OFF_GL_Q // kw, OFF_GL_K // kw, OFF_GL_V // vw, OFF_GL_R // vw
    idx = np.arange(L)
    tril = ((idx[:, None] >= idx[None, :]) & (idx[:, None] // c == idx[None, :] // c))
    tril = jnp.asarray(tril.astype(np.float32))
    mexp = np.zeros((kw, vw), np.float32)
    for h in range(GLA_HEADS):
        mexp[h * GLA_DK:(h + 1) * GLA_DK, h * GLA_DV:(h + 1) * GLA_DV] = 1.0
    mexp = jnp.asarray(mexp, dtype=BF16)
    return pl.pallas_call(
        functools.partial(_gla_kernel, L=L, c=c),
        grid=(batch, nc),
        in_specs=[
            pl.BlockSpec((L, kw), lambda b, i: (b * nc + i, qb)),
            pl.BlockSpec((L, kw), lambda b, i: (b * nc + i, kb)),
            pl.BlockSpec((L, vw), lambda b, i: (b * nc + i, vb)),
            pl.BlockSpec((L, vw), lambda b, i: (b * nc + i, rb)),
            pl.BlockSpec((L, SMALL_W), lambda b, i: (b * nc + i, 0)),
            pl.BlockSpec((SMALL_W, kw), lambda b, i: (0, 0)),
            pl.BlockSpec((1, kw), lambda b, i: (0, 0)),
            pl.BlockSpec((L, L), lambda b, i: (0, 0)),
            pl.BlockSpec((kw, vw), lambda b, i: (0, 0)),
        ],
        out_specs=pl.BlockSpec((L, vw), lambda b, i: (b * nc + i, 0)),
        out_shape=jax.ShapeDtypeStruct((batch * seq, vw), out_dtype),
        scratch_shapes=[
            pltpu.VMEM((L, kw), F32),
            pltpu.VMEM((c * c, kw), BF16),
            pltpu.VMEM((GLA_DV, kw), F32),
        ],
        compiler_params=_cparams(("parallel", "arbitrary")),
        name="gla",
    )(pm, pm, pm, pm, ps, wa_pad, ba_row, tril, mexp)


S5_NSTATE = S5_GROUPS * S5_STATE
S5_BLK = 4
S5_BLK_STATE = S5_NSTATE // S5_BLK


def _gelu_tanh(x):
    return 0.5 * x * (1.0 + jnp.tanh(math.sqrt(2.0 / math.pi) * (x + 0.044715 * (x * x * x))))


def _s5_kernel(u_ref, bre_ref, bim_ref, cre_ref, cim_ref, as_ref, pw_ref, d_ref, gw_ref,
               gb_ref, out_ref, xr_scr, xi_scr, cr_scr, ci_scr, *, rows, batch):
    ti = pl.program_id(0)

    @pl.when(ti == 0)
    def _():
        cr_scr[...] = jnp.zeros_like(cr_scr)
        ci_scr[...] = jnp.zeros_like(ci_scr)

    u = u_ref[...].astype(F32)
    ub = u.astype(BF16)
    nb = S5_BLK_STATE
    for q in range(S5_BLK):
        uq = ub[:, q * LANES:(q + 1) * LANES]
        xr_scr[:, q * nb:(q + 1) * nb] = _dot(uq, bre_ref[q])
        xi_scr[:, q * nb:(q + 1) * nb] = _dot(uq, bim_ref[q])

    rowi = lax.broadcasted_iota(jnp.int32, (SUBLANES, nb), 0)
    shifts = _s5_row_shifts(batch)
    for cc in range(S5_BLK):
        cols = slice(cc * nb, (cc + 1) * nb)

        def body(g, carry, cols=cols):
            cr, ci = carry
            for r in shifts:
                cr = jnp.where(rowi < SUBLANES - r, pltpu.roll(cr, SUBLANES - r, 0), cr)
                ci = jnp.where(rowi < SUBLANES - r, pltpu.roll(ci, SUBLANES - r, 0), ci)
            r0 = pl.multiple_of(g * SUBLANES, SUBLANES)
            zr = xr_scr[pl.ds(r0, SUBLANES), cols]
            zi = xi_scr[pl.ds(r0, SUBLANES), cols]
            for si, r in enumerate(shifts):
                sr = pltpu.roll(zr, r, 0)
                sim = pltpu.roll(zi, r, 0)
                ar = as_ref[0, si, :, cols]
                ai = as_ref[1, si, :, cols]
                zr, zi = zr + ar * sr - ai * sim, zi + ar * sim + ai * sr
            p_r = pw_ref[0, :, cols]
            p_i = pw_ref[1, :, cols]
            xr = zr + p_r * cr - p_i * ci
            xi = zi + p_r * ci + p_i * cr
            xr_scr[pl.ds(r0, SUBLANES), cols] = xr
            xi_scr[pl.ds(r0, SUBLANES), cols] = xi
            return xr, xi

        cr, ci = lax.fori_loop(0, rows // SUBLANES, body, (cr_scr[:, cols], ci_scr[:, cols]))
        cr_scr[:, cols] = cr
        ci_scr[:, cols] = ci

    ys = []
    for q in range(S5_BLK):
        xr = xr_scr[:, q * nb:(q + 1) * nb].astype(BF16)
        xi = xi_scr[:, q * nb:(q + 1) * nb].astype(BF16)
        ys.append(_dot(xr, cre_ref[q]) + _dot(xi, cim_ref[q]))
    y = jnp.concatenate(ys, axis=1) + d_ref[...] * u
    z = _gelu_tanh(y)
    gate = _sigmoid(_dot(z.astype(BF16), gw_ref[...]) + gb_ref[...])
    out_ref[...] = (z * gate).astype(out_ref.dtype)


def _s5_row_shifts(batch):
    assert SUBLANES % batch == 0
    return tuple(batch * (1 << k) for k in range(8) if batch * (1 << k) < SUBLANES)


def _s5_params(a_re, a_im, log_dt, b_re, b_im, c_re, c_im, batch):
    a_re, a_im = a_re.astype(F32), a_im.astype(F32)
    dt = jnp.exp(log_dt.astype(F32))[:, None]
    mag = jnp.exp(dt * a_re)
    ab_re, ab_im = mag * jnp.cos(dt * a_im), mag * jnp.sin(dt * a_im)
    nr, ni = ab_re - 1.0, ab_im
    den = a_re * a_re + a_im * a_im
    f_re = (nr * a_re + ni * a_im) / den
    f_im = (ni * a_re - nr * a_im) / den
    b_re, b_im = b_re.astype(F32), b_im.astype(F32)
    bb_re = f_re[..., None] * b_re - f_im[..., None] * b_im
    bb_im = f_re[..., None] * b_im + f_im[..., None] * b_re

    def apow(k):
        mk = jnp.exp(k * dt * a_re)
        return (mk * jnp.cos(k * dt * a_im)).reshape(-1), (mk * jnp.sin(k * dt * a_im)).reshape(-1)

    rows = np.arange(SUBLANES)[:, None]

    def shift_table(r, part):
        return jnp.where(jnp.asarray(rows >= r), apow(float(r // batch))[part][None, :], 0.0)

    as_arr = jnp.stack([jnp.stack([shift_table(r, part) for r in _s5_row_shifts(batch)])
                        for part in (0, 1)])
    pws = [apow(float(k // batch + 1)) for k in range(SUBLANES)]
    pw_arr = jnp.stack([jnp.stack([p[0] for p in pws]), jnp.stack([p[1] for p in pws])])

    gpb = S5_GROUPS // S5_BLK
    eye = jnp.eye(gpb, dtype=F32)

    def pack_b(bb):
        bb = bb.reshape(S5_BLK, gpb, S5_STATE, S5_CH)
        return jnp.einsum('qgpc,gh->qgchp', bb, eye).reshape(S5_BLK, gpb * S5_CH, gpb * S5_STATE)

    def pack_c(cc):
        cc = cc.reshape(S5_BLK, gpb, S5_CH, S5_STATE)
        return jnp.einsum('qgcp,gh->qgphc', cc, eye).reshape(S5_BLK, gpb * S5_STATE, gpb * S5_CH)

    return (pack_b(bb_re).astype(BF16), pack_b(bb_im).astype(BF16),
            pack_c(c_re.astype(F32)).astype(BF16), pack_c(-c_im.astype(F32)).astype(BF16),
            as_arr, pw_arr)


def _s5(u_tb, packed, d_row, glu_w, glu_b, batch, seq, out_dtype):
    rows = min(TM_S5, seq) * batch
    bre, bim, cre, cim, as_arr, pw_arr = packed
    w = BRANCH_WIDTH
    full = lambda *shape: pl.BlockSpec(shape, lambda i: (0,) * len(shape))
    return pl.pallas_call(
        functools.partial(_s5_kernel, rows=rows, batch=batch),
        grid=(seq * batch // rows,),
        in_specs=[
            pl.BlockSpec((rows, w), lambda i: (i, 0)),
            full(S5_BLK, LANES, S5_BLK_STATE), full(S5_BLK, LANES, S5_BLK_STATE),
            full(S5_BLK, S5_BLK_STATE, LANES), full(S5_BLK, S5_BLK_STATE, LANES),
            full(*as_arr.shape), full(2, SUBLANES, S5_NSTATE),
            full(1, w), full(w, w), full(1, w),
        ],
        out_specs=pl.BlockSpec((rows, w), lambda i: (i, 0)),
        out_shape=jax.ShapeDtypeStruct((seq * batch, w), out_dtype),
        scratch_shapes=[
            pltpu.VMEM((rows, S5_NSTATE), F32),
            pltpu.VMEM((rows, S5_NSTATE), F32),
            pltpu.VMEM((SUBLANES, S5_NSTATE), F32),
            pltpu.VMEM((SUBLANES, S5_NSTATE), F32),
        ],
        compiler_params=_cparams(("arbitrary",)),
        name="s5",
    )(u_tb, bre, bim, cre, cim, as_arr, pw_arr, d_row, glu_w, glu_b)


def _merge_kernel(x_ref, mod_ref, g_ref, oa_ref, ob_ref, oc_ref, od_ref, wg_ref, bg_ref,
                  wb_ref, wo_ref, out_ref, h_scr, acc_scr):
    n = pl.program_id(1)

    @pl.when(n == 0)
    def _():
        _modulated_norm_into(h_scr, x_ref, g_ref[0:1, :], mod_ref[0:1, :], mod_ref[1:2, :])
        acc_scr[...] = jnp.zeros_like(acc_scr)

    hb = h_scr[...]
    merged = None
    for i, o_ref in enumerate((oa_ref, ob_ref, oc_ref, od_ref)):
        gate = _sigmoid(_dot(hb, wg_ref[i]) + bg_ref[i])
        term = gate * _dot(o_ref[...].astype(BF16), wb_ref[i])
        merged = term if merged is None else merged + term
    acc_scr[...] += _dot(merged.astype(BF16), wo_ref[...])

    @pl.when(n == pl.num_programs(1) - 1)
    def _():
        _gated_residual_into(out_ref, x_ref, acc_scr, g_ref[1:2, :], mod_ref[2:3, :])


def _merge(x2, mod_l, gains, oa, ob, oc, od, w_gate, b_gate, w_branch, w_out, seq):
    t, d = x2.shape
    tm, tn = min(TM_MERGE, seq), TN_MERGE
    tiles_per_seq = seq // tm
    w = BRANCH_WIDTH
    br_spec = pl.BlockSpec((tm, w), lambda i, n: (i, 0))
    return pl.pallas_call(
        _merge_kernel,
        grid=(t // tm, d // tn),
        in_specs=[
            pl.BlockSpec((tm, d), lambda i, n: (i, 0)),
            pl.BlockSpec((None, SUBLANES, d), lambda i, n: (i // tiles_per_seq, 0, 0)),
            pl.BlockSpec((2, d), lambda i, n: (0, 0)),
            br_spec, br_spec, br_spec,
            pl.BlockSpec((tm, w), lambda i, n: (i % tiles_per_seq, i // tiles_per_seq)),
            pl.BlockSpec((N_BRANCH, d, tn), lambda i, n: (0, 0, n)),
            pl.BlockSpec((N_BRANCH, 1, tn), lambda i, n: (0, 0, n)),
            pl.BlockSpec((N_BRANCH, w, tn), lambda i, n: (0, 0, n)),
            pl.BlockSpec((tn, d), lambda i, n: (n, 0)),
        ],
        out_specs=pl.BlockSpec((tm, d), lambda i, n: (i, 0)),
        out_shape=jax.ShapeDtypeStruct((t, d), F32),
        scratch_shapes=[pltpu.VMEM((tm, d), BF16), pltpu.VMEM((tm, d), F32)],
        compiler_params=_cparams(("parallel", "arbitrary")),
        name="merge",
    )(x2, mod_l, gains, oa, ob, oc, od, w_gate, b_gate, w_branch, w_out)


def _ffn_kernel(x_ref, mod_ref, g_ref, wa_ref, wg_ref, wo_ref, out_ref, h_scr, acc_scr):
    j = pl.program_id(1)

    @pl.when(j == 0)
    def _():
        _modulated_norm_into(h_scr, x_ref, g_ref[0:1, :], mod_ref[3:4, :], mod_ref[4:5, :])
        acc_scr[...] = jnp.zeros_like(acc_scr)

    hb = h_scr[...]
    a = _dot(hb, wa_ref[...])
    g = _dot(hb, wg_ref[...])
    acc_scr[...] += _dot((_silu(a) * g).astype(BF16), wo_ref[...])

    @pl.when(j == pl.num_programs(1) - 1)
    def _():
        _gated_residual_into(out_ref, x_ref, acc_scr, g_ref[1:2, :], mod_ref[5:6, :])


def _ffn(x2, mod_l, gains, w_in, w_out, seq):
    t, d = x2.shape
    tm, th = min(TM_FFN, seq), TH_FFN
    tiles_per_seq = seq // tm
    nh = FFN_HIDDEN // th
    return pl.pallas_call(
        _ffn_kernel,
        grid=(t // tm, nh),
        in_specs=[
            pl.BlockSpec((tm, d), lambda i, j: (i, 0)),
            pl.BlockSpec((None, SUBLANES, d), lambda i, j: (i // tiles_per_seq, 0, 0)),
            pl.BlockSpec((2, d), lambda i, j: (0, 0)),
            pl.BlockSpec((d, th), lambda i, j: (0, j)),
            pl.BlockSpec((d, th), lambda i, j: (0, j + nh)),
            pl.BlockSpec((th, d), lambda i, j: (j, 0)),
        ],
        out_specs=pl.BlockSpec((tm, d), lambda i, j: (i, 0)),
        out_shape=jax.ShapeDtypeStruct((t, d), F32),
        scratch_shapes=[pltpu.VMEM((tm, d), BF16), pltpu.VMEM((tm, d), F32)],
        compiler_params=_cparams(("parallel", "arbitrary")),
        name="ffn",
    )(x2, mod_l, gains, w_in, w_in, w_out)


def _split_w_in(w):
    d = w.shape[0]
    main = jnp.concatenate([w[:, :3584], w[:, 3592:5128], w[:, 5144:5656]], axis=1)
    small = jnp.concatenate([w[:, 3584:3592], w[:, 5128:5144],
                             jnp.zeros((d, SMALL_W - 2 * ML_HEADS - GLA_RANK), w.dtype)], axis=1)
    return main.astype(BF16), small.astype(BF16)


def _pad_row(vals, offset):
    row = jnp.zeros((1, SMALL_W), F32)
    return lax.dynamic_update_slice(row, vals.reshape(1, -1).astype(F32), (0, offset))


ACT_DTYPE = BF16


def kernel(x, c, ada_w, ada_b, norm_g, w_in, rel_bias, diff_lambda, ml_conv, ml_gate_b,
           gla_wa2, gla_ba, s5_a_re, s5_a_im, s5_log_dt, s5_b_re, s5_b_im, s5_c_re, s5_c_im,
           s5_d, s5_glu_w, s5_glu_b, w_branch, w_gate, b_gate, w_out, ffn_w_in, ffn_w_out):
    batch, seq, d = x.shape
    depth = ada_w.shape[0]
    t = batch * seq

    c_pad = jnp.concatenate([c, jnp.zeros((SUBLANES - batch, d), c.dtype)], axis=0)
    mod = _adaln(c_pad, ada_w, ada_b)[:, :batch]
    mod = mod.reshape(depth, batch, N_MOD, d)
    mod = jnp.concatenate([mod, jnp.zeros((depth, batch, SUBLANES - N_MOD, d), F32)], axis=2)

    bias_tiles = _bias_tiles(rel_bias, min(TQ_ATT, seq))

    x2 = x.reshape(t, d)
    for l in range(depth):
        w_main, w_small = _split_w_in(w_in[l])
        pm, ps, pu = _proj(x2, mod[l], norm_g[l, 0:1], w_main, w_small, seq, ACT_DTYPE)

        lam_init = 0.8 - 0.6 * math.exp(-0.3 * l)
        lp = diff_lambda[l].astype(F32)
        lam = (jnp.exp(jnp.sum(lp[0] * lp[1])) - jnp.exp(jnp.sum(lp[2] * lp[3])) + lam_init)
        o_a = _attention(lam.reshape(1), pm, bias_tiles, batch, seq, lam_init, BF16)

        gate_row = (_pad_row(ml_gate_b[l, 0], SM_ML_I) + _pad_row(ml_gate_b[l, 1], SM_ML_F))
        o_b = _mlstm(pm, ps, ml_conv[l].astype(F32), gate_row, batch, seq, BF16)

        wa_pad = jnp.zeros((SMALL_W, GLA_HEADS * GLA_DK), F32)
        wa_pad = lax.dynamic_update_slice(wa_pad, gla_wa2[l].astype(F32), (SM_GL_A, 0))
        o_c = _gla(pm, ps, wa_pad, gla_ba[l].reshape(1, -1).astype(F32), batch, seq, BF16)

        packed = _s5_params(s5_a_re[l], s5_a_im[l], s5_log_dt[l], s5_b_re[l], s5_b_im[l],
                            s5_c_re[l], s5_c_im[l], batch)
        o_d = _s5(pu.reshape(seq * batch, BRANCH_WIDTH), packed, s5_d[l].reshape(1, -1).astype(F32),
                  s5_glu_w[l].astype(BF16), s5_glu_b[l].reshape(1, -1).astype(F32), batch, seq, BF16)
        o_d = o_d.reshape(seq, batch * BRANCH_WIDTH)

        x2 = _merge(x2, mod[l], norm_g[l, 0:2], o_a, o_b, o_c, o_d,
                    w_gate[l].astype(BF16), b_gate[l].reshape(N_BRANCH, 1, d).astype(F32),
                    w_branch[l].astype(BF16), w_out[l].astype(BF16), seq)
        x2 = _ffn(x2, mod[l], norm_g[l, 2:4], ffn_w_in[l].astype(BF16),
                  ffn_w_out[l].astype(BF16), seq)
    return x2.reshape(batch, seq, d)
```

```python
import functools
import math

import numpy as np
import jax
import jax.numpy as jnp
from jax import lax
from jax.experimental import pallas as pl
from jax.experimental.pallas import tpu as pltpu

F32 = jnp.float32
BF16 = jnp.bfloat16
HIGHEST = lax.Precision.HIGHEST

D_MODEL = 2048
DEPTH = 4
EPS = 1e-6
N_MOD = 6
N_BRANCH = 4
BRANCH_WIDTH = 512
DA_HEADS = 4
DA_QK_DIM = 64
DA_V_DIM = 128
N_BUCKETS = 32
MAX_DISTANCE = 128
ML_HEADS = 4
ML_DIM = 128
ML_CONV = 4
GLA_HEADS = 4
GLA_DK = 64
GLA_DV = 128
GLA_RANK = 16
GLA_TAU = 16.0
S5_CH = 16
S5_GROUPS = BRANCH_WIDTH // S5_CH
S5_STATE = 64
FFN_HIDDEN = -(-(8 * D_MODEL) // (3 * 256)) * 256

LANES = 128
SUBLANES = 8
BF16_ROWS = 16
VMEM_LIMIT = 56 * 1024 * 1024

MAIN_W = 5632
SMALL_W = LANES
OFF_DA_Q, OFF_DA_K, OFF_DA_V = 0, 512, 1024
OFF_ML_Q, OFF_ML_K, OFF_ML_V, OFF_ML_O = 1536, 2048, 2560, 3072
OFF_GL_Q, OFF_GL_K, OFF_GL_V, OFF_GL_R = 3584, 3840, 4096, 4608
OFF_S5_U = 5120
SM_ML_I, SM_ML_F, SM_GL_A = 0, 4, 8

NEG = -1e30
LOG2E = math.log2(math.e)

TM_PROJ, TN_PROJ = 1024, 512
TQ_ATT = 512
L_MLSTM = 256
L_GLA, C_GLA = 256, 16
TM_S5 = 256
TM_MERGE, TN_MERGE = 512, 256
TM_FFN, TH_FFN = 512, 512


def _cparams(sem):
    return pltpu.CompilerParams(dimension_semantics=sem, vmem_limit_bytes=VMEM_LIMIT)


def _rms(x):
    return x * lax.rsqrt(jnp.mean(x * x, axis=-1, keepdims=True) + EPS)


def _sigmoid(x):
    return 1.0 / (1.0 + jnp.exp(-x))


def _silu(x):
    return x * _sigmoid(x)


def _log_sigmoid(x):
    return jnp.minimum(x, 0.0) - jnp.log1p(jnp.exp(-jnp.abs(x)))


def _dot(a, b):
    return jnp.dot(a, b, preferred_element_type=F32)


def _dot_nt(a, b):
    return lax.dot_general(a, b, (((1,), (1,)), ((), ())), preferred_element_type=F32)


def _dot_tn(a, b):
    return lax.dot_general(a, b, (((0,), (0,)), ((), ())), preferred_element_type=F32)


def _dot_exact(a, b):
    return jnp.dot(a, b, preferred_element_type=F32, precision=HIGHEST)


def _adaln_kernel(c_ref, w_ref, b_ref, o_ref):
    c = c_ref[...]
    o_ref[...] = _dot(_silu(c).astype(BF16), w_ref[...].astype(BF16)) + b_ref[...]


def _adaln(c_pad, ada_w, ada_b):
    depth, d, n = ada_w.shape
    rows = c_pad.shape[0]
    tn = 2048
    return pl.pallas_call(
        _adaln_kernel,
        grid=(depth, n // tn),
        in_specs=[
            pl.BlockSpec((rows, d), lambda l, j: (0, 0)),
            pl.BlockSpec((None, d, tn), lambda l, j: (l, 0, j)),
            pl.BlockSpec((None, 1, tn), lambda l, j: (l, 0, j)),
        ],
        out_specs=pl.BlockSpec((None, rows, tn), lambda l, j: (l, 0, j)),
        out_shape=jax.ShapeDtypeStruct((depth, rows, n), F32),
        compiler_params=_cparams(("parallel", "parallel")),
        name="adaln",
    )(c_pad, ada_w, ada_b.reshape(depth, 1, n))


ROW_CHUNK = 32


def _for_row_chunks(n_rows, fn):
    def body(c, carry):
        fn(pl.ds(pl.multiple_of(c * ROW_CHUNK, ROW_CHUNK), ROW_CHUNK))
        return carry

    lax.fori_loop(0, n_rows // ROW_CHUNK, body, 0, unroll=4)


def _modulated_norm_into(h_scr, x_ref, gain, shift, scale):
    gs = gain * (1.0 + scale)

    def chunk(rows):
        h_scr[rows, :] = (_rms(x_ref[rows, :]) * gs + shift).astype(h_scr.dtype)

    _for_row_chunks(h_scr.shape[0], chunk)


def _gated_residual_into(out_ref, x_ref, acc_scr, gain, gate):
    gg = gain * gate

    def chunk(rows):
        out_ref[rows, :] = x_ref[rows, :] + _rms(acc_scr[rows, :]) * gg

    _for_row_chunks(out_ref.shape[0], chunk)


def _proj_kernel(x_ref, mod_ref, g_ref, w_ref, ws_ref, o_ref, os_ref, ou_ref, h_scr,
                 *, j_s5, off_s5):
    j = pl.program_id(1)

    @pl.when(j == 0)
    def _():
        gs = g_ref[...] * (1.0 + mod_ref[1:2, :])
        hb = (_rms(x_ref[...]) * gs + mod_ref[0:1, :]).astype(BF16)
        h_scr[...] = hb
        os_ref[...] = _dot(hb, ws_ref[...])

    res = _dot(h_scr[...], w_ref[...]).astype(o_ref.dtype)
    o_ref[...] = res

    @pl.when(j == j_s5)
    def _():
        ou_ref[...] = res[:, off_s5:off_s5 + BRANCH_WIDTH]


def _proj(x2, mod_l, gain, w_main, w_small, seq, out_dtype):
    t, d = x2.shape
    tm, tn = min(TM_PROJ, seq), TN_PROJ
    tiles_per_seq = seq // tm
    w = BRANCH_WIDTH
    j_s5, off_s5 = divmod(OFF_S5_U, tn)
    assert MAIN_W % tn == 0 and off_s5 % LANES == 0 and off_s5 + w <= tn
    return pl.pallas_call(
        functools.partial(_proj_kernel, j_s5=j_s5, off_s5=off_s5),
        grid=(t // tm, MAIN_W // tn),
        in_specs=[
            pl.BlockSpec((tm, d), lambda i, j: (i, 0)),
            pl.BlockSpec((None, SUBLANES, d), lambda i, j: (i // tiles_per_seq, 0, 0)),
            pl.BlockSpec((1, d), lambda i, j: (0, 0)),
            pl.BlockSpec((d, tn), lambda i, j: (0, j)),
            pl.BlockSpec((d, SMALL_W), lambda i, j: (0, 0)),
        ],
        out_specs=[
            pl.BlockSpec((tm, tn), lambda i, j: (i, j)),
            pl.BlockSpec((tm, SMALL_W), lambda i, j: (i, 0)),
            pl.BlockSpec((tm, w), lambda i, j: (i % tiles_per_seq, i // tiles_per_seq)),
        ],
        out_shape=[
            jax.ShapeDtypeStruct((t, MAIN_W), out_dtype),
            jax.ShapeDtypeStruct((t, SMALL_W), F32),
            jax.ShapeDtypeStruct((seq, (t // seq) * w), out_dtype),
        ],
        scratch_shapes=[pltpu.VMEM((tm, d), BF16)],
        compiler_params=_cparams(("parallel", "arbitrary")),
        name="proj",
    )(x2, mod_l, gain, w_main, w_small)


def _attn_kernel(lam_ref, q_ref, k_ref, v_ref, bias_ref, o_ref, m_scr, l_scr, acc_scr,
                 s0_scr, s1_scr, *, tq, out_scale):
    i = pl.program_id(2)
    lam = lam_ref[0]
    q = q_ref[...].astype(F32) * (DA_QK_DIM ** -0.5 * LOG2E)
    lane = lax.broadcasted_iota(jnp.int32, q.shape, 1)
    qa = jnp.where(lane < DA_QK_DIM, q, 0.0).astype(BF16)
    qb = jnp.where(lane >= DA_QK_DIM, q, 0.0).astype(BF16)
    q2 = jnp.concatenate([qa, qb], axis=0)

    m_scr[...] = jnp.full_like(m_scr, NEG)
    l_scr[...] = jnp.zeros_like(l_scr)
    acc_scr[...] = jnp.zeros_like(acc_scr)

    def scores(j):
        r0 = pl.multiple_of(j * tq, tq)
        return _dot_nt(k_ref[pl.ds(r0, tq), :].astype(BF16), q2)

    def accumulate(j, s):
        r0 = pl.multiple_of(j * tq, tq)
        vt = v_ref[pl.ds(r0, tq), :].astype(BF16)
        m_old = m_scr[...]
        m_new = jnp.maximum(m_old, jnp.max(s, axis=0, keepdims=True))
        p = jnp.exp2(s - m_new)
        alpha = jnp.exp2(m_old - m_new)
        l_scr[...] = alpha * l_scr[...] + jnp.sum(p, axis=0, keepdims=True)
        acc_scr[...] = alpha * acc_scr[...] + _dot_tn(vt, p.astype(BF16))
        m_scr[...] = m_new

    def biased_scores(j, bias):
        return scores(j) + bias

    n_far = jnp.maximum(i - 1, 0)
    j_prev = jnp.maximum(i - 1, 0)
    s0_scr[...] = biased_scores(i, bias_ref[1])
    s1_scr[...] = biased_scores(j_prev, bias_ref[jnp.where(i >= 1, 0, 2)])
    accumulate(i, s0_scr[...])
    s0_scr[...] = scores(0)
    accumulate(j_prev, s1_scr[...])

    def far_pair(base):
        s1_scr[...] = scores(base + 1)
        accumulate(base, s0_scr[...])
        s0_scr[...] = scores(jnp.minimum(base + 2, n_far - 1))
        accumulate(base + 1, s1_scr[...])

    def far_quad(g, carry):
        far_pair(4 * g)
        far_pair(4 * g + 2)
        return carry

    quads = n_far // 4
    lax.fori_loop(0, quads, far_quad, 0)
    rest = n_far - 4 * quads

    @pl.when(rest >= 2)
    def _():
        far_pair(4 * quads)

    @pl.when(rest % 2 == 1)
    def _():
        accumulate(n_far - 1, s0_scr[...])

    on = acc_scr[...] / l_scr[...]
    ot = on[:, 0:tq] - lam * on[:, tq:2 * tq]
    ot = ot * (lax.rsqrt(jnp.mean(ot * ot, axis=0, keepdims=True) + EPS) * out_scale)
    o_ref[...] = ot.T.astype(o_ref.dtype)


def _attention(lam, pm, bias_tiles, batch, seq, lam_init, out_dtype):
    tq = min(TQ_ATT, seq)
    nq = seq // tq
    kern = functools.partial(_attn_kernel, tq=tq, out_scale=1.0 - lam_init)
    scratch = [pltpu.VMEM((1, 2 * tq), F32), pltpu.VMEM((1, 2 * tq), F32),
               pltpu.VMEM((DA_V_DIM, 2 * tq), F32),
               pltpu.VMEM((tq, 2 * tq), F32), pltpu.VMEM((tq, 2 * tq), F32)]
    qb, kb, vb = OFF_DA_Q // LANES, OFF_DA_K // LANES, OFF_DA_V // LANES
    return pl.pallas_call(
        kern,
        grid=(batch, DA_HEADS, nq),
        in_specs=[
            pl.BlockSpec(memory_space=pltpu.SMEM),
            pl.BlockSpec((tq, LANES), lambda b, h, i: (b * nq + i, qb + h)),
            pl.BlockSpec((seq, LANES), lambda b, h, i: (b, kb + h)),
            pl.BlockSpec((seq, LANES), lambda b, h, i: (b, vb + h)),
            pl.BlockSpec((None, 3, tq, 2 * tq), lambda b, h, i: (h, 0, 0, 0)),
        ],
        out_specs=pl.BlockSpec((tq, LANES), lambda b, h, i: (b * nq + i, h)),
        out_shape=jax.ShapeDtypeStruct((batch * seq, DA_HEADS * DA_V_DIM), out_dtype),
        scratch_shapes=scratch,
        compiler_params=_cparams(("parallel", "parallel", "arbitrary")),
        name="diff_attn",
    )(lam, pm, pm, pm, bias_tiles)


def _t5_bucket_table(n_max):
    n = np.arange(n_max)
    exact = N_BUCKETS // 2
    nf = np.maximum(n, 1).astype(np.float64)
    large = exact + (np.log(nf / exact) / math.log(MAX_DISTANCE / exact)
                     * (N_BUCKETS - exact)).astype(np.int64)
    return np.where(n < exact, n, np.minimum(large, N_BUCKETS - 1)).astype(np.int32)


def _bias_tiles(rel_bias, tq):
    assert tq >= MAX_DISTANCE
    n = tq
    heads = rel_bias.shape[1]
    rb = rel_bias.astype(F32)
    near = (rb[_t5_bucket_table(MAX_DISTANCE)] - rb[N_BUCKETS - 1][None, :]) * LOG2E
    f = jnp.concatenate([near, jnp.zeros((2 * n - MAX_DISTANCE, heads), F32)], axis=0)

    def toeplitz(v):
        vp = jnp.concatenate([v, jnp.zeros((1, heads), F32)], axis=0)
        flat = jnp.tile(vp, (n, 1))[: n * (2 * n - 1)]
        return jnp.transpose(flat.reshape(n, 2 * n - 1, heads)[:, n - 1:, :], (2, 0, 1))

    prev = toeplitz(f[1:2 * n])
    diag = toeplitz(jnp.concatenate([jnp.full((n - 1, heads), NEG, F32), f[0:n]], axis=0))
    masked = jnp.full_like(prev, NEG)
    tiles = jnp.stack([prev, diag, masked], axis=1)
    return jnp.concatenate([tiles, tiles], axis=-1)


def _mlstm_kernel(q_ref, k_ref, v_ref, og_ref, sm_ref, qh_ref, kh_ref, cw_ref, gb_ref,
                  tril_ref, out_ref, xq_scr, xk_scr, c_scr, n_scr, m_scr, *, L):
    ci = pl.program_id(1)

    @pl.when(ci == 0)
    def _():
        c_scr[...] = jnp.zeros_like(c_scr)
        n_scr[...] = jnp.zeros_like(n_scr)
        m_scr[...] = jnp.zeros_like(m_scr)

    halo = BF16_ROWS
    keep = (ci > 0).astype(F32)
    xq_scr[0:halo, :] = qh_ref[...].astype(F32) * keep
    xk_scr[0:halo, :] = kh_ref[...].astype(F32) * keep
    xq_scr[halo:halo + L, :] = q_ref[...].astype(F32)
    xk_scr[halo:halo + L, :] = k_ref[...].astype(F32)

    def conv_silu(scr, w):
        y = scr[halo:halo + L, :] * w[ML_CONV - 1:ML_CONV, :]
        for j in range(ML_CONV - 1):
            off = halo - (ML_CONV - 1) + j
            y = y + scr[off:off + L, :] * w[j:j + 1, :]
        return _silu(y)

    w_all = cw_ref[...]
    width = ML_HEADS * ML_DIM
    q_all = conv_silu(xq_scr, w_all[:, 0:width])
    k_all = conv_silu(xk_scr, w_all[:, width:2 * width]) * (ML_DIM ** -0.5)

    g = sm_ref[...] + gb_ref[...]
    lane = lax.broadcasted_iota(jnp.int32, g.shape, 1)
    is_f = (lane >= SM_ML_F) & (lane < SM_ML_F + ML_HEADS)
    g = jnp.where(is_f, _log_sigmoid(g), g)
    bcum = _dot_exact(tril_ref[...], g)
    g_t = g.T
    b_t = bcum.T
    row = lax.broadcasted_iota(jnp.int32, (L, L), 0)
    colj = lax.broadcasted_iota(jnp.int32, (L, L), 1)
    causal = colj <= row

    for h in range(ML_HEADS):
        sl = slice(h * ML_DIM, (h + 1) * ML_DIM)
        qh = q_all[:, sl]
        kh = k_all[:, sl]
        vh = v_ref[:, sl].astype(F32)
        qb, kb, vb = qh.astype(BF16), kh.astype(BF16), vh.astype(BF16)
        ig_col = g[:, SM_ML_I + h:SM_ML_I + h + 1]
        b_col = bcum[:, SM_ML_F + h:SM_ML_F + h + 1]
        ig_row = g_t[SM_ML_I + h:SM_ML_I + h + 1, :]
        b_row = b_t[SM_ML_F + h:SM_ML_F + h + 1, :]
        m_old = m_scr[h][:, 0:1]
        c_old = c_scr[h]
        n_old = n_scr[h]

        dm = jnp.where(causal, b_col - b_row + ig_row, NEG)
        inter = b_col + m_old
        m_t = jnp.maximum(inter, jnp.max(dm, axis=-1, keepdims=True))
        s = _dot_nt(qb, kb) * jnp.exp(dm - m_t)
        a = jnp.exp(inter - m_t)
        num = a * _dot(qb, c_old.astype(BF16)) + _dot(s.astype(BF16), vb)
        den = (a * jnp.sum(qh * n_old, axis=-1, keepdims=True)
               + jnp.sum(s, axis=-1, keepdims=True))
        hv = num / jnp.maximum(jnp.abs(den), jnp.exp(-m_t))
        out_ref[:, sl] = (_sigmoid(og_ref[:, sl].astype(F32)) * hv).astype(out_ref.dtype)

        m_new = m_t[L - 1:L, :]
        b_last = b_col[L - 1:L, :]
        a_state = jnp.exp(b_last + m_old - m_new)
        w_col = jnp.exp(b_last - b_col + ig_col - m_new)
        kw = kh * w_col
        c_scr[h] = a_state * c_old + _dot_tn(kw.astype(BF16), vb)
        n_scr[h] = a_state * n_old + jnp.sum(kw, axis=0, keepdims=True)
        m_scr[h] = jnp.broadcast_to(m_new, (1, LANES))


def _mlstm(pm, ps, conv_w, gate_row, batch, seq, out_dtype):
    L = min(L_MLSTM, seq)
    nc = seq // L
    width = ML_HEADS * ML_DIM
    qb, kb, vb, ob = (OFF_ML_Q // width, OFF_ML_K // width, OFF_ML_V // width, OFF_ML_O // width)
    lb = L // BF16_ROWS
    tril = jnp.asarray(np.tril(np.ones((L, L), np.float32)))

    def halo_map(colblk):
        return lambda b, c: (jnp.maximum(b * (seq // BF16_ROWS) + c * lb - 1, 0), colblk)

    return pl.pallas_call(
        functools.partial(_mlstm_kernel, L=L),
        grid=(batch, nc),
        in_specs=[
            pl.BlockSpec((L, width), lambda b, c: (b * nc + c, qb)),
            pl.BlockSpec((L, width), lambda b, c: (b * nc + c, kb)),
            pl.BlockSpec((L, width), lambda b, c: (b * nc + c, vb)),
            pl.BlockSpec((L, width), lambda b, c: (b * nc + c, ob)),
            pl.BlockSpec((L, SMALL_W), lambda b, c: (b * nc + c, 0)),
            pl.BlockSpec((BF16_ROWS, width), halo_map(qb)),
            pl.BlockSpec((BF16_ROWS, width), halo_map(kb)),
            pl.BlockSpec((ML_CONV, 2 * width), lambda b, c: (0, 0)),
            pl.BlockSpec((1, SMALL_W), lambda b, c: (0, 0)),
            pl.BlockSpec((L, L), lambda b, c: (0, 0)),
        ],
        out_specs=pl.BlockSpec((L, width), lambda b, c: (b * nc + c, 0)),
        out_shape=jax.ShapeDtypeStruct((batch * seq, width), out_dtype),
        scratch_shapes=[
            pltpu.VMEM((L + BF16_ROWS, width), F32),
            pltpu.VMEM((L + BF16_ROWS, width), F32),
            pltpu.VMEM((ML_HEADS, ML_DIM, ML_DIM), F32),
            pltpu.VMEM((ML_HEADS, 1, ML_DIM), F32),
            pltpu.VMEM((ML_HEADS, 1, LANES), F32),
        ],
        compiler_params=_cparams(("parallel", "arbitrary")),
        name="mlstm",
    )(pm, pm, pm, pm, ps, pm, pm, conv_w, gate_row, tril)


def _gla_kernel(q_ref, k_ref, v_ref, r_ref, sm_ref, wa_ref, ba_ref, tril_ref, mexp_ref,
                out_ref, bc_scr, a_scr, st_scr, *, L, c):
    ci = pl.program_id(1)

    @pl.when(ci == 0)
    def _():
        st_scr[...] = jnp.zeros_like(st_scr)

    la = _log_sigmoid(_dot_exact(sm_ref[...], wa_ref[...]) + ba_ref[...]) * (1.0 / GLA_TAU)
    bc_scr[...] = _dot_exact(tril_ref[...], la)
    kw = GLA_HEADS * GLA_DK
    rowc = lax.broadcasted_iota(jnp.int32, (c, kw), 0)
    lane_head = lax.broadcasted_iota(jnp.int32, (c, kw), 1) // GLA_DK
    head_masks = [lane_head == h for h in range(GLA_HEADS)]

    def stack_heads(x):
        return jnp.concatenate([jnp.where(mk, x, 0.0) for mk in head_masks], axis=0)

    def sub(i, carry):
        r0 = pl.multiple_of(i * c, c)
        qs = q_ref[pl.ds(r0, c), :].astype(F32) * (GLA_DK ** -0.5)
        ks = k_ref[pl.ds(r0, c), :].astype(F32)
        vs = v_ref[pl.ds(r0, c), :].astype(BF16).astype(F32)
        bcs = bc_scr[pl.ds(r0, c), :]
        e_end = bcs[c - 1:c, :]
        st = st_scr[...]
        o_stack = _dot_nt(stack_heads(qs * jnp.exp(bcs)).astype(BF16), st.astype(BF16))
        o = jnp.concatenate([o_stack[h * c:(h + 1) * c] for h in range(GLA_HEADS)], axis=1)

        for t in range(c):
            dec = jnp.exp(jnp.minimum(bcs[t:t + 1, :] - bcs, 0.0))
            a_t = jnp.where(rowc <= t, qs[t:t + 1, :] * ks * dec, 0.0)
            a_scr[t * c:(t + 1) * c, :] = a_t.astype(BF16)
        p = _dot(a_scr[...], mexp_ref[...])
        o = o + jnp.sum(p.reshape(c, c, GLA_HEADS * GLA_DV) * vs[None, :, :], axis=1)

        outs = [_rms(o[:, h * GLA_DV:(h + 1) * GLA_DV]) for h in range(GLA_HEADS)]
        on = jnp.concatenate(outs, axis=1)
        out_ref[pl.ds(r0, c), :] = (on * _silu(r_ref[pl.ds(r0, c), :].astype(F32))).astype(out_ref.dtype)

        khat = stack_heads(ks * jnp.exp(e_end - bcs)).astype(BF16)
        v_stack = jnp.concatenate([vs[:, h * GLA_DV:(h + 1) * GLA_DV] for h in range(GLA_HEADS)],
                                  axis=0).astype(BF16)
        st_scr[...] = st * jnp.exp(e_end) + _dot_tn(v_stack, khat)
        return carry

    lax.fori_loop(0, L // c, sub, 0, unroll=8)


def _gla(pm, ps, wa_pad, ba_row, batch, seq, out_dtype):
    L = min(L_GLA, seq)
    c = C_GLA
    nc = seq // L
    kw, vw = GLA_HEADS * GLA_DK, GLA_HEADS * GLA_DV
    qb, kb, vb, rb = OFF_GL_Q // kw, OFF_GL_K // kw, OFF_GL_V // vw, OFF_GL_R // vw
    idx = np.arange(L)
    tril = ((idx[:, None] >= idx[None, :]) & (idx[:, None] // c == idx[None, :] // c))
    tril = jnp.asarray(tril.astype(np.float32))
    mexp = np.zeros((kw, vw), np.float32)
    for h in range(GLA_HEADS):
        mexp[h * GLA_DK:(h + 1) * GLA_DK, h * GLA_DV:(h + 1) * GLA_DV] = 1.0
    mexp = jnp.asarray(mexp, dtype=BF16)
    return pl.pallas_call(
        functools.partial(_gla_kernel, L=L, c=c),
        grid=(batch, nc),
        in_specs=[
            pl.BlockSpec((L, kw), lambda b, i: (b * nc + i, qb)),
            pl.BlockSpec((L, kw), lambda b, i: (b * nc + i, kb)),
            pl.BlockSpec((L, vw), lambda b, i: (b * nc + i, vb)),
            pl.BlockSpec((L, vw), lambda b, i: (b * nc + i, rb)),
            pl.BlockSpec((L, SMALL_W), lambda b, i: (b * nc + i, 0)),
            pl.BlockSpec((SMALL_W, kw), lambda b, i: (0, 0)),
            pl.BlockSpec((1, kw), lambda b, i: (0, 0)),
            pl.BlockSpec((L, L), lambda b, i: (0, 0)),
            pl.BlockSpec((kw, vw), lambda b, i: (0, 0)),
        ],
        out_specs=pl.BlockSpec((L, vw), lambda b, i: (b * nc + i, 0)),
        out_shape=jax.ShapeDtypeStruct((batch * seq, vw), out_dtype),
        scratch_shapes=[
            pltpu.VMEM((L, kw), F32),
            pltpu.VMEM((c * c, kw), BF16),
            pltpu.VMEM((GLA_DV, kw), F32),
        ],
        compiler_params=_cparams(("parallel", "arbitrary")),
        name="gla",
    )(pm, pm, pm, pm, ps, wa_pad, ba_row, tril, mexp)


S5_NSTATE = S5_GROUPS * S5_STATE
S5_BLK = 4
S5_BLK_STATE = S5_NSTATE // S5_BLK


def _gelu_tanh(x):
    return 0.5 * x * (1.0 + jnp.tanh(math.sqrt(2.0 / math.pi) * (x + 0.044715 * (x * x * x))))


def _s5_kernel(u_ref, bre_ref, bim_ref, cre_ref, cim_ref, as_ref, pw_ref, d_ref, gw_ref,
               gb_ref, out_ref, xr_scr, xi_scr, cr_scr, ci_scr, *, rows, batch):
    ti = pl.program_id(0)

    @pl.when(ti == 0)
    def _():
        cr_scr[...] = jnp.zeros_like(cr_scr)
        ci_scr[...] = jnp.zeros_like(ci_scr)

    u = u_ref[...].astype(F32)
    ub = u.astype(BF16)
    nb = S5_BLK_STATE
    for q in range(S5_BLK):
        uq = ub[:, q * LANES:(q + 1) * LANES]
        xr_scr[:, q * nb:(q + 1) * nb] = _dot(uq, bre_ref[q])
        xi_scr[:, q * nb:(q + 1) * nb] = _dot(uq, bim_ref[q])

    rowi = lax.broadcasted_iota(jnp.int32, (SUBLANES, nb), 0)
    shifts = _s5_row_shifts(batch)
    for cc in range(S5_BLK):
        cols = slice(cc * nb, (cc + 1) * nb)

        def body(g, carry, cols=cols):
            cr, ci = carry
            for r in shifts:
                cr = jnp.where(rowi < SUBLANES - r, pltpu.roll(cr, SUBLANES - r, 0), cr)
                ci = jnp.where(rowi < SUBLANES - r, pltpu.roll(ci, SUBLANES - r, 0), ci)
            r0 = pl.multiple_of(g * SUBLANES, SUBLANES)
            zr = xr_scr[pl.ds(r0, SUBLANES), cols]
            zi = xi_scr[pl.ds(r0, SUBLANES), cols]
            for si, r in enumerate(shifts):
                sr = pltpu.roll(zr, r, 0)
                sim = pltpu.roll(zi, r, 0)
                ar = as_ref[0, si, :, cols]
                ai = as_ref[1, si, :, cols]
                zr, zi = zr + ar * sr - ai * sim, zi + ar * sim + ai * sr
            p_r = pw_ref[0, :, cols]
            p_i = pw_ref[1, :, cols]
            xr = zr + p_r * cr - p_i * ci
            xi = zi + p_r * ci + p_i * cr
            xr_scr[pl.ds(r0, SUBLANES), cols] = xr
            xi_scr[pl.ds(r0, SUBLANES), cols] = xi
            return xr, xi

        cr, ci = lax.fori_loop(0, rows // SUBLANES, body, (cr_scr[:, cols], ci_scr[:, cols]))
        cr_scr[:, cols] = cr
        ci_scr[:, cols] = ci

    ys = []
    for q in range(S5_BLK):
        xr = xr_scr[:, q * nb:(q + 1) * nb].astype(BF16)
        xi = xi_scr[:, q * nb:(q + 1) * nb].astype(BF16)
        ys.append(_dot(xr, cre_ref[q]) + _dot(xi, cim_ref[q]))
    y = jnp.concatenate(ys, axis=1) + d_ref[...] * u
    z = _gelu_tanh(y)
    gate = _sigmoid(_dot(z.astype(BF16), gw_ref[...]) + gb_ref[...])
    out_ref[...] = (z * gate).astype(out_ref.dtype)


def _s5_row_shifts(batch):
    assert SUBLANES % batch == 0
    return tuple(batch * (1 << k) for k in range(8) if batch * (1 << k) < SUBLANES)


def _s5_params(a_re, a_im, log_dt, b_re, b_im, c_re, c_im, batch):
    a_re, a_im = a_re.astype(F32), a_im.astype(F32)
    dt = jnp.exp(log_dt.astype(F32))[:, None]
    mag = jnp.exp(dt * a_re)
    ab_re, ab_im = mag * jnp.cos(dt * a_im), mag * jnp.sin(dt * a_im)
    nr, ni = ab_re - 1.0, ab_im
    den = a_re * a_re + a_im * a_im
    f_re = (nr * a_re + ni * a_im) / den
    f_im = (ni * a_re - nr * a_im) / den
    b_re, b_im = b_re.astype(F32), b_im.astype(F32)
    bb_re = f_re[..., None] * b_re - f_im[..., None] * b_im
    bb_im = f_re[..., None] * b_im + f_im[..., None] * b_re

    def apow(k):
        mk = jnp.exp(k * dt * a_re)
        return (mk * jnp.cos(k * dt * a_im)).reshape(-1), (mk * jnp.sin(k * dt * a_im)).reshape(-1)

    rows = np.arange(SUBLANES)[:, None]

    def shift_table(r, part):
        return jnp.where(jnp.asarray(rows >= r), apow(float(r // batch))[part][None, :], 0.0)

    as_arr = jnp.stack([jnp.stack([shift_table(r, part) for r in _s5_row_shifts(batch)])
                        for part in (0, 1)])
    pws = [apow(float(k // batch + 1)) for k in range(SUBLANES)]
    pw_arr = jnp.stack([jnp.stack([p[0] for p in pws]), jnp.stack([p[1] for p in pws])])

    gpb = S5_GROUPS // S5_BLK
    eye = jnp.eye(gpb, dtype=F32)

    def pack_b(bb):
        bb = bb.reshape(S5_BLK, gpb, S5_STATE, S5_CH)
        return jnp.einsum('qgpc,gh->qgchp', bb, eye).reshape(S5_BLK, gpb * S5_CH, gpb * S5_STATE)

    def pack_c(cc):
        cc = cc.reshape(S5_BLK, gpb, S5_CH, S5_STATE)
        return jnp.einsum('qgcp,gh->qgphc', cc, eye).reshape(S5_BLK, gpb * S5_STATE, gpb * S5_CH)

    return (pack_b(bb_re).astype(BF16), pack_b(bb_im).astype(BF16),
            pack_c(c_re.astype(F32)).astype(BF16), pack_c(-c_im.astype(F32)).astype(BF16),
            as_arr, pw_arr)


def _s5(u_tb, packed, d_row, glu_w, glu_b, batch, seq, out_dtype):
    rows = min(TM_S5, seq) * batch
    bre, bim, cre, cim, as_arr, pw_arr = packed
    w = BRANCH_WIDTH
    full = lambda *shape: pl.BlockSpec(shape, lambda i: (0,) * len(shape))
    return pl.pallas_call(
        functools.partial(_s5_kernel, rows=rows, batch=batch),
        grid=(seq * batch // rows,),
        in_specs=[
            pl.BlockSpec((rows, w), lambda i: (i, 0)),
            full(S5_BLK, LANES, S5_BLK_STATE), full(S5_BLK, LANES, S5_BLK_STATE),
            full(S5_BLK, S5_BLK_STATE, LANES), full(S5_BLK, S5_BLK_STATE, LANES),
            full(*as_arr.shape), full(2, SUBLANES, S5_NSTATE),
            full(1, w), full(w, w), full(1, w),
        ],
        out_specs=pl.BlockSpec((rows, w), lambda i: (i, 0)),
        out_shape=jax.ShapeDtypeStruct((seq * batch, w), out_dtype),
        scratch_shapes=[
            pltpu.VMEM((rows, S5_NSTATE), F32),
            pltpu.VMEM((rows, S5_NSTATE), F32),
            pltpu.VMEM((SUBLANES, S5_NSTATE), F32),
            pltpu.VMEM((SUBLANES, S5_NSTATE), F32),
        ],
        compiler_params=_cparams(("arbitrary",)),
        name="s5",
    )(u_tb, bre, bim, cre, cim, as_arr, pw_arr, d_row, glu_w, glu_b)


def _merge_kernel(x_ref, mod_ref, g_ref, oa_ref, ob_ref, oc_ref, od_ref, wg_ref, bg_ref,
                  wb_ref, wo_ref, out_ref, h_scr, acc_scr):
    n = pl.program_id(1)

    @pl.when(n == 0)
    def _():
        _modulated_norm_into(h_scr, x_ref, g_ref[0:1, :], mod_ref[0:1, :], mod_ref[1:2, :])
        acc_scr[...] = jnp.zeros_like(acc_scr)

    hb = h_scr[...]
    merged = None
    for i, o_ref in enumerate((oa_ref, ob_ref, oc_ref, od_ref)):
        gate = _sigmoid(_dot(hb, wg_ref[i]) + bg_ref[i])
        term = gate * _dot(o_ref[...].astype(BF16), wb_ref[i])
        merged = term if merged is None else merged + term
    acc_scr[...] += _dot(merged.astype(BF16), wo_ref[...])

    @pl.when(n == pl.num_programs(1) - 1)
    def _():
        _gated_residual_into(out_ref, x_ref, acc_scr, g_ref[1:2, :], mod_ref[2:3, :])


def _merge(x2, mod_l, gains, oa, ob, oc, od, w_gate, b_gate, w_branch, w_out, seq):
    t, d = x2.shape
    tm, tn = min(TM_MERGE, seq), TN_MERGE
    tiles_per_seq = seq // tm
    w = BRANCH_WIDTH
    br_spec = pl.BlockSpec((tm, w), lambda i, n: (i, 0))
    return pl.pallas_call(
        _merge_kernel,
        grid=(t // tm, d // tn),
        in_specs=[
            pl.BlockSpec((tm, d), lambda i, n: (i, 0)),
            pl.BlockSpec((None, SUBLANES, d), lambda i, n: (i // tiles_per_seq, 0, 0)),
            pl.BlockSpec((2, d), lambda i, n: (0, 0)),
            br_spec, br_spec, br_spec,
            pl.BlockSpec((tm, w), lambda i, n: (i % tiles_per_seq, i // tiles_per_seq)),
            pl.BlockSpec((N_BRANCH, d, tn), lambda i, n: (0, 0, n)),
            pl.BlockSpec((N_BRANCH, 1, tn), lambda i, n: (0, 0, n)),
            pl.BlockSpec((N_BRANCH, w, tn), lambda i, n: (0, 0, n)),
            pl.BlockSpec((tn, d), lambda i, n: (n, 0)),
        ],
        out_specs=pl.BlockSpec((tm, d), lambda i, n: (i, 0)),
        out_shape=jax.ShapeDtypeStruct((t, d), F32),
        scratch_shapes=[pltpu.VMEM((tm, d), BF16), pltpu.VMEM((tm, d), F32)],
        compiler_params=_cparams(("parallel", "arbitrary")),
        name="merge",
    )(x2, mod_l, gains, oa, ob, oc, od, w_gate, b_gate, w_branch, w_out)


def _ffn_kernel(x_ref, mod_ref, g_ref, wa_ref, wg_ref, wo_ref, out_ref, h_scr, acc_scr):
    j = pl.program_id(1)

    @pl.when(j == 0)
    def _():
        _modulated_norm_into(h_scr, x_ref, g_ref[0:1, :], mod_ref[3:4, :], mod_ref[4:5, :])
        acc_scr[...] = jnp.zeros_like(acc_scr)

    hb = h_scr[...]
    a = _dot(hb, wa_ref[...])
    g = _dot(hb, wg_ref[...])
    acc_scr[...] += _dot((_silu(a) * g).astype(BF16), wo_ref[...])

    @pl.when(j == pl.num_programs(1) - 1)
    def _():
        _gated_residual_into(out_ref, x_ref, acc_scr, g_ref[1:2, :], mod_ref[5:6, :])


def _ffn(x2, mod_l, gains, w_in, w_out, seq):
    t, d = x2.shape
    tm, th = min(TM_FFN, seq), TH_FFN
    tiles_per_seq = seq // tm
    nh = FFN_HIDDEN // th
    return pl.pallas_call(
        _ffn_kernel,
        grid=(t // tm, nh),
        in_specs=[
            pl.BlockSpec((tm, d), lambda i, j: (i, 0)),
            pl.BlockSpec((None, SUBLANES, d), lambda i, j: (i // tiles_per_seq, 0, 0)),
            pl.BlockSpec((2, d), lambda i, j: (0, 0)),
            pl.BlockSpec((d, th), lambda i, j: (0, j)),
            pl.BlockSpec((d, th), lambda i, j: (0, j + nh)),
            pl.BlockSpec((th, d), lambda i, j: (j, 0)),
        ],
        out_specs=pl.BlockSpec((tm, d), lambda i, j: (i, 0)),
        out_shape=jax.ShapeDtypeStruct((t, d), F32),
        scratch_shapes=[pltpu.VMEM((tm, d), BF16), pltpu.VMEM((tm, d), F32)],
        compiler_params=_cparams(("parallel", "arbitrary")),
        name="ffn",
    )(x2, mod_l, gains, w_in, w_in, w_out)


def _split_w_in(w):
    d = w.shape[0]
    main = jnp.concatenate([w[:, :3584], w[:, 3592:5128], w[:, 5144:5656]], axis=1)
    small = jnp.concatenate([w[:, 3584:3592], w[:, 5128:5144],
                             jnp.zeros((d, SMALL_W - 2 * ML_HEADS - GLA_RANK), w.dtype)], axis=1)
    return main.astype(BF16), small.astype(BF16)


def _pad_row(vals, offset):
    row = jnp.zeros((1, SMALL_W), F32)
    return lax.dynamic_update_slice(row, vals.reshape(1, -1).astype(F32), (0, offset))


ACT_DTYPE = BF16


def kernel(x, c, ada_w, ada_b, norm_g, w_in, rel_bias, diff_lambda, ml_conv, ml_gate_b,
           gla_wa2, gla_ba, s5_a_re, s5_a_im, s5_log_dt, s5_b_re, s5_b_im, s5_c_re, s5_c_im,
           s5_d, s5_glu_w, s5_glu_b, w_branch, w_gate, b_gate, w_out, ffn_w_in, ffn_w_out):
    batch, seq, d = x.shape
    depth = ada_w.shape[0]
    t = batch * seq

    c_pad = jnp.concatenate([c, jnp.zeros((SUBLANES - batch, d), c.dtype)], axis=0)
    mod = _adaln(c_pad, ada_w, ada_b)[:, :batch]
    mod = mod.reshape(depth, batch, N_MOD, d)
    mod = jnp.concatenate([mod, jnp.zeros((depth, batch, SUBLANES - N_MOD, d), F32)], axis=2)

    bias_tiles = _bias_tiles(rel_bias, min(TQ_ATT, seq))

    x2 = x.reshape(t, d)
    for l in range(depth):
        w_main, w_small = _split_w_in(w_in[l])
        pm, ps, pu = _proj(x2, mod[l], norm_g[l, 0:1], w_main, w_small, seq, ACT_DTYPE)

        lam_init = 0.8 - 0.6 * math.exp(-0.3 * l)
        lp = diff_lambda[l].astype(F32)
        lam = (jnp.exp(jnp.sum(lp[0] * lp[1])) - jnp.exp(jnp.sum(lp[2] * lp[3])) + lam_init)
        o_a = _attention(lam.reshape(1), pm, bias_tiles, batch, seq, lam_init, BF16)

        gate_row = (_pad_row(ml_gate_b[l, 0], SM_ML_I) + _pad_row(ml_gate_b[l, 1], SM_ML_F))
        o_b = _mlstm(pm, ps, ml_conv[l].astype(F32), gate_row, batch, seq, BF16)

        wa_pad = jnp.zeros((SMALL_W, GLA_HEADS * GLA_DK), F32)
        wa_pad = lax.dynamic_update_slice(wa_pad, gla_wa2[l].astype(F32), (SM_GL_A, 0))
        o_c = _gla(pm, ps, wa_pad, gla_ba[l].reshape(1, -1).astype(F32), batch, seq, BF16)

        packed = _s5_params(s5_a_re[l], s5_a_im[l], s5_log_dt[l], s5_b_re[l], s5_b_im[l],
                            s5_c_re[l], s5_c_im[l], batch)
        o_d = _s5(pu.reshape(seq * batch, BRANCH_WIDTH), packed, s5_d[l].reshape(1, -1).astype(F32),
                  s5_glu_w[l].astype(BF16), s5_glu_b[l].reshape(1, -1).astype(F32), batch, seq, BF16)
        o_d = o_d.reshape(seq, batch * BRANCH_WIDTH)

        x2 = _merge(x2, mod[l], norm_g[l, 0:2], o_a, o_b, o_c, o_d,
                    w_gate[l].astype(BF16), b_gate[l].reshape(N_BRANCH, 1, d).astype(F32),
                    w_branch[l].astype(BF16), w_out[l].astype(BF16), seq)
        x2 = _ffn(x2, mod[l], norm_g[l, 2:4], ffn_w_in[l].astype(BF16),
                  ffn_w_out[l].astype(BF16), seq)
    return x2.reshape(batch, seq, d)
```

```python
import functools
import math

import numpy as np
import jax
import jax.numpy as jnp
from jax import lax
from jax.experimental import pallas as pl
from jax.experimental.pallas import tpu as pltpu

F32 = jnp.float32
BF16 = jnp.bfloat16
HIGHEST = lax.Precision.HIGHEST

D_MODEL = 2048
DEPTH = 4
EPS = 1e-6
N_MOD = 6
N_BRANCH = 4
BRANCH_WIDTH = 512
DA_HEADS = 4
DA_QK_DIM = 64
DA_V_DIM = 128
N_BUCKETS = 32
MAX_DISTANCE = 128
ML_HEADS = 4
ML_DIM = 128
ML_CONV = 4
GLA_HEADS = 4
GLA_DK = 64
GLA_DV = 128
GLA_RANK = 16
GLA_TAU = 16.0
S5_CH = 16
S5_GROUPS = BRANCH_WIDTH // S5_CH
S5_STATE = 64
FFN_HIDDEN = -(-(8 * D_MODEL) // (3 * 256)) * 256

LANES = 128
SUBLANES = 8
BF16_ROWS = 16
VMEM_LIMIT = 56 * 1024 * 1024

MAIN_W = 5632
SMALL_W = LANES
OFF_DA_Q, OFF_DA_K, OFF_DA_V = 0, 512, 1024
OFF_ML_Q, OFF_ML_K, OFF_ML_V, OFF_ML_O = 1536, 2048, 2560, 3072
OFF_GL_Q, OFF_GL_K, OFF_GL_V, OFF_GL_R = 3584, 3840, 4096, 4608
OFF_S5_U = 5120
SM_ML_I, SM_ML_F, SM_GL_A = 0, 4, 8

NEG = -1e30
LOG2E = math.log2(math.e)

TM_PROJ, TN_PROJ = 1024, 512
TQ_ATT = 512
L_MLSTM = 256
L_GLA, C_GLA = 256, 16
TM_S5 = 256
TM_MERGE, TN_MERGE = 512, 256
TM_FFN, TH_FFN = 512, 512


def _cparams(sem):
    return pltpu.CompilerParams(dimension_semantics=sem, vmem_limit_bytes=VMEM_LIMIT)


def _rms(x):
    return x * lax.rsqrt(jnp.mean(x * x, axis=-1, keepdims=True) + EPS)


def _sigmoid(x):
    return 1.0 / (1.0 + jnp.exp(-x))


def _silu(x):
    return x * _sigmoid(x)


def _log_sigmoid(x):
    return jnp.minimum(x, 0.0) - jnp.log1p(jnp.exp(-jnp.abs(x)))


def _dot(a, b):
    return jnp.dot(a, b, preferred_element_type=F32)


def _dot_nt(a, b):
    return lax.dot_general(a, b, (((1,), (1,)), ((), ())), preferred_element_type=F32)


def _dot_tn(a, b):
    return lax.dot_general(a, b, (((0,), (0,)), ((), ())), preferred_element_type=F32)


def _dot_exact(a, b):
    return jnp.dot(a, b, preferred_element_type=F32, precision=HIGHEST)


def _adaln_kernel(c_ref, w_ref, b_ref, o_ref):
    c = c_ref[...]
    o_ref[...] = _dot(_silu(c).astype(BF16), w_ref[...].astype(BF16)) + b_ref[...]


def _adaln(c_pad, ada_w, ada_b):
    depth, d, n = ada_w.shape
    rows = c_pad.shape[0]
    tn = 2048
    return pl.pallas_call(
        _adaln_kernel,
        grid=(depth, n // tn),
        in_specs=[
            pl.BlockSpec((rows, d), lambda l, j: (0, 0)),
            pl.BlockSpec((None, d, tn), lambda l, j: (l, 0, j)),
            pl.BlockSpec((None, 1, tn), lambda l, j: (l, 0, j)),
        ],
        out_specs=pl.BlockSpec((None, rows, tn), lambda l, j: (l, 0, j)),
        out_shape=jax.ShapeDtypeStruct((depth, rows, n), F32),
        compiler_params=_cparams(("parallel", "parallel")),
        name="adaln",
    )(c_pad, ada_w, ada_b.reshape(depth, 1, n))


ROW_CHUNK = 32


def _for_row_chunks(n_rows, fn):
    def body(c, carry):
        fn(pl.ds(pl.multiple_of(c * ROW_CHUNK, ROW_CHUNK), ROW_CHUNK))
        return carry

    lax.fori_loop(0, n_rows // ROW_CHUNK, body, 0, unroll=4)


def _modulated_norm_into(h_scr, x_ref, gain, shift, scale):
    gs = gain * (1.0 + scale)

    def chunk(rows):
        h_scr[rows, :] = (_rms(x_ref[rows, :]) * gs + shift).astype(h_scr.dtype)

    _for_row_chunks(h_scr.shape[0], chunk)


def _gated_residual_into(out_ref, x_ref, acc_scr, gain, gate):
    gg = gain * gate

    def chunk(rows):
        out_ref[rows, :] = x_ref[rows, :] + _rms(acc_scr[rows, :]) * gg

    _for_row_chunks(out_ref.shape[0], chunk)


def _proj_kernel(x_ref, mod_ref, g_ref, w_ref, ws_ref, o_ref, os_ref, ou_ref, h_scr,
                 *, j_s5, off_s5):
    j = pl.program_id(1)

    @pl.when(j == 0)
    def _():
        gs = g_ref[...] * (1.0 + mod_ref[1:2, :])
        hb = (_rms(x_ref[...]) * gs + mod_ref[0:1, :]).astype(BF16)
        h_scr[...] = hb
        os_ref[...] = _dot(hb, ws_ref[...])

    res = _dot(h_scr[...], w_ref[...]).astype(o_ref.dtype)
    o_ref[...] = res

    @pl.when(j == j_s5)
    def _():
        ou_ref[...] = res[:, off_s5:off_s5 + BRANCH_WIDTH]


def _proj(x2, mod_l, gain, w_main, w_small, seq, out_dtype):
    t, d = x2.shape
    tm, tn = min(TM_PROJ, seq), TN_PROJ
    tiles_per_seq = seq // tm
    w = BRANCH_WIDTH
    j_s5, off_s5 = divmod(OFF_S5_U, tn)
    assert MAIN_W % tn == 0 and off_s5 % LANES == 0 and off_s5 + w <= tn
    return pl.pallas_call(
        functools.partial(_proj_kernel, j_s5=j_s5, off_s5=off_s5),
        grid=(t // tm, MAIN_W // tn),
        in_specs=[
            pl.BlockSpec((tm, d), lambda i, j: (i, 0)),
            pl.BlockSpec((None, SUBLANES, d), lambda i, j: (i // tiles_per_seq, 0, 0)),
            pl.BlockSpec((1, d), lambda i, j: (0, 0)),
            pl.BlockSpec((d, tn), lambda i, j: (0, j)),
            pl.BlockSpec((d, SMALL_W), lambda i, j: (0, 0)),
        ],
        out_specs=[
            pl.BlockSpec((tm, tn), lambda i, j: (i, j)),
            pl.BlockSpec((tm, SMALL_W), lambda i, j: (i, 0)),
            pl.BlockSpec((tm, w), lambda i, j: (i % tiles_per_seq, i // tiles_per_seq)),
        ],
        out_shape=[
            jax.ShapeDtypeStruct((t, MAIN_W), out_dtype),
            jax.ShapeDtypeStruct((t, SMALL_W), F32),
            jax.ShapeDtypeStruct((seq, (t // seq) * w), out_dtype),
        ],
        scratch_shapes=[pltpu.VMEM((tm, d), BF16)],
        compiler_params=_cparams(("parallel", "arbitrary")),
        name="proj",
    )(x2, mod_l, gain, w_main, w_small)


def _attn_kernel(lam_ref, q_ref, k_ref, v_ref, bias_ref, o_ref, m_scr, l_scr, acc_scr,
                 s0_scr, s1_scr, *, tq, out_scale):
    i = pl.program_id(2)
    lam = lam_ref[0]
    q = q_ref[...].astype(F32) * (DA_QK_DIM ** -0.5 * LOG2E)
    lane = lax.broadcasted_iota(jnp.int32, q.shape, 1)
    qa = jnp.where(lane < DA_QK_DIM, q, 0.0).astype(BF16)
    qb = jnp.where(lane >= DA_QK_DIM, q, 0.0).astype(BF16)
    q2 = jnp.concatenate([qa, qb], axis=0)

    m_scr[...] = jnp.full_like(m_scr, NEG)
    l_scr[...] = jnp.zeros_like(l_scr)
    acc_scr[...] = jnp.zeros_like(acc_scr)

    def scores(j):
        r0 = pl.multiple_of(j * tq, tq)
        return _dot_nt(k_ref[pl.ds(r0, tq), :].astype(BF16), q2)

    def accumulate(j, s):
        r0 = pl.multiple_of(j * tq, tq)
        vt = v_ref[pl.ds(r0, tq), :].astype(BF16)
        m_old = m_scr[...]
        m_new = jnp.maximum(m_old, jnp.max(s, axis=0, keepdims=True))
        p = jnp.exp2(s - m_new)
        alpha = jnp.exp2(m_old - m_new)
        l_scr[...] = alpha * l_scr[...] + jnp.sum(p, axis=0, keepdims=True)
        acc_scr[...] = alpha * acc_scr[...] + _dot_tn(vt, p.astype(BF16))
        m_scr[...] = m_new

    def biased_scores(j, bias):
        return scores(j) + bias

    n_far = jnp.maximum(i - 1, 0)
    j_prev = jnp.maximum(i - 1, 0)
    s0_scr[...] = biased_scores(i, bias_ref[1])
    s1_scr[...] = biased_scores(j_prev, bias_ref[jnp.where(i >= 1, 0, 2)])
    accumulate(i, s0_scr[...])
    s0_scr[...] = scores(0)
    accumulate(j_prev, s1_scr[...])

    def far_pair(base):
        s1_scr[...] = scores(base + 1)
        accumulate(base, s0_scr[...])
        s0_scr[...] = scores(jnp.minimum(base + 2, n_far - 1))
        accumulate(base + 1, s1_scr[...])

    def far_quad(g, carry):
        far_pair(4 * g)
        far_pair(4 * g + 2)
        return carry

    quads = n_far // 4
    lax.fori_loop(0, quads, far_quad, 0)
    rest = n_far - 4 * quads

    @pl.when(rest >= 2)
    def _():
        far_pair(4 * quads)

    @pl.when(rest % 2 == 1)
    def _():
        accumulate(n_far - 1, s0_scr[...])

    on = acc_scr[...] / l_scr[...]
    ot = on[:, 0:tq] - lam * on[:, tq:2 * tq]
    ot = ot * (lax.rsqrt(jnp.mean(ot * ot, axis=0, keepdims=True) + EPS) * out_scale)
    o_ref[...] = ot.T.astype(o_ref.dtype)


def _attention(lam, pm, bias_tiles, batch, seq, lam_init, out_dtype):
    tq = min(TQ_ATT, seq)
    nq = seq // tq
    kern = functools.partial(_attn_kernel, tq=tq, out_scale=1.0 - lam_init)
    scratch = [pltpu.VMEM((1, 2 * tq), F32), pltpu.VMEM((1, 2 * tq), F32),
               pltpu.VMEM((DA_V_DIM, 2 * tq), F32),
               pltpu.VMEM((tq, 2 * tq), F32), pltpu.VMEM((tq, 2 * tq), F32)]
    qb, kb, vb = OFF_DA_Q // LANES, OFF_DA_K // LANES, OFF_DA_V // LANES
    return pl.pallas_call(
        kern,
        grid=(batch, DA_HEADS, nq),
        in_specs=[
            pl.BlockSpec(memory_space=pltpu.SMEM),
            pl.BlockSpec((tq, LANES), lambda b, h, i: (b * nq + i, qb + h)),
            pl.BlockSpec((seq, LANES), lambda b, h, i: (b, kb + h)),
            pl.BlockSpec((seq, LANES), lambda b, h, i: (b, vb + h)),
            pl.BlockSpec((None, 3, tq, 2 * tq), lambda b, h, i: (h, 0, 0, 0)),
        ],
        out_specs=pl.BlockSpec((tq, LANES), lambda b, h, i: (b * nq + i, h)),
        out_shape=jax.ShapeDtypeStruct((batch * seq, DA_HEADS * DA_V_DIM), out_dtype),
        scratch_shapes=scratch,
        compiler_params=_cparams(("parallel", "parallel", "arbitrary")),
        name="diff_attn",
    )(lam, pm, pm, pm, bias_tiles)


def _t5_bucket_table(n_max):
    n = np.arange(n_max)
    exact = N_BUCKETS // 2
    nf = np.maximum(n, 1).astype(np.float64)
    large = exact + (np.log(nf / exact) / math.log(MAX_DISTANCE / exact)
                     * (N_BUCKETS - exact)).astype(np.int64)
    return np.where(n < exact, n, np.minimum(large, N_BUCKETS - 1)).astype(np.int32)


def _bias_tiles(rel_bias, tq):
    assert tq >= MAX_DISTANCE
    n = tq
    heads = rel_bias.shape[1]
    rb = rel_bias.astype(F32)
    near = (rb[_t5_bucket_table(MAX_DISTANCE)] - rb[N_BUCKETS - 1][None, :]) * LOG2E
    f = jnp.concatenate([near, jnp.zeros((2 * n - MAX_DISTANCE, heads), F32)], axis=0)

    def toeplitz(v):
        vp = jnp.concatenate([v, jnp.zeros((1, heads), F32)], axis=0)
        flat = jnp.tile(vp, (n, 1))[: n * (2 * n - 1)]
        return jnp.transpose(flat.reshape(n, 2 * n - 1, heads)[:, n - 1:, :], (2, 0, 1))

    prev = toeplitz(f[1:2 * n])
    diag = toeplitz(jnp.concatenate([jnp.full((n - 1, heads), NEG, F32), f[0:n]], axis=0))
    masked = jnp.full_like(prev, NEG)
    tiles = jnp.stack([prev, diag, masked], axis=1)
    return jnp.concatenate([tiles, tiles], axis=-1)


def _mlstm_kernel(q_ref, k_ref, v_ref, og_ref, sm_ref, qh_ref, kh_ref, cw_ref, gb_ref,
                  tril_ref, out_ref, xq_scr, xk_scr, c_scr, n_scr, m_scr, *, L):
    ci = pl.program_id(1)

    @pl.when(ci == 0)
    def _():
        c_scr[...] = jnp.zeros_like(c_scr)
        n_scr[...] = jnp.zeros_like(n_scr)
        m_scr[...] = jnp.zeros_like(m_scr)

    halo = BF16_ROWS
    keep = (ci > 0).astype(F32)
    xq_scr[0:halo, :] = qh_ref[...].astype(F32) * keep
    xk_scr[0:halo, :] = kh_ref[...].astype(F32) * keep
    xq_scr[halo:halo + L, :] = q_ref[...].astype(F32)
    xk_scr[halo:halo + L, :] = k_ref[...].astype(F32)

    def conv_silu(scr, w):
        y = scr[halo:halo + L, :] * w[ML_CONV - 1:ML_CONV, :]
        for j in range(ML_CONV - 1):
            off = halo - (ML_CONV - 1) + j
            y = y + scr[off:off + L, :] * w[j:j + 1, :]
        return _silu(y)

    w_all = cw_ref[...]
    width = ML_HEADS * ML_DIM
    q_all = conv_silu(xq_scr, w_all[:, 0:width])
    k_all = conv_silu(xk_scr, w_all[:, width:2 * width]) * (ML_DIM ** -0.5)

    g = sm_ref[...] + gb_ref[...]
    lane = lax.broadcasted_iota(jnp.int32, g.shape, 1)
    is_f = (lane >= SM_ML_F) & (lane < SM_ML_F + ML_HEADS)
    g = jnp.where(is_f, _log_sigmoid(g), g)
    bcum = _dot_exact(tril_ref[...], g)
    g_t = g.T
    b_t = bcum.T
    row = lax.broadcasted_iota(jnp.int32, (L, L), 0)
    colj = lax.broadcasted_iota(jnp.int32, (L, L), 1)
    causal = colj <= row

    for h in range(ML_HEADS):
        sl = slice(h * ML_DIM, (h + 1) * ML_DIM)
        qh = q_all[:, sl]
        kh = k_all[:, sl]
        vh = v_ref[:, sl].astype(F32)
        qb, kb, vb = qh.astype(BF16), kh.astype(BF16), vh.astype(BF16)
        ig_col = g[:, SM_ML_I + h:SM_ML_I + h + 1]
        b_col = bcum[:, SM_ML_F + h:SM_ML_F + h + 1]
        ig_row = g_t[SM_ML_I + h:SM_ML_I + h + 1, :]
        b_row = b_t[SM_ML_F + h:SM_ML_F + h + 1, :]
        m_old = m_scr[h][:, 0:1]
        c_old = c_scr[h]
        n_old = n_scr[h]

        dm = jnp.where(causal, b_col - b_row + ig_row, NEG)
        inter = b_col + m_old
        m_t = jnp.maximum(inter, jnp.max(dm, axis=-1, keepdims=True))
        s = _dot_nt(qb, kb) * jnp.exp(dm - m_t)
        a = jnp.exp(inter - m_t)
        num = a * _dot(qb, c_old.astype(BF16)) + _dot(s.astype(BF16), vb)
        den = (a * jnp.sum(qh * n_old, axis=-1, keepdims=True)
               + jnp.sum(s, axis=-1, keepdims=True))
        hv = num / jnp.maximum(jnp.abs(den), jnp.exp(-m_t))
        out_ref[:, sl] = (_sigmoid(og_ref[:, sl].astype(F32)) * hv).astype(out_ref.dtype)

        m_new = m_t[L - 1:L, :]
        b_last = b_col[L - 1:L, :]
        a_state = jnp.exp(b_last + m_old - m_new)
        w_col = jnp.exp(b_last - b_col + ig_col - m_new)
        kw = kh * w_col
        c_scr[h] = a_state * c_old + _dot_tn(kw.astype(BF16), vb)
        n_scr[h] = a_state * n_old + jnp.sum(kw, axis=0, keepdims=True)
        m_scr[h] = jnp.broadcast_to(m_new, (1, LANES))


def _mlstm(pm, ps, conv_w, gate_row, batch, seq, out_dtype):
    L = min(L_MLSTM, seq)
    nc = seq // L
    width = ML_HEADS * ML_DIM
    qb, kb, vb, ob = (OFF_ML_Q // width, OFF_ML_K // width, OFF_ML_V // width, OFF_ML_O // width)
    lb = L // BF16_ROWS
    tril = jnp.asarray(np.tril(np.ones((L, L), np.float32)))

    def halo_map(colblk):
        return lambda b, c: (jnp.maximum(b * (seq // BF16_ROWS) + c * lb - 1, 0), colblk)

    return pl.pallas_call(
        functools.partial(_mlstm_kernel, L=L),
        grid=(batch, nc),
        in_specs=[
            pl.BlockSpec((L, width), lambda b, c: (b * nc + c, qb)),
            pl.BlockSpec((L, width), lambda b, c: (b * nc + c, kb)),
            pl.BlockSpec((L, width), lambda b, c: (b * nc + c, vb)),
            pl.BlockSpec((L, width), lambda b, c: (b * nc + c, ob)),
            pl.BlockSpec((L, SMALL_W), lambda b, c: (b * nc + c, 0)),
            pl.BlockSpec((BF16_ROWS, width), halo_map(qb)),
            pl.BlockSpec((BF16_ROWS, width), halo_map(kb)),
            pl.BlockSpec((ML_CONV, 2 * width), lambda b, c: (0, 0)),
            pl.BlockSpec((1, SMALL_W), lambda b, c: (0, 0)),
            pl.BlockSpec((L, L), lambda b, c: (0, 0)),
        ],
        out_specs=pl.BlockSpec((L, width), lambda b, c: (b * nc + c, 0)),
        out_shape=jax.ShapeDtypeStruct((batch * seq, width), out_dtype),
        scratch_shapes=[
            pltpu.VMEM((L + BF16_ROWS, width), F32),
            pltpu.VMEM((L + BF16_ROWS, width), F32),
            pltpu.VMEM((ML_HEADS, ML_DIM, ML_DIM), F32),
            pltpu.VMEM((ML_HEADS, 1, ML_DIM), F32),
            pltpu.VMEM((ML_HEADS, 1, LANES), F32),
        ],
        compiler_params=_cparams(("parallel", "arbitrary")),
        name="mlstm",
    )(pm, pm, pm, pm, ps, pm, pm, conv_w, gate_row, tril)


def _gla_kernel(q_ref, k_ref, v_ref, r_ref, sm_ref, wa_ref, ba_ref, tril_ref, mexp_ref,
                out_ref, bc_scr, a_scr, st_scr, *, L, c):
    ci = pl.program_id(1)

    @pl.when(ci == 0)
    def _():
        st_scr[...] = jnp.zeros_like(st_scr)

    la = _log_sigmoid(_dot_exact(sm_ref[...], wa_ref[...]) + ba_ref[...]) * (1.0 / GLA_TAU)
    bc_scr[...] = _dot_exact(tril_ref[...], la)
    kw = GLA_HEADS * GLA_DK
    rowc = lax.broadcasted_iota(jnp.int32, (c, kw), 0)
    lane_head = lax.broadcasted_iota(jnp.int32, (c, kw), 1) // GLA_DK
    head_masks = [lane_head == h for h in range(GLA_HEADS)]

    def stack_heads(x):
        return jnp.concatenate([jnp.where(mk, x, 0.0) for mk in head_masks], axis=0)

    def sub(i, carry):
        r0 = pl.multiple_of(i * c, c)
        qs = q_ref[pl.ds(r0, c), :].astype(F32) * (GLA_DK ** -0.5)
        ks = k_ref[pl.ds(r0, c), :].astype(F32)
        vs = v_ref[pl.ds(r0, c), :].astype(BF16).astype(F32)
        bcs = bc_scr[pl.ds(r0, c), :]
        e_end = bcs[c - 1:c, :]
        st = st_scr[...]
        o_stack = _dot_nt(stack_heads(qs * jnp.exp(bcs)).astype(BF16), st.astype(BF16))
        o = jnp.concatenate([o_stack[h * c:(h + 1) * c] for h in range(GLA_HEADS)], axis=1)

        for t in range(c):
            dec = jnp.exp(jnp.minimum(bcs[t:t + 1, :] - bcs, 0.0))
            a_t = jnp.where(rowc <= t, qs[t:t + 1, :] * ks * dec, 0.0)
            a_scr[t * c:(t + 1) * c, :] = a_t.astype(BF16)
        p = _dot(a_scr[...], mexp_ref[...])
        o = o + jnp.sum(p.reshape(c, c, GLA_HEADS * GLA_DV) * vs[None, :, :], axis=1)

        outs = [_rms(o[:, h * GLA_DV:(h + 1) * GLA_DV]) for h in range(GLA_HEADS)]
        on = jnp.concatenate(outs, axis=1)
        out_ref[pl.ds(r0, c), :] = (on * _silu(r_ref[pl.ds(r0, c), :].astype(F32))).astype(out_ref.dtype)

        khat = stack_heads(ks * jnp.exp(e_end - bcs)).astype(BF16)
        v_stack = jnp.concatenate([vs[:, h * GLA_DV:(h + 1) * GLA_DV] for h in range(GLA_HEADS)],
                                  axis=0).astype(BF16)
        st_scr[...] = st * jnp.exp(e_end) + _dot_tn(v_stack, khat)
        return carry

    lax.fori_loop(0, L // c, sub, 0, unroll=8)


def _gla(pm, ps, wa_pad, ba_row, batch, seq, out_dtype):
    L = min(L_GLA, seq)
    c = C_GLA
    nc = seq // L
    kw, vw = GLA_HEADS * GLA_DK, GLA_HEADS * GLA_DV
    qb, kb, vb, rb = OFF_GL_Q // kw, OFF_GL_K // kw, OFF_GL_V // vw, OFF_GL_R // vw
    idx = np.arange(L)
    tril = ((idx[:, None] >= idx[None, :]) & (idx[:, None] // c == idx[None, :] // c))
    tril = jnp.asarray(tril.astype(np.float32))
    mexp = np.zeros((kw, vw), np.float32)
    for h in range(GLA_HEADS):
        mexp[h * GLA_DK:(h + 1) * GLA_DK, h * GLA_DV:(h + 1) * GLA_DV] = 1.0
    mexp = jnp.asarray(mexp, dtype=BF16)
    return pl.pallas_call(
        functools.partial(_gla_kernel, L=L, c=c),
        grid=(batch, nc),
        in_specs=[
            pl.BlockSpec((L, kw), lambda b, i: (b * nc + i, qb)),
            pl.BlockSpec((L, kw), lambda b, i: (b * nc + i, kb)),
            pl.BlockSpec((L, vw), lambda b, i: (b * nc + i, vb)),
            pl.BlockSpec((L, vw), lambda b, i: (b * nc + i, rb)),
            pl.BlockSpec((L, SMALL_W), lambda b, i: (b * nc + i, 0)),
            pl.BlockSpec((SMALL_W, kw), lambda b, i: (0, 0)),
            pl.BlockSpec((1, kw), lambda b, i: (0, 0)),
            pl.BlockSpec((L, L), lambda b, i: (0, 0)),
            pl.BlockSpec((kw, vw), lambda b, i: (0, 0)),
        ],
        out_specs=pl.BlockSpec((L, vw), lambda b, i: (b * nc + i, 0)),
        out_shape=jax.ShapeDtypeStruct((batch * seq, vw), out_dtype),
        scratch_shapes=[
            pltpu.VMEM((L, kw), F32),
            pltpu.VMEM((c * c, kw), BF16),
            pltpu.VMEM((GLA_DV, kw), F32),
        ],
        compiler_params=_cparams(("parallel", "arbitrary")),
        name="gla",
    )(pm, pm, pm, pm, ps, wa_pad, ba_row, tril, mexp)


S5_NSTATE = S5_GROUPS * S5_STATE
S5_BLK = 4
S5_BLK_STATE = S5_NSTATE // S5_BLK


def _gelu_tanh(x):
    return 0.5 * x * (1.0 + jnp.tanh(math.sqrt(2.0 / math.pi) * (x + 0.044715 * (x * x * x))))


def _s5_kernel(u_ref, e_ref, et_ref, bre_ref, bim_ref, cre_ref, cim_ref, as_ref, pw_ref, d_ref,
               gw_ref, gb_ref, out_ref, xr_scr, xi_scr, cr_scr, ci_scr, *, rows, batch):
    ti = pl.program_id(0)
    w = BRANCH_WIDTH

    @pl.when(ti == 0)
    def _():
        cr_scr[...] = jnp.zeros_like(cr_scr)
        ci_scr[...] = jnp.zeros_like(ci_scr)

    u = _dot(e_ref[0], u_ref[:, 0:w].astype(BF16))
    for b in range(1, batch):
        u = u + _dot(e_ref[b], u_ref[:, b * w:(b + 1) * w].astype(BF16))
    ub = u.astype(BF16)
    nb = S5_BLK_STATE
    for q in range(S5_BLK):
        uq = ub[:, q * LANES:(q + 1) * LANES]
        xr_scr[:, q * nb:(q + 1) * nb] = _dot(uq, bre_ref[q])
        xi_scr[:, q * nb:(q + 1) * nb] = _dot(uq, bim_ref[q])

    rowi = lax.broadcasted_iota(jnp.int32, (SUBLANES, nb), 0)
    shifts = _s5_row_shifts(batch)
    for cc in range(S5_BLK):
        cols = slice(cc * nb, (cc + 1) * nb)

        def body(g, carry, cols=cols):
            cr, ci = carry
            for r in shifts:
                cr = jnp.where(rowi < SUBLANES - r, pltpu.roll(cr, SUBLANES - r, 0), cr)
                ci = jnp.where(rowi < SUBLANES - r, pltpu.roll(ci, SUBLANES - r, 0), ci)
            r0 = pl.multiple_of(g * SUBLANES, SUBLANES)
            zr = xr_scr[pl.ds(r0, SUBLANES), cols]
            zi = xi_scr[pl.ds(r0, SUBLANES), cols]
            for si, r in enumerate(shifts):
                sr = pltpu.roll(zr, r, 0)
                sim = pltpu.roll(zi, r, 0)
                ar = as_ref[0, si, :, cols]
                ai = as_ref[1, si, :, cols]
                zr, zi = zr + ar * sr - ai * sim, zi + ar * sim + ai * sr
            p_r = pw_ref[0, :, cols]
            p_i = pw_ref[1, :, cols]
            xr = zr + p_r * cr - p_i * ci
            xi = zi + p_r * ci + p_i * cr
            xr_scr[pl.ds(r0, SUBLANES), cols] = xr
            xi_scr[pl.ds(r0, SUBLANES), cols] = xi
            return xr, xi

        cr, ci = lax.fori_loop(0, rows // SUBLANES, body, (cr_scr[:, cols], ci_scr[:, cols]))
        cr_scr[:, cols] = cr
        ci_scr[:, cols] = ci

    ys = []
    for q in range(S5_BLK):
        xr = xr_scr[:, q * nb:(q + 1) * nb].astype(BF16)
        xi = xi_scr[:, q * nb:(q + 1) * nb].astype(BF16)
        ys.append(_dot(xr, cre_ref[q]) + _dot(xi, cim_ref[q]))
    y = jnp.concatenate(ys, axis=1) + d_ref[...] * u
    z = _gelu_tanh(y)
    gate = _sigmoid(_dot(z.astype(BF16), gw_ref[...]) + gb_ref[...])
    o = (z * gate).astype(BF16)
    for b in range(batch):
        out_ref[:, b * w:(b + 1) * w] = _dot(et_ref[b], o).astype(out_ref.dtype)


def _s5_row_shifts(batch):
    assert SUBLANES % batch == 0
    return tuple(batch * (1 << k) for k in range(8) if batch * (1 << k) < SUBLANES)


def _s5_params(a_re, a_im, log_dt, b_re, b_im, c_re, c_im, batch):
    a_re, a_im = a_re.astype(F32), a_im.astype(F32)
    dt = jnp.exp(log_dt.astype(F32))[:, None]
    mag = jnp.exp(dt * a_re)
    ab_re, ab_im = mag * jnp.cos(dt * a_im), mag * jnp.sin(dt * a_im)
    nr, ni = ab_re - 1.0, ab_im
    den = a_re * a_re + a_im * a_im
    f_re = (nr * a_re + ni * a_im) / den
    f_im = (ni * a_re - nr * a_im) / den
    b_re, b_im = b_re.astype(F32), b_im.astype(F32)
    bb_re = f_re[..., None] * b_re - f_im[..., None] * b_im
    bb_im = f_re[..., None] * b_im + f_im[..., None] * b_re

    def apow(k):
        mk = jnp.exp(k * dt * a_re)
        return (mk * jnp.cos(k * dt * a_im)).reshape(-1), (mk * jnp.sin(k * dt * a_im)).reshape(-1)

    rows = np.arange(SUBLANES)[:, None]

    def shift_table(r, part):
        return jnp.where(jnp.asarray(rows >= r), apow(float(r // batch))[part][None, :], 0.0)

    as_arr = jnp.stack([jnp.stack([shift_table(r, part) for r in _s5_row_shifts(batch)])
                        for part in (0, 1)])
    pws = [apow(float(k // batch + 1)) for k in range(SUBLANES)]
    pw_arr = jnp.stack([jnp.stack([p[0] for p in pws]), jnp.stack([p[1] for p in pws])])

    gpb = S5_GROUPS // S5_BLK
    eye = jnp.eye(gpb, dtype=F32)

    def pack_b(bb):
        bb = bb.reshape(S5_BLK, gpb, S5_STATE, S5_CH)
        return jnp.einsum('qgpc,gh->qgchp', bb, eye).reshape(S5_BLK, gpb * S5_CH, gpb * S5_STATE)

    def pack_c(cc):
        cc = cc.reshape(S5_BLK, gpb, S5_CH, S5_STATE)
        return jnp.einsum('qgcp,gh->qgphc', cc, eye).reshape(S5_BLK, gpb * S5_STATE, gpb * S5_CH)

    return (pack_b(bb_re).astype(BF16), pack_b(bb_im).astype(BF16),
            pack_c(c_re.astype(F32)).astype(BF16), pack_c(-c_im.astype(F32)).astype(BF16),
            as_arr, pw_arr)


def _s5(u_sb, packed, d_row, glu_w, glu_b, batch, seq, out_dtype):
    tt = min(TM_S5, seq)
    rows = tt * batch
    bre, bim, cre, cim, as_arr, pw_arr = packed
    w = BRANCH_WIDTH
    e = np.zeros((batch, rows, tt), np.float32)
    for b in range(batch):
        e[b, np.arange(tt) * batch + b, np.arange(tt)] = 1.0
    et = jnp.asarray(e.transpose(0, 2, 1), dtype=BF16)
    e = jnp.asarray(e, dtype=BF16)
    full = lambda *shape: pl.BlockSpec(shape, lambda i: (0,) * len(shape))
    return pl.pallas_call(
        functools.partial(_s5_kernel, rows=rows, batch=batch),
        grid=(seq // tt,),
        in_specs=[
            pl.BlockSpec((tt, batch * w), lambda i: (i, 0)),
            full(batch, rows, tt), full(batch, tt, rows),
            full(S5_BLK, LANES, S5_BLK_STATE), full(S5_BLK, LANES, S5_BLK_STATE),
            full(S5_BLK, S5_BLK_STATE, LANES), full(S5_BLK, S5_BLK_STATE, LANES),
            full(*as_arr.shape), full(2, SUBLANES, S5_NSTATE),
            full(1, w), full(w, w), full(1, w),
        ],
        out_specs=pl.BlockSpec((tt, batch * w), lambda i: (i, 0)),
        out_shape=jax.ShapeDtypeStruct((seq, batch * w), out_dtype),
        scratch_shapes=[
            pltpu.VMEM((rows, S5_NSTATE), F32),
            pltpu.VMEM((rows, S5_NSTATE), F32),
            pltpu.VMEM((SUBLANES, S5_NSTATE), F32),
            pltpu.VMEM((SUBLANES, S5_NSTATE), F32),
        ],
        compiler_params=_cparams(("arbitrary",)),
        name="s5",
    )(u_sb, e, et, bre, bim, cre, cim, as_arr, pw_arr, d_row, glu_w, glu_b)


def _merge_kernel(x_ref, mod_ref, g_ref, oa_ref, ob_ref, oc_ref, od_ref, wg_ref, bg_ref,
                  wb_ref, wo_ref, out_ref, h_scr, acc_scr):
    n = pl.program_id(1)

    @pl.when(n == 0)
    def _():
        _modulated_norm_into(h_scr, x_ref, g_ref[0:1, :], mod_ref[0:1, :], mod_ref[1:2, :])
        acc_scr[...] = jnp.zeros_like(acc_scr)

    hb = h_scr[...]
    merged = None
    for i, o_ref in enumerate((oa_ref, ob_ref, oc_ref, od_ref)):
        gate = _sigmoid(_dot(hb, wg_ref[i]) + bg_ref[i])
        term = gate * _dot(o_ref[...].astype(BF16), wb_ref[i])
        merged = term if merged is None else merged + term
    acc_scr[...] += _dot(merged.astype(BF16), wo_ref[...])

    @pl.when(n == pl.num_programs(1) - 1)
    def _():
        _gated_residual_into(out_ref, x_ref, acc_scr, g_ref[1:2, :], mod_ref[2:3, :])


def _merge(x2, mod_l, gains, oa, ob, oc, od, w_gate, b_gate, w_branch, w_out, seq):
    t, d = x2.shape
    tm, tn = min(TM_MERGE, seq), TN_MERGE
    tiles_per_seq = seq // tm
    w = BRANCH_WIDTH
    br_spec = pl.BlockSpec((tm, w), lambda i, n: (i, 0))
    return pl.pallas_call(
        _merge_kernel,
        grid=(t // tm, d // tn),
        in_specs=[
            pl.BlockSpec((tm, d), lambda i, n: (i, 0)),
            pl.BlockSpec((None, SUBLANES, d), lambda i, n: (i // tiles_per_seq, 0, 0)),
            pl.BlockSpec((2, d), lambda i, n: (0, 0)),
            br_spec, br_spec, br_spec,
            pl.BlockSpec((tm, w), lambda i, n: (i % tiles_per_seq, i // tiles_per_seq)),
            pl.BlockSpec((N_BRANCH, d, tn), lambda i, n: (0, 0, n)),
            pl.BlockSpec((N_BRANCH, 1, tn), lambda i, n: (0, 0, n)),
            pl.BlockSpec((N_BRANCH, w, tn), lambda i, n: (0, 0, n)),
            pl.BlockSpec((tn, d), lambda i, n: (n, 0)),
        ],
        out_specs=pl.BlockSpec((tm, d), lambda i, n: (i, 0)),
        out_shape=jax.ShapeDtypeStruct((t, d), F32),
        scratch_shapes=[pltpu.VMEM((tm, d), BF16), pltpu.VMEM((tm, d), F32)],
        compiler_params=_cparams(("parallel", "arbitrary")),
        name="merge",
    )(x2, mod_l, gains, oa, ob, oc, od, w_gate, b_gate, w_branch, w_out)


def _ffn_kernel(x_ref, mod_ref, g_ref, wa_ref, wg_ref, wo_ref, out_ref, h_scr, acc_scr):
    j = pl.program_id(1)

    @pl.when(j == 0)
    def _():
        _modulated_norm_into(h_scr, x_ref, g_ref[0:1, :], mod_ref[3:4, :], mod_ref[4:5, :])
        acc_scr[...] = jnp.zeros_like(acc_scr)

    hb = h_scr[...]
    a = _dot(hb, wa_ref[...])
    g = _dot(hb, wg_ref[...])
    acc_scr[...] += _dot((_silu(a) * g).astype(BF16), wo_ref[...])

    @pl.when(j == pl.num_programs(1) - 1)
    def _():
        _gated_residual_into(out_ref, x_ref, acc_scr, g_ref[1:2, :], mod_ref[5:6, :])


def _ffn(x2, mod_l, gains, w_in, w_out, seq):
    t, d = x2.shape
    tm, th = min(TM_FFN, seq), TH_FFN
    tiles_per_seq = seq // tm
    nh = FFN_HIDDEN // th
    return pl.pallas_call(
        _ffn_kernel,
        grid=(t // tm, nh),
        in_specs=[
            pl.BlockSpec((tm, d), lambda i, j: (i, 0)),
            pl.BlockSpec((None, SUBLANES, d), lambda i, j: (i // tiles_per_seq, 0, 0)),
            pl.BlockSpec((2, d), lambda i, j: (0, 0)),
            pl.BlockSpec((d, th), lambda i, j: (0, j)),
            pl.BlockSpec((d, th), lambda i, j: (0, j + nh)),
            pl.BlockSpec((th, d), lambda i, j: (j, 0)),
        ],
        out_specs=pl.BlockSpec((tm, d), lambda i, j: (i, 0)),
        out_shape=jax.ShapeDtypeStruct((t, d), F32),
        scratch_shapes=[pltpu.VMEM((tm, d), BF16), pltpu.VMEM((tm, d), F32)],
        compiler_params=_cparams(("parallel", "arbitrary")),
        name="ffn",
    )(x2, mod_l, gains, w_in, w_in, w_out)


def _split_w_in(w):
    d = w.shape[0]
    main = jnp.concatenate([w[:, :3584], w[:, 3592:5128], w[:, 5144:5656]], axis=1)
    small = jnp.concatenate([w[:, 3584:3592], w[:, 5128:5144],
                             jnp.zeros((d, SMALL_W - 2 * ML_HEADS - GLA_RANK), w.dtype)], axis=1)
    return main.astype(BF16), small.astype(BF16)


def _pad_row(vals, offset):
    row = jnp.zeros((1, SMALL_W), F32)
    return lax.dynamic_update_slice(row, vals.reshape(1, -1).astype(F32), (0, offset))


ACT_DTYPE = BF16


def kernel(x, c, ada_w, ada_b, norm_g, w_in, rel_bias, diff_lambda, ml_conv, ml_gate_b,
           gla_wa2, gla_ba, s5_a_re, s5_a_im, s5_log_dt, s5_b_re, s5_b_im, s5_c_re, s5_c_im,
           s5_d, s5_glu_w, s5_glu_b, w_branch, w_gate, b_gate, w_out, ffn_w_in, ffn_w_out):
    batch, seq, d = x.shape
    depth = ada_w.shape[0]
    t = batch * seq

    c_pad = jnp.concatenate([c, jnp.zeros((SUBLANES - batch, d), c.dtype)], axis=0)
    mod = _adaln(c_pad, ada_w, ada_b)[:, :batch]
    mod = mod.reshape(depth, batch, N_MOD, d)
    mod = jnp.concatenate([mod, jnp.zeros((depth, batch, SUBLANES - N_MOD, d), F32)], axis=2)

    bias_tiles = _bias_tiles(rel_bias, min(TQ_ATT, seq))

    x2 = x.reshape(t, d)
    for l in range(depth):
        w_main, w_small = _split_w_in(w_in[l])
        pm, ps, pu = _proj(x2, mod[l], norm_g[l, 0:1], w_main, w_small, seq, ACT_DTYPE)

        lam_init = 0.8 - 0.6 * math.exp(-0.3 * l)
        lp = diff_lambda[l].astype(F32)
        lam = (jnp.exp(jnp.sum(lp[0] * lp[1])) - jnp.exp(jnp.sum(lp[2] * lp[3])) + lam_init)
        o_a = _attention(lam.reshape(1), pm, bias_tiles, batch, seq, lam_init, BF16)

        gate_row = (_pad_row(ml_gate_b[l, 0], SM_ML_I) + _pad_row(ml_gate_b[l, 1], SM_ML_F))
        o_b = _mlstm(pm, ps, ml_conv[l].astype(F32), gate_row, batch, seq, BF16)

        wa_pad = jnp.zeros((SMALL_W, GLA_HEADS * GLA_DK), F32)
        wa_pad = lax.dynamic_update_slice(wa_pad, gla_wa2[l].astype(F32), (SM_GL_A, 0))
        o_c = _gla(pm, ps, wa_pad, gla_ba[l].reshape(1, -1).astype(F32), batch, seq, BF16)

        packed = _s5_params(s5_a_re[l], s5_a_im[l], s5_log_dt[l], s5_b_re[l], s5_b_im[l],
                            s5_c_re[l], s5_c_im[l], batch)
        o_d = _s5(pu, packed, s5_d[l].reshape(1, -1).astype(F32),
                  s5_glu_w[l].astype(BF16), s5_glu_b[l].reshape(1, -1).astype(F32), batch, seq, BF16)

        x2 = _merge(x2, mod[l], norm_g[l, 0:2], o_a, o_b, o_c, o_d,
                    w_gate[l].astype(BF16), b_gate[l].reshape(N_BRANCH, 1, d).astype(F32),
                    w_branch[l].astype(BF16), w_out[l].astype(BF16), seq)
        x2 = _ffn(x2, mod[l], norm_g[l, 2:4], ffn_w_in[l].astype(BF16),
                  ffn_w_out[l].astype(BF16), seq)
    return x2.reshape(batch, seq, d)
```
